```python
import math
import jax, jax.numpy as jnp
from jax import lax
import numpy as np

D_MODEL = 2048
BATCH = 2
SEQ = 16384
DEPTH = 2

GRID_W = 64
CTX_LEN = 256
N_MIXERS = 2
EPS = 1e-6
GLA_HEADS = 4
GLA_KEY_DIM = D_MODEL // 2
GLA_VAL_DIM = D_MODEL
GLA_DK = GLA_KEY_DIM // GLA_HEADS
GLA_DV = GLA_VAL_DIM // GLA_HEADS
GLA_GATE_RANK = 16
GLA_GATE_NORM = 16.0
GLA_CHUNK = 64
GLA_IN_DIM = 2 * GLA_KEY_DIM + 2 * GLA_VAL_DIM
DIFF_HEAD_DIM = 128
DIFF_HEADS = D_MODEL // (2 * DIFF_HEAD_DIM)
DIFF_VAL_DIM = 2 * DIFF_HEAD_DIM
DIFF_WIDTH = DIFF_HEADS * DIFF_VAL_DIM
DIFF_IN_DIM = 4 * DIFF_WIDTH
Q_BLOCK = 128
ROPE_BASE = 10000.0

kernel_name = "hybrid_gla_diffattn_prefix_dit"


def rmsnorm(x, w):
    xf = x.astype(jnp.float32)
    y = xf * lax.rsqrt(jnp.mean(xf * xf, axis=-1, keepdims=True) + EPS)
    return (y * w.astype(jnp.float32)).astype(x.dtype)


def adaln(cond, w, b):
    m = jax.nn.silu(cond) @ w + b
    return jnp.split(m, 3, axis=-1)


def axial_rope_tables(n_tokens):
    rows_n = n_tokens // GRID_W
    row = jnp.repeat(jnp.arange(rows_n), GRID_W).astype(jnp.float32)
    col = jnp.tile(jnp.arange(GRID_W), rows_n).astype(jnp.float32)
    half = DIFF_HEAD_DIM // 2
    inv_freq = ROPE_BASE ** (-jnp.arange(0, half, 2, dtype=jnp.float32) / half)
    ang_r = row[:, None] * inv_freq
    ang_c = col[:, None] * inv_freq
    ang = jnp.concatenate([ang_r, ang_r, ang_c, ang_c], axis=-1)
    return jnp.cos(ang), jnp.sin(ang)


def apply_axial_rope(t, cos, sin):
    q4 = DIFF_HEAD_DIM // 4
    x1, x2, x3, x4 = t[..., :q4], t[..., q4:2 * q4], t[..., 2 * q4:3 * q4], t[..., 3 * q4:]
    rot = jnp.concatenate([-x2, x1, -x4, x3], axis=-1)
    cb = cos[None, :, None, None, :]
    sb = sin[None, :, None, None, :]
    return (t * cb + rot * sb).astype(t.dtype)


def gla_scan(q, k, v, log_a, s0):
    b_, t_, h_, _ = q.shape
    dv = v.shape[-1]
    n = t_ // GLA_CHUNK

    def chunks(t):
        return t.astype(jnp.float32).reshape(b_, n, GLA_CHUNK, h_, t.shape[-1]).transpose(1, 0, 3, 2, 4)

    lower = jnp.tril(jnp.ones((GLA_CHUNK, GLA_CHUNK), dtype=bool))

    def step(S, inp):
        qc, kc, vc, ac = inp
        bcum = jnp.cumsum(ac, axis=-2)
        o_inter = jnp.einsum('bhid,bhde->bhie', qc * jnp.exp(bcum), S)
        rel = bcum[:, :, :, None, :] - bcum[:, :, None, :, :]
        decay = jnp.exp(jnp.where(lower[:, :, None], rel, -jnp.inf))
        scores = jnp.einsum('bhid,bhjd,bhijd->bhij', qc, kc, decay)
        o_intra = jnp.einsum('bhij,bhje->bhie', scores, vc)
        b_last = bcum[:, :, -1:, :]
        S_new = jnp.exp(b_last[:, :, 0, :])[..., None] * S + jnp.einsum(
            'bhjd,bhje->bhde', kc * jnp.exp(b_last - bcum), vc)
        return S_new, o_inter + o_intra

    s_fin, o = lax.scan(step, s0, (chunks(q), chunks(k), chunks(v), chunks(log_a)))
    o = o.transpose(1, 0, 3, 2, 4).reshape(b_, t_, h_, dv)
    return o, s_fin


def gla_bidir(q, k, v, la_f, la_b, s0_f, s0_b):
    o_f, s_f = gla_scan(q, k, v, la_f, s0_f)
    fl = lambda t: jnp.flip(t, axis=1)
    o_b, s_b = gla_scan(fl(q), fl(k), fl(v), fl(la_b), s0_b)
    return o_f + fl(o_b), s_f, s_b


def gla_mixer(h, hc, w_in, gate_a1, gate_a2, gate_b, gn_w, w_out, need_ctx):
    def project(t):
        bt, tt = t.shape[:2]
        p = t @ w_in
        q, k, v, g = jnp.split(p, [GLA_KEY_DIM, 2 * GLA_KEY_DIM, 2 * GLA_KEY_DIM + GLA_VAL_DIM], axis=-1)
        q = q.reshape(bt, tt, GLA_HEADS, GLA_DK) * (GLA_DK ** -0.5)
        k = k.reshape(bt, tt, GLA_HEADS, GLA_DK)
        v = v.reshape(bt, tt, GLA_HEADS, GLA_DV)
        tf = t.astype(jnp.float32)
        las = [(jax.nn.log_sigmoid((tf @ gate_a1[d]) @ gate_a2[d] + gate_b[d]) / GLA_GATE_NORM
                ).reshape(bt, tt, GLA_HEADS, GLA_DK) for d in range(2)]
        return q, k, v, g, las[0], las[1]

    def finish(o, g):
        bt, tt = o.shape[:2]
        o = rmsnorm(o, gn_w).reshape(bt, tt, GLA_VAL_DIM).astype(g.dtype)
        return (o * jax.nn.silu(g)) @ w_out

    b_ = h.shape[0]
    zeros = jnp.zeros((b_, GLA_HEADS, GLA_DK, GLA_DV), jnp.float32)
    qc, kc, vc, gc, lafc, labc = project(hc)
    o_c, s_f, s_b = gla_bidir(qc, kc, vc, lafc, labc, zeros, zeros)
    q, k, v, g, laf, lab = project(h)
    o, _, _ = gla_bidir(q, k, v, laf, lab, s_f, s_b)
    y = finish(o, g)
    yc = finish(o_c, gc) if need_ctx else None
    return y, yc


def diff_attend(qb, k_all, v_all, lam):
    s = jnp.einsum('bqhmd,bkhmd->bhmqk', qb, k_all) * (DIFF_HEAD_DIM ** -0.5)
    p = jax.nn.softmax(s, axis=-1)
    a = p[:, :, 0] - lam * p[:, :, 1]
    return jnp.einsum('bhqk,bkhe->bqhe', a, v_all)


def diff_mixer(h, hc, w_in, qn_w, kn_w, lam_v, subln_w, w_out, lambda_init, cos, sin, need_ctx):
    def project(t):
        bt, tt = t.shape[:2]
        q, k, v, g = jnp.split(t @ w_in, 4, axis=-1)
        q = rmsnorm(q.reshape(bt, tt, DIFF_HEADS, 2, DIFF_HEAD_DIM), qn_w)
        k = rmsnorm(k.reshape(bt, tt, DIFF_HEADS, 2, DIFF_HEAD_DIM), kn_w)
        v = v.reshape(bt, tt, DIFF_HEADS, DIFF_VAL_DIM)
        return q, k, v, g

    lv = lam_v.astype(jnp.float32)
    lam = jnp.exp(jnp.sum(lv[0] * lv[1])) - jnp.exp(jnp.sum(lv[2] * lv[3])) + lambda_init

    q, k, v, g = project(h)
    q = apply_axial_rope(q, cos, sin)
    k = apply_axial_rope(k, cos, sin)
    qc, kc, vc, gc = project(hc)
    f32 = jnp.float32
    k_all = jnp.concatenate([k, kc], axis=1).astype(f32)
    v_all = jnp.concatenate([v, vc], axis=1).astype(f32)

    b_, s_ = h.shape[:2]
    nblk = s_ // Q_BLOCK
    q_blocks = q.astype(f32).reshape(b_, nblk, Q_BLOCK, DIFF_HEADS, 2, DIFF_HEAD_DIM).transpose(1, 0, 2, 3, 4, 5)
    o = lax.map(lambda qb: diff_attend(qb, k_all, v_all, lam), q_blocks)
    o = o.transpose(1, 0, 2, 3, 4).reshape(b_, s_, DIFF_HEADS, DIFF_VAL_DIM)

    def finish(o, g):
        bt, tt = o.shape[:2]
        o = (rmsnorm(o, subln_w) * (1.0 - lambda_init)).reshape(bt, tt, DIFF_WIDTH).astype(g.dtype)
        return (o * jax.nn.silu(g)) @ w_out

    y = finish(o, g)
    yc = None
    if need_ctx:
        o_c = diff_attend(qc.astype(f32), kc.astype(f32), vc.astype(f32), lam)
        yc = finish(o_c, gc)
    return y, yc


def setup_inputs(seed: int = 0) -> dict:
    key = jax.random.key(seed)
    ks = jax.random.split(key, 24)
    n_a = (DEPTH + N_MIXERS - 1) // N_MIXERS
    n_b = DEPTH // N_MIXERS
    nrm = jax.random.normal
    f32 = jnp.float32
    D = D_MODEL
    return {
        "x": nrm(ks[0], (BATCH, SEQ, D), f32),
        "c": nrm(ks[1], (BATCH, D), f32),
        "ctx": nrm(ks[2], (BATCH, CTX_LEN, D), f32),
        "c_ctx": nrm(ks[3], (D,), f32),
        "norm_w": 1.0 + 0.02 * nrm(ks[4], (DEPTH, D), f32),
        "ada_w": nrm(ks[5], (DEPTH, D, 3 * D), f32) * D ** -0.5,
        "ada_b": 0.02 * nrm(ks[6], (DEPTH, 3 * D), f32),
        "gla_w_in": nrm(ks[7], (n_a, D, GLA_IN_DIM), f32) * D ** -0.5,
        "gla_gate_a1": nrm(ks[8], (n_a, 2, D, GLA_GATE_RANK), f32) * D ** -0.5,
        "gla_gate_a2": nrm(ks[9], (n_a, 2, GLA_GATE_RANK, GLA_KEY_DIM), f32) * GLA_GATE_RANK ** -0.5,
        "gla_gate_b": 0.1 * nrm(ks[10], (n_a, 2, GLA_KEY_DIM), f32),
        "gla_gn_w": 1.0 + 0.02 * nrm(ks[11], (n_a, GLA_DV), f32),
        "gla_w_out": nrm(ks[12], (n_a, GLA_VAL_DIM, D), f32) * GLA_VAL_DIM ** -0.5,
        "diff_w_in": nrm(ks[13], (n_b, D, DIFF_IN_DIM), f32) * D ** -0.5,
        "diff_qn_w": 1.0 + 0.02 * nrm(ks[14], (n_b, 2, DIFF_HEAD_DIM), f32),
        "diff_kn_w": 1.0 + 0.02 * nrm(ks[15], (n_b, 2, DIFF_HEAD_DIM), f32),
        "diff_lam": 0.1 * nrm(ks[16], (n_b, 4, DIFF_HEAD_DIM), f32),
        "diff_subln_w": 1.0 + 0.02 * nrm(ks[17], (n_b, DIFF_VAL_DIM), f32),
        "diff_w_out": nrm(ks[18], (n_b, DIFF_WIDTH, D), f32) * DIFF_WIDTH ** -0.5,
    }


def reference(x, c, ctx, c_ctx, norm_w, ada_w, ada_b,
              gla_w_in, gla_gate_a1, gla_gate_a2, gla_gate_b, gla_gn_w, gla_w_out,
              diff_w_in, diff_qn_w, diff_kn_w, diff_lam, diff_subln_w, diff_w_out):
    cos, sin = axial_rope_tables(x.shape[1])
    for i in range(DEPTH):
        last = i == DEPTH - 1
        j = i // N_MIXERS
        shift, scale, gate = adaln(c, ada_w[i], ada_b[i])
        shift_c, scale_c, gate_c = adaln(c_ctx, ada_w[i], ada_b[i])
        h = rmsnorm(x, norm_w[i]) * (1.0 + scale[:, None]) + shift[:, None]
        hc = rmsnorm(ctx, norm_w[i]) * (1.0 + scale_c) + shift_c
        if i % N_MIXERS == 0:
            y, yc = gla_mixer(h, hc, gla_w_in[j], gla_gate_a1[j], gla_gate_a2[j], gla_gate_b[j],
                              gla_gn_w[j], gla_w_out[j], not last)
        else:
            lambda_init = 0.8 - 0.6 * math.exp(-0.3 * i)
            y, yc = diff_mixer(h, hc, diff_w_in[j], diff_qn_w[j], diff_kn_w[j], diff_lam[j],
                               diff_subln_w[j], diff_w_out[j], lambda_init, cos, sin, not last)
        x = x + gate[:, None] * y
        if not last:
            ctx = ctx + gate_c * yc
    return x
```

```python
import functools
import math

import jax
import jax.numpy as jnp
from jax import lax
from jax.experimental import pallas as pl
from jax.experimental.pallas import tpu as pltpu

F32 = jnp.float32
BF16 = jnp.bfloat16

D_MODEL = 2048
DEPTH = 2
GRID_W = 64
EPS = 1e-6
GLA_HEADS = 4
GLA_KEY_DIM = D_MODEL // 2
GLA_VAL_DIM = D_MODEL
GLA_DK = GLA_KEY_DIM // GLA_HEADS
GLA_DV = GLA_VAL_DIM // GLA_HEADS
GLA_GATE_RANK = 16
GLA_GATE_NORM = 16.0
GLA_CHUNK = 64
GLA_IN_DIM = 2 * GLA_KEY_DIM + 2 * GLA_VAL_DIM
DIFF_HEAD_DIM = 128
DIFF_HEADS = D_MODEL // (2 * DIFF_HEAD_DIM)
DIFF_VAL_DIM = 2 * DIFF_HEAD_DIM
DIFF_WIDTH = DIFF_HEADS * DIFF_VAL_DIM
DIFF_IN_DIM = 4 * DIFF_WIDTH
ROPE_BASE = 10000.0

V7X_LANES = 128
V7X_SUBLANES = 8
V7X_VMEM_BYTES = 64 * 1024 * 1024
VMEM_LIMIT = 56 * 1024 * 1024

GLA_BLOCK = 256
ADA_ROWS = V7X_SUBLANES


def _dot(a, b):
    return jnp.dot(a, b, preferred_element_type=F32)


def _dot_nt(a, b):
    return lax.dot_general(a, b, (((1,), (1,)), ((), ())), preferred_element_type=F32)


def _params(sem):
    return pltpu.CompilerParams(dimension_semantics=sem, vmem_limit_bytes=VMEM_LIMIT)


def _ada_kernel(c_ref, w_ref, b_ref, o_ref):
    c = c_ref[...]
    s = c * jax.nn.sigmoid(c)
    o_ref[...] = jnp.dot(s, w_ref[...], preferred_element_type=F32,
                         precision=lax.Precision.HIGHEST) + b_ref[...]


def _ada_call(cond, ada_w, ada_b):
    tn = 1024
    n = ada_w.shape[-1]
    return pl.pallas_call(
        _ada_kernel,
        grid=(DEPTH, n // tn),
        in_specs=[
            pl.BlockSpec((ADA_ROWS, D_MODEL), lambda l, j: (0, 0)),
            pl.BlockSpec((None, D_MODEL, tn), lambda l, j: (l, 0, j)),
            pl.BlockSpec((None, 1, tn), lambda l, j: (l, 0, j)),
        ],
        out_specs=pl.BlockSpec((None, ADA_ROWS, tn), lambda l, j: (l, 0, j)),
        out_shape=jax.ShapeDtypeStruct((DEPTH, ADA_ROWS, n), F32),
        compiler_params=_params(("arbitrary", "arbitrary")),
        name="ada_mod",
    )(cond, ada_w, ada_b.reshape(DEPTH, 1, n))


def _normalize_into(x_ref, shift_ref, scale_ref, nw_ref, hn_ref):
    x = x_ref[...]
    ms = jnp.mean(x * x, axis=-1, keepdims=True)
    a = nw_ref[...] * (1.0 + scale_ref[...])
    hn_ref[...] = (x * lax.rsqrt(ms + EPS) * a + shift_ref[...]).astype(BF16)


def _gla_inproj_kernel(x_ref, shift_ref, scale_ref, nw_ref, w_ref, a1_ref, a2_ref, gb_ref,
                       *rest):
    out_ref, la_ref, hn_ref = rest[-3:]
    j = pl.program_id(2)

    @pl.when(j == 0)
    def _():
        _normalize_into(x_ref, shift_ref, scale_ref, nw_ref, hn_ref)
        z = _dot(hn_ref[...], a1_ref[...])
        lg = _dot(z.astype(BF16), a2_ref[...]) + gb_ref[...]
        log_sig = jnp.minimum(lg, 0.0) - jnp.log(1.0 + jnp.exp(-jnp.abs(lg)))
        la_ref[...] = log_sig * (1.0 / GLA_GATE_NORM)

    out_ref[...] = _dot(hn_ref[...], w_ref[...]).astype(BF16)


def _rope_rotate(t, cos, sin_signed, first_of_pair):
    partner = jnp.where(first_of_pair, pltpu.roll(t, 96, 1), pltpu.roll(t, 32, 1))
    return t * cos + partner * sin_signed


def _diff_inproj_kernel(x_ref, shift_ref, scale_ref, nw_ref, w_ref, qn_ref, kn_ref,
                        cos_ref, sin_ref, *rest, rope, tn, q_scale):
    out_ref, hn_ref = rest[-2:]
    j = pl.program_id(2)
    tiles_per_part = DIFF_WIDTH // tn

    @pl.when(j == 0)
    def _():
        _normalize_into(x_ref, shift_ref, scale_ref, nw_ref, hn_ref)

    acc = _dot(hn_ref[...], w_ref[...])

    def head_norm(nrm_ref, scale):
        lane = lax.broadcasted_iota(jnp.int32, (1, DIFF_HEAD_DIM), 1)
        first_of_pair = (lane // (DIFF_HEAD_DIM // 4)) % 2 == 0
        if rope:
            cos = cos_ref[...]
            sin_signed = jnp.where(first_of_pair, -sin_ref[...], sin_ref[...])
        for g in range(tn // DIFF_HEAD_DIM):
            sl = slice(g * DIFF_HEAD_DIM, (g + 1) * DIFF_HEAD_DIM)
            t = acc[:, sl]
            ms = jnp.mean(t * t, axis=-1, keepdims=True)
            t = t * lax.rsqrt(ms + EPS) * nrm_ref[g % 2:g % 2 + 1, :]
            if rope:
                t = _rope_rotate(t, cos, sin_signed, first_of_pair)
            if scale != 1.0:
                t = t * scale
            out_ref[:, sl] = t.astype(BF16)

    @pl.when(j < tiles_per_part)
    def _():
        head_norm(qn_ref, q_scale)

    @pl.when((j >= tiles_per_part) & (j < 2 * tiles_per_part))
    def _():
        head_norm(kn_ref, 1.0)

    @pl.when(j >= 2 * tiles_per_part)
    def _():
        out_ref[...] = acc.astype(BF16)


def _inproj_call(kernel, x, shift, scale, nw, w, extras, extra_specs, *, t_total, row_off,
                 tm, tn, extra_out=None, alias_in=None, name):
    b, t, d = x.shape
    n = w.shape[-1]
    blk_off = row_off // tm
    assert row_off % tm == 0 and t % tm == 0 and n % tn == 0
    in_specs = [
        pl.BlockSpec((None, tm, d), lambda bi, i, j: (bi, i, 0)),
        pl.BlockSpec((None, 1, d), lambda bi, i, j: (bi, 0, 0)),
        pl.BlockSpec((None, 1, d), lambda bi, i, j: (bi, 0, 0)),
        pl.BlockSpec((1, d), lambda bi, i, j: (0, 0)),
        pl.BlockSpec((d, tn), lambda bi, i, j: (0, j)),
    ] + extra_specs
    out_specs = [pl.BlockSpec((None, tm, tn), lambda bi, i, j: (bi, i + blk_off, j))]
    out_shape = [jax.ShapeDtypeStruct((b, t_total, n), BF16)]
    if extra_out is not None:
        width, dtype = extra_out
        out_specs.append(pl.BlockSpec((None, tm, width), lambda bi, i, j: (bi, i + blk_off, 0)))
        out_shape.append(jax.ShapeDtypeStruct((b, t_total, width), dtype))
    args = [x, shift, scale, nw, w] + extras
    aliases = {}
    if alias_in is not None:
        for k, arr in enumerate(alias_in):
            aliases[len(args)] = k
            args.append(arr)
            in_specs.append(pl.BlockSpec(memory_space=pl.ANY))
    return pl.pallas_call(
        kernel,
        grid=(b, t // tm, n // tn),
        in_specs=in_specs,
        out_specs=out_specs,
        out_shape=out_shape,
        scratch_shapes=[pltpu.VMEM((tm, d), BF16)],
        input_output_aliases=aliases,
        compiler_params=_params(("arbitrary", "arbitrary", "arbitrary")),
        name=name,
    )(*args)


def _gla_chunk_matrices():
    idx = jnp.arange(GLA_BLOCK)
    i = idx[:, None]
    j = idx[None, :]
    same = (i // GLA_CHUNK) == (j // GLA_CHUNK)
    start = (i // GLA_CHUNK) * GLA_CHUNK
    mats = []
    for direction in range(2):
        if direction == 0:
            a_q = same & (j <= i)
            ref = same & (j <= start + GLA_CHUNK // 2 - 1)
            b_k = same & (i > j)
        else:
            a_q = same & (j >= i)
            ref = same & (j >= start + GLA_CHUNK // 2)
            b_k = same & (i < j)
        a_t = a_q.astype(F32) - ref.astype(F32)
        mats.append(jnp.stack([a_q.astype(F32), a_t, b_k.astype(F32), same.astype(F32)]))
    return jnp.stack(mats).astype(BF16)


def _split_hi_lo(x):
    hi = x.astype(BF16)
    lo = (x - hi.astype(F32)).astype(BF16)
    return hi, lo


def _gla_direction(q_ref, k_ref, v_ref, la_ref, mats_ref, s_ref, o_ref, direction):
    n_chunks = GLA_BLOCK // GLA_CHUNK
    la = la_ref[...]
    la_hi, la_lo = _split_hi_lo(la)
    lat_hi, lat_lo = _split_hi_lo(la.T)
    a_q = mats_ref[direction, 0]
    a_t = mats_ref[direction, 1]
    b_k = mats_ref[direction, 2]
    same = mats_ref[direction, 3]

    e_q = _dot(a_q, la_hi) + _dot(a_q, la_lo)
    e_t = _dot(a_t, la_hi) + _dot(a_t, la_lo)
    et_k = _dot(lat_hi, b_k) + _dot(lat_lo, b_k)
    tot_t = _dot(lat_hi, same) + _dot(lat_lo, same)

    q = q_ref[...].astype(F32) * (GLA_DK ** -0.5)
    k = k_ref[...].astype(F32)
    v = v_ref[...]
    q_in = (q * jnp.exp(e_q)).astype(BF16)
    q_t = (q * jnp.exp(e_t)).astype(BF16)
    k_t = (k * jnp.exp(-e_t)).astype(BF16)
    kin_t = k.T * jnp.exp(et_k)

    scores = _dot_nt(q_t, k_t)
    scores = jnp.where(a_q > 0, scores, 0.0)
    o_intra = _dot(scores.astype(BF16), v)

    col = lax.broadcasted_iota(jnp.int32, (1, GLA_BLOCK), 1) // GLA_CHUNK
    order = range(n_chunks) if direction == 0 else range(n_chunks - 1, -1, -1)
    for c in order:
        rows = slice(c * GLA_CHUNK, (c + 1) * GLA_CHUNK)
        s = s_ref[...]
        o_c = _dot(q_in[rows], s.astype(BF16)) + o_intra[rows]
        o_ref[rows, :] = o_c.astype(BF16)
        upd = _dot(jnp.where(col == c, kin_t, 0.0).astype(BF16), v)
        decay = jnp.exp(tot_t[:, c * GLA_CHUNK:c * GLA_CHUNK + 1])
        s_ref[...] = decay * s + upd


def _gla_scan_kernel(qf_ref, kf_ref, vf_ref, laf_ref, qb_ref, kb_ref, vb_ref, lab_ref,
                     mats_ref, of_ref, ob_ref, sf_ref, sb_ref):
    @pl.when(pl.program_id(2) == 0)
    def _():
        sf_ref[...] = jnp.zeros_like(sf_ref)
        sb_ref[...] = jnp.zeros_like(sb_ref)

    _gla_direction(qf_ref, kf_ref, vf_ref, laf_ref, mats_ref, sf_ref, of_ref, 0)
    _gla_direction(qb_ref, kb_ref, vb_ref, lab_ref, mats_ref, sb_ref, ob_ref, 1)


def _gla_scan_call(qkvg, la, n_latent_blocks):
    b, t_total, _ = qkvg.shape
    nb = t_total // GLA_BLOCK
    ctx_blk = n_latent_blocks
    k_off = GLA_KEY_DIM // GLA_DK
    v_off = 2 * GLA_KEY_DIM // GLA_DV

    def pos_f(t):
        return jnp.where(t == 0, ctx_blk, t - 1)

    def pos_b(t):
        return jnp.where(t == 0, ctx_blk, n_latent_blocks - t)

    def specs(pos, la_col_off):
        return [
            pl.BlockSpec((None, GLA_BLOCK, GLA_DK), lambda bi, h, t: (bi, pos(t), h)),
            pl.BlockSpec((None, GLA_BLOCK, GLA_DK), lambda bi, h, t: (bi, pos(t), k_off + h)),
            pl.BlockSpec((None, GLA_BLOCK, GLA_DV), lambda bi, h, t: (bi, pos(t), v_off + h)),
            pl.BlockSpec((None, GLA_BLOCK, GLA_DK), lambda bi, h, t: (bi, pos(t), la_col_off + h)),
        ]

    mats = _gla_chunk_matrices()
    in_specs = specs(pos_f, 0) + specs(pos_b, GLA_HEADS) + [
        pl.BlockSpec(mats.shape, lambda bi, h, t: (0, 0, 0, 0))]
    out_spec_f = pl.BlockSpec((None, GLA_BLOCK, GLA_DV), lambda bi, h, t: (bi, pos_f(t), h))
    out_spec_b = pl.BlockSpec((None, GLA_BLOCK, GLA_DV), lambda bi, h, t: (bi, pos_b(t), h))
    o_shape = jax.ShapeDtypeStruct((b, t_total, GLA_VAL_DIM), BF16)
    return pl.pallas_call(
        _gla_scan_kernel,
        grid=(b, GLA_HEADS, nb),
        in_specs=in_specs,
        out_specs=[out_spec_f, out_spec_b],
        out_shape=[o_shape, o_shape],
        scratch_shapes=[pltpu.VMEM((GLA_DK, GLA_DV), F32), pltpu.VMEM((GLA_DK, GLA_DV), F32)],
        compiler_params=_params(("arbitrary", "arbitrary", "arbitrary")),
        name="gla_scan",
    )(qkvg, qkvg, qkvg, la, qkvg, qkvg, qkvg, la, mats)


def _finish_kernel(*refs, head_norm):
    if head_norm:
        x_ref, gate_ref, o1_ref, o2_ref, g_ref, gn_ref, w_ref, out_ref = refs
        o = o1_ref[...].astype(F32) + o2_ref[...].astype(F32)
        parts = []
        for h in range(GLA_HEADS):
            oh = o[:, h * GLA_DV:(h + 1) * GLA_DV]
            ms = jnp.mean(oh * oh, axis=-1, keepdims=True)
            parts.append(oh * lax.rsqrt(ms + EPS) * gn_ref[...])
        o = jnp.concatenate(parts, axis=-1)
    else:
        x_ref, gate_ref, o1_ref, g_ref, w_ref, out_ref = refs
        o = o1_ref[...].astype(F32)
    g = g_ref[...].astype(F32)
    y_in = (o * (g * jax.nn.sigmoid(g))).astype(BF16)
    y = _dot(y_in, w_ref[...])
    out_ref[...] = x_ref[...] + gate_ref[...] * y


def _finish_call(x, gate, o_list, g_arr, g_col_blk, w, *, row_off, gn_w=None, name):
    b, t, d = x.shape
    tm = 256
    blk_off = row_off // tm
    assert row_off % tm == 0 and t % tm == 0
    width = w.shape[0]
    in_specs = [
        pl.BlockSpec((None, tm, d), lambda bi, i: (bi, i, 0)),
        pl.BlockSpec((None, 1, d), lambda bi, i: (bi, 0, 0)),
    ]
    args = [x, gate]
    for o in o_list:
        in_specs.append(pl.BlockSpec((None, tm, width), lambda bi, i: (bi, i + blk_off, 0)))
        args.append(o)
    in_specs.append(pl.BlockSpec((None, tm, width), lambda bi, i: (bi, i + blk_off, g_col_blk)))
    args.append(g_arr)
    if gn_w is not None:
        in_specs.append(pl.BlockSpec((1, gn_w.shape[-1]), lambda bi, i: (0, 0)))
        args.append(gn_w)
    in_specs.append(pl.BlockSpec((width, d), lambda bi, i: (0, 0)))
    args.append(w)
    return pl.pallas_call(
        functools.partial(_finish_kernel, head_norm=gn_w is not None),
        grid=(b, t // tm),
        in_specs=in_specs,
        out_specs=pl.BlockSpec((None, tm, d), lambda bi, i: (bi, i, 0)),
        out_shape=jax.ShapeDtypeStruct((b, t, d), F32),
        compiler_params=_params(("arbitrary", "arbitrary")),
        name=name,
    )(*args)


def _attn_kernel(q1_ref, q2_ref, k1_ref, k2_ref, v_ref, lam_ref, sw_ref, o_ref,
                 m_ref, l_ref, acc_ref, *, tk, lambda_init):
    n_keys = v_ref.shape[0]
    m_ref[...] = jnp.full_like(m_ref, -1e30)
    l_ref[...] = jnp.zeros_like(l_ref)
    acc_ref[...] = jnp.zeros_like(acc_ref)

    def body(c, carry):
        off = pl.multiple_of(c * tk, tk)
        vc = v_ref[pl.ds(off, tk), :]
        for mp, (q_ref, k_ref) in enumerate(((q1_ref, k1_ref), (q2_ref, k2_ref))):
            s = _dot_nt(q_ref[...], k_ref[pl.ds(off, tk), :])
            m_old = m_ref[mp]
            m_new = jnp.maximum(m_old, jnp.max(s, axis=1, keepdims=True))
            alpha = jnp.exp2(m_old - m_new)
            p = jnp.exp2(s - m_new)
            l_ref[mp] = alpha * l_ref[mp] + jnp.sum(p, axis=1, keepdims=True)
            acc_ref[mp] = alpha * acc_ref[mp] + _dot(p.astype(BF16), vc)
            m_ref[mp] = m_new
        return carry

    lax.fori_loop(0, n_keys // tk, body, 0)

    lv = lam_ref[...]
    lam = (jnp.exp(jnp.sum(lv[0:1] * lv[1:2], axis=-1, keepdims=True))
           - jnp.exp(jnp.sum(lv[2:3] * lv[3:4], axis=-1, keepdims=True)) + lambda_init)
    o = acc_ref[0] / l_ref[0] - lam * (acc_ref[1] / l_ref[1])
    ms = jnp.mean(o * o, axis=-1, keepdims=True)
    o = o * lax.rsqrt(ms + EPS) * sw_ref[...] * (1.0 - lambda_init)
    o_ref[...] = o.astype(BF16)


def _attn_call(qkvg, n_query, lam_v, subln_w, lambda_init):
    b, n_keys, _ = qkvg.shape
    tq, tk = 512, 1280
    assert n_query % tq == 0 and n_keys % tk == 0
    d = DIFF_HEAD_DIM
    k_off = DIFF_WIDTH // d
    v_off = 2 * DIFF_WIDTH // DIFF_VAL_DIM
    in_specs = [
        pl.BlockSpec((None, tq, d), lambda bi, h, i: (bi, i, 2 * h)),
        pl.BlockSpec((None, tq, d), lambda bi, h, i: (bi, i, 2 * h + 1)),
        pl.BlockSpec((None, n_keys, d), lambda bi, h, i: (bi, 0, k_off + 2 * h)),
        pl.BlockSpec((None, n_keys, d), lambda bi, h, i: (bi, 0, k_off + 2 * h + 1)),
        pl.BlockSpec((None, n_keys, DIFF_VAL_DIM), lambda bi, h, i: (bi, 0, v_off + h)),
        pl.BlockSpec(lam_v.shape, lambda bi, h, i: (0, 0)),
        pl.BlockSpec((1, DIFF_VAL_DIM), lambda bi, h, i: (0, 0)),
    ]
    return pl.pallas_call(
        functools.partial(_attn_kernel, tk=tk, lambda_init=lambda_init),
        grid=(b, DIFF_HEADS, n_query // tq),
        in_specs=in_specs,
        out_specs=pl.BlockSpec((None, tq, DIFF_VAL_DIM), lambda bi, h, i: (bi, i, h)),
        out_shape=jax.ShapeDtypeStruct((b, n_query, DIFF_WIDTH), BF16),
        scratch_shapes=[
            pltpu.VMEM((2, tq, 1), F32),
            pltpu.VMEM((2, tq, 1), F32),
            pltpu.VMEM((2, tq, DIFF_VAL_DIM), F32),
        ],
        compiler_params=_params(("arbitrary", "arbitrary", "arbitrary")),
        name="diff_attn",
    )(qkvg, qkvg, qkvg, qkvg, qkvg, lam_v, subln_w.reshape(1, DIFF_VAL_DIM))


def _axial_rope_tables(n_tokens):
    rows_n = n_tokens // GRID_W
    row = jnp.repeat(jnp.arange(rows_n), GRID_W).astype(F32)
    col = jnp.tile(jnp.arange(GRID_W), rows_n).astype(F32)
    half = DIFF_HEAD_DIM // 2
    inv_freq = ROPE_BASE ** (-jnp.arange(0, half, 2, dtype=F32) / half)
    ang_r = row[:, None] * inv_freq
    ang_c = col[:, None] * inv_freq
    ang = jnp.concatenate([ang_r, ang_r, ang_c, ang_c], axis=-1)
    return jnp.cos(ang), jnp.sin(ang)


def kernel(x, c, ctx, c_ctx, norm_w, ada_w, ada_b, gla_w_in, gla_gate_a1, gla_gate_a2, gla_gate_b,
           gla_gn_w, gla_w_out, diff_w_in, diff_qn_w, diff_kn_w, diff_lam, diff_subln_w, diff_w_out):
    b, seq, d = x.shape
    ctx_len = ctx.shape[1]
    t_total = seq + ctx_len
    assert ctx_len == GLA_BLOCK and seq % GLA_BLOCK == 0 and d == D_MODEL

    cond = jnp.zeros((ADA_ROWS, d), F32).at[:b].set(c).at[b].set(c_ctx)
    mod = _ada_call(cond, ada_w, ada_b)

    def mod_rows(layer, part):
        m = mod[layer, :, part * d:(part + 1) * d]
        lat = m[:b, None, :]
        cx = jnp.broadcast_to(m[b][None, None, :], (b, 1, d))
        return lat, cx

    shift, shift_c = mod_rows(0, 0)
    scale, scale_c = mod_rows(0, 1)
    gate, gate_c = mod_rows(0, 2)
    nw0 = norm_w[0].reshape(1, d)
    w_in0 = gla_w_in[0].astype(BF16)
    a1 = jnp.zeros((d, V7X_LANES), F32)
    a1 = a1.at[:, :GLA_GATE_RANK].set(gla_gate_a1[0, 0]).at[:, GLA_GATE_RANK:2 * GLA_GATE_RANK].set(
        gla_gate_a1[0, 1]).astype(BF16)
    a2 = jnp.zeros((V7X_LANES, 2 * GLA_KEY_DIM), F32)
    a2 = a2.at[:GLA_GATE_RANK, :GLA_KEY_DIM].set(gla_gate_a2[0, 0])
    a2 = a2.at[GLA_GATE_RANK:2 * GLA_GATE_RANK, GLA_KEY_DIM:].set(gla_gate_a2[0, 1]).astype(BF16)
    gb = gla_gate_b[0].reshape(1, 2 * GLA_KEY_DIM)
    gla_extras = [a1, a2, gb]
    gla_extra_specs = [
        pl.BlockSpec(a1.shape, lambda bi, i, j: (0, 0)),
        pl.BlockSpec(a2.shape, lambda bi, i, j: (0, 0)),
        pl.BlockSpec(gb.shape, lambda bi, i, j: (0, 0)),
    ]
    la_out = (2 * GLA_KEY_DIM, F32)
    qkvg0, la0 = _inproj_call(
        _gla_inproj_kernel, x, shift, scale, nw0, w_in0, gla_extras, gla_extra_specs,
        t_total=t_total, row_off=0, tm=512, tn=1024, extra_out=la_out, name="gla_inproj_x")
    qkvg0, la0 = _inproj_call(
        _gla_inproj_kernel, ctx, shift_c, scale_c, nw0, w_in0, gla_extras, gla_extra_specs,
        t_total=t_total, row_off=seq, tm=GLA_BLOCK, tn=1024, extra_out=la_out,
        alias_in=[qkvg0, la0], name="gla_inproj_ctx")

    o_f, o_b = _gla_scan_call(qkvg0, la0, seq // GLA_BLOCK)

    w_out0 = gla_w_out[0].astype(BF16)
    gn_w = gla_gn_w[0].reshape(1, GLA_DV)
    g_col_blk0 = (2 * GLA_KEY_DIM + GLA_VAL_DIM) // GLA_VAL_DIM
    x1 = _finish_call(x, gate, [o_f, o_b], qkvg0, g_col_blk0, w_out0, row_off=0, gn_w=gn_w,
                      name="gla_finish_x")
    ctx1 = _finish_call(ctx, gate_c, [o_f, o_b], qkvg0, g_col_blk0, w_out0, row_off=seq, gn_w=gn_w,
                        name="gla_finish_ctx")

    lambda_init = 0.8 - 0.6 * math.exp(-0.3 * 1)
    shift, shift_c = mod_rows(1, 0)
    scale, scale_c = mod_rows(1, 1)
    gate, _ = mod_rows(1, 2)
    nw1 = norm_w[1].reshape(1, d)
    w_in1 = diff_w_in[0].astype(BF16)
    cos, sin = _axial_rope_tables(seq)
    q_scale = math.log2(math.e) * DIFF_HEAD_DIM ** -0.5
    tm1, tn1 = 1024, 1024
    diff_extras = [diff_qn_w[0], diff_kn_w[0]]
    nrm_specs = [
        pl.BlockSpec((2, DIFF_HEAD_DIM), lambda bi, i, j: (0, 0)),
        pl.BlockSpec((2, DIFF_HEAD_DIM), lambda bi, i, j: (0, 0)),
    ]
    (qkvg1,) = _inproj_call(
        functools.partial(_diff_inproj_kernel, rope=True, tn=tn1, q_scale=q_scale),
        x1, shift, scale, nw1, w_in1, diff_extras + [cos, sin],
        nrm_specs + [pl.BlockSpec((tm1, DIFF_HEAD_DIM), lambda bi, i, j: (i, 0)),
                     pl.BlockSpec((tm1, DIFF_HEAD_DIM), lambda bi, i, j: (i, 0))],
        t_total=t_total, row_off=0, tm=tm1, tn=tn1, name="diff_inproj_x")
    (qkvg1,) = _inproj_call(
        functools.partial(_diff_inproj_kernel, rope=False, tn=tn1, q_scale=q_scale),
        ctx1, shift_c, scale_c, nw1, w_in1, diff_extras + [cos, sin],
        nrm_specs + [pl.BlockSpec((GLA_BLOCK, DIFF_HEAD_DIM), lambda bi, i, j: (0, 0)),
                     pl.BlockSpec((GLA_BLOCK, DIFF_HEAD_DIM), lambda bi, i, j: (0, 0))],
        t_total=t_total, row_off=seq, tm=GLA_BLOCK, tn=tn1, alias_in=[qkvg1],
        name="diff_inproj_ctx")

    o_attn = _attn_call(qkvg1, seq, diff_lam[0], diff_subln_w[0], lambda_init)

    w_out1 = diff_w_out[0].astype(BF16)
    g_col_blk1 = 3 * DIFF_WIDTH // DIFF_WIDTH
    return _finish_call(x1, gate, [o_attn], qkvg1, g_col_blk1, w_out1, row_off=0,
                        name="diff_finish_x")
```

```python
import functools
import math

import jax
import jax.numpy as jnp
from jax import lax
from jax.experimental import pallas as pl
from jax.experimental.pallas import tpu as pltpu

F32 = jnp.float32
BF16 = jnp.bfloat16

D_MODEL = 2048
DEPTH = 2
GRID_W = 64
EPS = 1e-6
GLA_HEADS = 4
GLA_KEY_DIM = D_MODEL // 2
GLA_VAL_DIM = D_MODEL
GLA_DK = GLA_KEY_DIM // GLA_HEADS
GLA_DV = GLA_VAL_DIM // GLA_HEADS
GLA_GATE_RANK = 16
GLA_GATE_NORM = 16.0
GLA_CHUNK = 64
GLA_IN_DIM = 2 * GLA_KEY_DIM + 2 * GLA_VAL_DIM
DIFF_HEAD_DIM = 128
DIFF_HEADS = D_MODEL // (2 * DIFF_HEAD_DIM)
DIFF_VAL_DIM = 2 * DIFF_HEAD_DIM
DIFF_WIDTH = DIFF_HEADS * DIFF_VAL_DIM
DIFF_IN_DIM = 4 * DIFF_WIDTH
ROPE_BASE = 10000.0

V7X_LANES = 128
V7X_SUBLANES = 8
V7X_VMEM_BYTES = 64 * 1024 * 1024
VMEM_LIMIT = 56 * 1024 * 1024

GLA_BLOCK = 256
ADA_ROWS = V7X_SUBLANES
V7X_MXU_DIM = 256
ATTN_KEY_SUB = V7X_MXU_DIM
ATTN_SCORE_BOUND = 60.0


def _dot(a, b):
    return jnp.dot(a, b, preferred_element_type=F32)


def _dot_nt(a, b):
    return lax.dot_general(a, b, (((1,), (1,)), ((), ())), preferred_element_type=F32)


def _params(sem):
    return pltpu.CompilerParams(dimension_semantics=sem, vmem_limit_bytes=VMEM_LIMIT)


def _ada_kernel(c_ref, w_ref, b_ref, o_ref):
    c = c_ref[...]
    s = c * jax.nn.sigmoid(c)
    o_ref[...] = jnp.dot(s, w_ref[...], preferred_element_type=F32,
                         precision=lax.Precision.HIGHEST) + b_ref[...]


def _ada_call(cond, ada_w, ada_b):
    tn = 1024
    n = ada_w.shape[-1]
    return pl.pallas_call(
        _ada_kernel,
        grid=(DEPTH, n // tn),
        in_specs=[
            pl.BlockSpec((ADA_ROWS, D_MODEL), lambda l, j: (0, 0)),
            pl.BlockSpec((None, D_MODEL, tn), lambda l, j: (l, 0, j)),
            pl.BlockSpec((None, 1, tn), lambda l, j: (l, 0, j)),
        ],
        out_specs=pl.BlockSpec((None, ADA_ROWS, tn), lambda l, j: (l, 0, j)),
        out_shape=jax.ShapeDtypeStruct((DEPTH, ADA_ROWS, n), F32),
        compiler_params=_params(("arbitrary", "arbitrary")),
        name="ada_mod",
    )(cond, ada_w, ada_b.reshape(DEPTH, 1, n))


def _normalize_into(x_ref, shift_ref, scale_ref, nw_ref, hn_ref):
    x = x_ref[...]
    ms = jnp.mean(x * x, axis=-1, keepdims=True)
    a = nw_ref[...] * (1.0 + scale_ref[...])
    hn_ref[...] = (x * lax.rsqrt(ms + EPS) * a + shift_ref[...]).astype(BF16)


def _gla_inproj_kernel(x_ref, shift_ref, scale_ref, nw_ref, w_ref, a1_ref, a2_ref, gb_ref,
                       *rest):
    out_ref, la_ref, hn_ref = rest[-3:]
    j = pl.program_id(2)

    @pl.when(j == 0)
    def _():
        _normalize_into(x_ref, shift_ref, scale_ref, nw_ref, hn_ref)
        z = _dot(hn_ref[...], a1_ref[...])
        lg = _dot(z.astype(BF16), a2_ref[...]) + gb_ref[...]
        log_sig = jnp.minimum(lg, 0.0) - jnp.log(1.0 + jnp.exp(-jnp.abs(lg)))
        la_ref[...] = log_sig * (1.0 / GLA_GATE_NORM)

    out_ref[...] = _dot(hn_ref[...], w_ref[...]).astype(BF16)


def _rope_rotate(t, cos, sin_signed, first_of_pair):
    partner = jnp.where(first_of_pair, pltpu.roll(t, 96, 1), pltpu.roll(t, 32, 1))
    return t * cos + partner * sin_signed


def _diff_inproj_kernel(x_ref, shift_ref, scale_ref, nw_ref, w_ref, qn_ref, kn_ref,
                        cos_ref, sin_ref, *rest, rope, tn, q_scale):
    out_ref, hn_ref = rest[-2:]
    j = pl.program_id(2)
    tiles_per_part = DIFF_WIDTH // tn

    @pl.when(j == 0)
    def _():
        _normalize_into(x_ref, shift_ref, scale_ref, nw_ref, hn_ref)

    acc = _dot(hn_ref[...], w_ref[...])

    def head_norm(nrm_ref, scale):
        lane = lax.broadcasted_iota(jnp.int32, (1, DIFF_HEAD_DIM), 1)
        first_of_pair = (lane // (DIFF_HEAD_DIM // 4)) % 2 == 0
        if rope:
            cos = cos_ref[...]
            sin_signed = jnp.where(first_of_pair, -sin_ref[...], sin_ref[...])
        for g in range(tn // DIFF_HEAD_DIM):
            sl = slice(g * DIFF_HEAD_DIM, (g + 1) * DIFF_HEAD_DIM)
            t = acc[:, sl]
            ms = jnp.mean(t * t, axis=-1, keepdims=True)
            t = t * lax.rsqrt(ms + EPS) * nrm_ref[g % 2:g % 2 + 1, :]
            if rope:
                t = _rope_rotate(t, cos, sin_signed, first_of_pair)
            if scale != 1.0:
                t = t * scale
            out_ref[:, sl] = t.astype(BF16)

    @pl.when(j < tiles_per_part)
    def _():
        head_norm(qn_ref, q_scale)

    @pl.when((j >= tiles_per_part) & (j < 2 * tiles_per_part))
    def _():
        head_norm(kn_ref, 1.0)

    @pl.when(j >= 2 * tiles_per_part)
    def _():
        out_ref[...] = acc.astype(BF16)


def _inproj_call(kernel, x, shift, scale, nw, w, extras, extra_specs, *, t_total, row_off,
                 tm, tn, extra_out=None, alias_in=None, name):
    b, t, d = x.shape
    n = w.shape[-1]
    blk_off = row_off // tm
    assert row_off % tm == 0 and t % tm == 0 and n % tn == 0
    in_specs = [
        pl.BlockSpec((None, tm, d), lambda bi, i, j: (bi, i, 0)),
        pl.BlockSpec((None, 1, d), lambda bi, i, j: (bi, 0, 0)),
        pl.BlockSpec((None, 1, d), lambda bi, i, j: (bi, 0, 0)),
        pl.BlockSpec((1, d), lambda bi, i, j: (0, 0)),
        pl.BlockSpec((d, tn), lambda bi, i, j: (0, j)),
    ] + extra_specs
    out_specs = [pl.BlockSpec((None, tm, tn), lambda bi, i, j: (bi, i + blk_off, j))]
    out_shape = [jax.ShapeDtypeStruct((b, t_total, n), BF16)]
    if extra_out is not None:
        width, dtype = extra_out
        out_specs.append(pl.BlockSpec((None, tm, width), lambda bi, i, j: (bi, i + blk_off, 0)))
        out_shape.append(jax.ShapeDtypeStruct((b, t_total, width), dtype))
    args = [x, shift, scale, nw, w] + extras
    aliases = {}
    if alias_in is not None:
        for k, arr in enumerate(alias_in):
            aliases[len(args)] = k
            args.append(arr)
            in_specs.append(pl.BlockSpec(memory_space=pl.ANY))
    return pl.pallas_call(
        kernel,
        grid=(b, t // tm, n // tn),
        in_specs=in_specs,
        out_specs=out_specs,
        out_shape=out_shape,
        scratch_shapes=[pltpu.VMEM((tm, d), BF16)],
        input_output_aliases=aliases,
        compiler_params=_params(("arbitrary", "arbitrary", "arbitrary")),
        name=name,
    )(*args)


def _gla_chunk_matrices():
    idx = jnp.arange(GLA_BLOCK)
    i = idx[:, None]
    j = idx[None, :]
    same = (i // GLA_CHUNK) == (j // GLA_CHUNK)
    start = (i // GLA_CHUNK) * GLA_CHUNK
    mats = []
    for direction in range(2):
        if direction == 0:
            a_q = same & (j <= i)
            ref = same & (j <= start + GLA_CHUNK // 2 - 1)
            b_k = same & (i > j)
        else:
            a_q = same & (j >= i)
            ref = same & (j >= start + GLA_CHUNK // 2)
            b_k = same & (i < j)
        a_t = a_q.astype(F32) - ref.astype(F32)
        mats.append(jnp.stack([a_q.astype(F32), a_t, b_k.astype(F32), same.astype(F32)]))
    return jnp.stack(mats).astype(BF16)


def _split_hi_lo(x):
    hi = x.astype(BF16)
    lo = (x - hi.astype(F32)).astype(BF16)
    return hi, lo


def _gla_direction(q_ref, k_ref, v_ref, la_ref, mats_ref, s_ref, o_ref, direction):
    n_chunks = GLA_BLOCK // GLA_CHUNK
    la = la_ref[...]
    la_hi, la_lo = _split_hi_lo(la)
    lat_hi, lat_lo = _split_hi_lo(la.T)
    a_q = mats_ref[direction, 0]
    a_t = mats_ref[direction, 1]
    b_k = mats_ref[direction, 2]
    same = mats_ref[direction, 3]

    e_q = _dot(a_q, la_hi) + _dot(a_q, la_lo)
    e_t = _dot(a_t, la_hi) + _dot(a_t, la_lo)
    et_k = _dot(lat_hi, b_k) + _dot(lat_lo, b_k)
    tot_t = _dot(lat_hi, same) + _dot(lat_lo, same)

    q = q_ref[...].astype(F32) * (GLA_DK ** -0.5)
    k = k_ref[...].astype(F32)
    v = v_ref[...]
    q_in = (q * jnp.exp(e_q)).astype(BF16)
    q_t = (q * jnp.exp(e_t)).astype(BF16)
    k_t = (k * jnp.exp(-e_t)).astype(BF16)
    kin_t = k.T * jnp.exp(et_k)

    scores = _dot_nt(q_t, k_t)
    scores = jnp.where(a_q > 0, scores, 0.0)
    o_intra = _dot(scores.astype(BF16), v)

    col = lax.broadcasted_iota(jnp.int32, (1, GLA_BLOCK), 1) // GLA_CHUNK
    order = range(n_chunks) if direction == 0 else range(n_chunks - 1, -1, -1)
    for c in order:
        rows = slice(c * GLA_CHUNK, (c + 1) * GLA_CHUNK)
        s = s_ref[...]
        o_c = _dot(q_in[rows], s.astype(BF16)) + o_intra[rows]
        o_ref[rows, :] = o_c.astype(BF16)
        upd = _dot(jnp.where(col == c, kin_t, 0.0).astype(BF16), v)
        decay = jnp.exp(tot_t[:, c * GLA_CHUNK:c * GLA_CHUNK + 1])
        s_ref[...] = decay * s + upd


def _gla_scan_kernel(qf_ref, kf_ref, vf_ref, laf_ref, qb_ref, kb_ref, vb_ref, lab_ref,
                     mats_ref, of_ref, ob_ref, sf_ref, sb_ref):
    @pl.when(pl.program_id(2) == 0)
    def _():
        sf_ref[...] = jnp.zeros_like(sf_ref)
        sb_ref[...] = jnp.zeros_like(sb_ref)

    _gla_direction(qf_ref, kf_ref, vf_ref, laf_ref, mats_ref, sf_ref, of_ref, 0)
    _gla_direction(qb_ref, kb_ref, vb_ref, lab_ref, mats_ref, sb_ref, ob_ref, 1)


def _gla_scan_call(qkvg, la, n_latent_blocks):
    b, t_total, _ = qkvg.shape
    nb = t_total // GLA_BLOCK
    ctx_blk = n_latent_blocks
    k_off = GLA_KEY_DIM // GLA_DK
    v_off = 2 * GLA_KEY_DIM // GLA_DV

    def pos_f(t):
        return jnp.where(t == 0, ctx_blk, t - 1)

    def pos_b(t):
        return jnp.where(t == 0, ctx_blk, n_latent_blocks - t)

    def specs(pos, la_col_off):
        return [
            pl.BlockSpec((None, GLA_BLOCK, GLA_DK), lambda bi, h, t: (bi, pos(t), h)),
            pl.BlockSpec((None, GLA_BLOCK, GLA_DK), lambda bi, h, t: (bi, pos(t), k_off + h)),
            pl.BlockSpec((None, GLA_BLOCK, GLA_DV), lambda bi, h, t: (bi, pos(t), v_off + h)),
            pl.BlockSpec((None, GLA_BLOCK, GLA_DK), lambda bi, h, t: (bi, pos(t), la_col_off + h)),
        ]

    mats = _gla_chunk_matrices()
    in_specs = specs(pos_f, 0) + specs(pos_b, GLA_HEADS) + [
        pl.BlockSpec(mats.shape, lambda bi, h, t: (0, 0, 0, 0))]
    out_spec_f = pl.BlockSpec((None, GLA_BLOCK, GLA_DV), lambda bi, h, t: (bi, pos_f(t), h))
    out_spec_b = pl.BlockSpec((None, GLA_BLOCK, GLA_DV), lambda bi, h, t: (bi, pos_b(t), h))
    o_shape = jax.ShapeDtypeStruct((b, t_total, GLA_VAL_DIM), BF16)
    return pl.pallas_call(
        _gla_scan_kernel,
        grid=(b, GLA_HEADS, nb),
        in_specs=in_specs,
        out_specs=[out_spec_f, out_spec_b],
        out_shape=[o_shape, o_shape],
        scratch_shapes=[pltpu.VMEM((GLA_DK, GLA_DV), F32), pltpu.VMEM((GLA_DK, GLA_DV), F32)],
        compiler_params=_params(("arbitrary", "arbitrary", "arbitrary")),
        name="gla_scan",
    )(qkvg, qkvg, qkvg, la, qkvg, qkvg, qkvg, la, mats)


def _finish_kernel(*refs, head_norm):
    if head_norm:
        x_ref, gate_ref, o1_ref, o2_ref, g_ref, gn_ref, w_ref, out_ref = refs
        o = o1_ref[...].astype(F32) + o2_ref[...].astype(F32)
        parts = []
        for h in range(GLA_HEADS):
            oh = o[:, h * GLA_DV:(h + 1) * GLA_DV]
            ms = jnp.mean(oh * oh, axis=-1, keepdims=True)
            parts.append(oh * lax.rsqrt(ms + EPS) * gn_ref[...])
        o = jnp.concatenate(parts, axis=-1)
    else:
        x_ref, gate_ref, o1_ref, g_ref, w_ref, out_ref = refs
        o = o1_ref[...].astype(F32)
    g = g_ref[...].astype(F32)
    y_in = (o * (g * jax.nn.sigmoid(g))).astype(BF16)
    y = _dot(y_in, w_ref[...])
    out_ref[...] = x_ref[...] + gate_ref[...] * y


def _finish_call(x, gate, o_list, g_arr, g_col_blk, w, *, row_off, gn_w=None, name):
    b, t, d = x.shape
    tm = 256
    blk_off = row_off // tm
    assert row_off % tm == 0 and t % tm == 0
    width = w.shape[0]
    in_specs = [
        pl.BlockSpec((None, tm, d), lambda bi, i: (bi, i, 0)),
        pl.BlockSpec((None, 1, d), lambda bi, i: (bi, 0, 0)),
    ]
    args = [x, gate]
    for o in o_list:
        in_specs.append(pl.BlockSpec((None, tm, width), lambda bi, i: (bi, i + blk_off, 0)))
        args.append(o)
    in_specs.append(pl.BlockSpec((None, tm, width), lambda bi, i: (bi, i + blk_off, g_col_blk)))
    args.append(g_arr)
    if gn_w is not None:
        in_specs.append(pl.BlockSpec((1, gn_w.shape[-1]), lambda bi, i: (0, 0)))
        args.append(gn_w)
    in_specs.append(pl.BlockSpec((width, d), lambda bi, i: (0, 0)))
    args.append(w)
    return pl.pallas_call(
        functools.partial(_finish_kernel, head_norm=gn_w is not None),
        grid=(b, t // tm),
        in_specs=in_specs,
        out_specs=pl.BlockSpec((None, tm, d), lambda bi, i: (bi, i, 0)),
        out_shape=jax.ShapeDtypeStruct((b, t, d), F32),
        compiler_params=_params(("arbitrary", "arbitrary")),
        name=name,
    )(*args)


def _attn_kernel(q1_ref, q2_ref, k1_ref, k2_ref, v_ref, lam_ref, sw_ref, o_ref,
                 m_ref, l_ref, acc_ref, *, tk, lambda_init):
    n_keys = v_ref.shape[0]
    m_ref[...] = jnp.full_like(m_ref, -1e30)
    l_ref[...] = jnp.zeros_like(l_ref)
    acc_ref[...] = jnp.zeros_like(acc_ref)

    def body(c, carry):
        off = pl.multiple_of(c * tk, tk)
        vc = v_ref[pl.ds(off, tk), :]
        for mp, (q_ref, k_ref) in enumerate(((q1_ref, k1_ref), (q2_ref, k2_ref))):
            s = _dot_nt(q_ref[...], k_ref[pl.ds(off, tk), :])
            m_old = m_ref[mp]
            m_new = jnp.maximum(m_old, jnp.max(s, axis=1, keepdims=True))
            alpha = jnp.exp2(m_old - m_new)
            p = jnp.exp2(s - m_new)
            l_ref[mp] = alpha * l_ref[mp] + jnp.sum(p, axis=1, keepdims=True)
            acc_ref[mp] = alpha * acc_ref[mp] + _dot(p.astype(BF16), vc)
            m_ref[mp] = m_new
        return carry

    lax.fori_loop(0, n_keys // tk, body, 0)

    lv = lam_ref[...]
    lam = (jnp.exp(jnp.sum(lv[0:1] * lv[1:2], axis=-1, keepdims=True))
           - jnp.exp(jnp.sum(lv[2:3] * lv[3:4], axis=-1, keepdims=True)) + lambda_init)
    o = acc_ref[0] / l_ref[0] - lam * (acc_ref[1] / l_ref[1])
    ms = jnp.mean(o * o, axis=-1, keepdims=True)
    o = o * lax.rsqrt(ms + EPS) * sw_ref[...] * (1.0 - lambda_init)
    o_ref[...] = o.astype(BF16)


def _attn_bounded_kernel(q1_ref, q2_ref, k1_ref, k2_ref, v_ref, lam_ref, sw_ref, o_ref,
                         l_ref, acc_ref, *, tk, lambda_init):
    n_keys = v_ref.shape[0]
    l_ref[...] = jnp.zeros_like(l_ref)
    acc_ref[...] = jnp.zeros_like(acc_ref)

    def body(c, carry):
        for mp, (q_ref, k_ref) in enumerate(((q1_ref, k1_ref), (q2_ref, k2_ref))):
            q = q_ref[...]
            acc = None
            lsum = None
            for sb in range(tk // ATTN_KEY_SUB):
                off = pl.multiple_of(c * tk + sb * ATTN_KEY_SUB, ATTN_KEY_SUB)
                s = _dot_nt(q, k_ref[pl.ds(off, ATTN_KEY_SUB), :])
                p = jnp.exp2(s)
                part = p[:, :V7X_LANES] + p[:, V7X_LANES:]
                lsum = part if lsum is None else lsum + part
                d = _dot(p.astype(BF16), v_ref[pl.ds(off, ATTN_KEY_SUB), :])
                acc = d if acc is None else acc + d
            l_ref[mp] += lsum
            acc_ref[mp] += acc
        return carry

    lax.fori_loop(0, n_keys // tk, body, 0)

    lv = lam_ref[...]
    lam = (jnp.exp(jnp.sum(lv[0:1] * lv[1:2], axis=-1, keepdims=True))
           - jnp.exp(jnp.sum(lv[2:3] * lv[3:4], axis=-1, keepdims=True)) + lambda_init)
    l1 = jnp.sum(l_ref[0], axis=-1, keepdims=True)
    l2 = jnp.sum(l_ref[1], axis=-1, keepdims=True)
    o = acc_ref[0] / l1 - lam * (acc_ref[1] / l2)
    ms = jnp.mean(o * o, axis=-1, keepdims=True)
    o = o * lax.rsqrt(ms + EPS) * sw_ref[...] * (1.0 - lambda_init)
    o_ref[...] = o.astype(BF16)


def _attn_call(qkvg, n_query, lam_v, subln_w, lambda_init, *, bounded):
    b, n_keys, _ = qkvg.shape
    tq, tk = 512, 1280
    assert n_query % tq == 0 and n_keys % tk == 0 and tk % ATTN_KEY_SUB == 0
    d = DIFF_HEAD_DIM
    k_off = DIFF_WIDTH // d
    v_off = 2 * DIFF_WIDTH // DIFF_VAL_DIM
    in_specs = [
        pl.BlockSpec((None, tq, d), lambda bi, h, i: (bi, i, 2 * h)),
        pl.BlockSpec((None, tq, d), lambda bi, h, i: (bi, i, 2 * h + 1)),
        pl.BlockSpec((None, n_keys, d), lambda bi, h, i: (bi, 0, k_off + 2 * h)),
        pl.BlockSpec((None, n_keys, d), lambda bi, h, i: (bi, 0, k_off + 2 * h + 1)),
        pl.BlockSpec((None, n_keys, DIFF_VAL_DIM), lambda bi, h, i: (bi, 0, v_off + h)),
        pl.BlockSpec(lam_v.shape, lambda bi, h, i: (0, 0)),
        pl.BlockSpec((1, DIFF_VAL_DIM), lambda bi, h, i: (0, 0)),
    ]
    if bounded:
        body = functools.partial(_attn_bounded_kernel, tk=tk, lambda_init=lambda_init)
        scratch = [pltpu.VMEM((2, tq, V7X_LANES), F32), pltpu.VMEM((2, tq, DIFF_VAL_DIM), F32)]
    else:
        body = functools.partial(_attn_kernel, tk=tk, lambda_init=lambda_init)
        scratch = [pltpu.VMEM((2, tq, 1), F32), pltpu.VMEM((2, tq, 1), F32),
                   pltpu.VMEM((2, tq, DIFF_VAL_DIM), F32)]
    return pl.pallas_call(
        body,
        grid=(b, DIFF_HEADS, n_query // tq),
        in_specs=in_specs,
        out_specs=pl.BlockSpec((None, tq, DIFF_VAL_DIM), lambda bi, h, i: (bi, i, h)),
        out_shape=jax.ShapeDtypeStruct((b, n_query, DIFF_WIDTH), BF16),
        scratch_shapes=scratch,
        compiler_params=_params(("arbitrary", "arbitrary", "arbitrary")),
        name="diff_attn_bounded" if bounded else "diff_attn",
    )(qkvg, qkvg, qkvg, qkvg, qkvg, lam_v, subln_w.reshape(1, DIFF_VAL_DIM))


def _axial_rope_tables(n_tokens):
    rows_n = n_tokens // GRID_W
    row = jnp.repeat(jnp.arange(rows_n), GRID_W).astype(F32)
    col = jnp.tile(jnp.arange(GRID_W), rows_n).astype(F32)
    half = DIFF_HEAD_DIM // 2
    inv_freq = ROPE_BASE ** (-jnp.arange(0, half, 2, dtype=F32) / half)
    ang_r = row[:, None] * inv_freq
    ang_c = col[:, None] * inv_freq
    ang = jnp.concatenate([ang_r, ang_r, ang_c, ang_c], axis=-1)
    return jnp.cos(ang), jnp.sin(ang)


def kernel(x, c, ctx, c_ctx, norm_w, ada_w, ada_b, gla_w_in, gla_gate_a1, gla_gate_a2, gla_gate_b,
           gla_gn_w, gla_w_out, diff_w_in, diff_qn_w, diff_kn_w, diff_lam, diff_subln_w, diff_w_out):
    b, seq, d = x.shape
    ctx_len = ctx.shape[1]
    t_total = seq + ctx_len
    assert ctx_len == GLA_BLOCK and seq % GLA_BLOCK == 0 and d == D_MODEL

    cond = jnp.zeros((ADA_ROWS, d), F32).at[:b].set(c).at[b].set(c_ctx)
    mod = _ada_call(cond, ada_w, ada_b)

    def mod_rows(layer, part):
        m = mod[layer, :, part * d:(part + 1) * d]
        lat = m[:b, None, :]
        cx = jnp.broadcast_to(m[b][None, None, :], (b, 1, d))
        return lat, cx

    shift, shift_c = mod_rows(0, 0)
    scale, scale_c = mod_rows(0, 1)
    gate, gate_c = mod_rows(0, 2)
    nw0 = norm_w[0].reshape(1, d)
    w_in0 = gla_w_in[0].astype(BF16)
    a1 = jnp.zeros((d, V7X_LANES), F32)
    a1 = a1.at[:, :GLA_GATE_RANK].set(gla_gate_a1[0, 0]).at[:, GLA_GATE_RANK:2 * GLA_GATE_RANK].set(
        gla_gate_a1[0, 1]).astype(BF16)
    a2 = jnp.zeros((V7X_LANES, 2 * GLA_KEY_DIM), F32)
    a2 = a2.at[:GLA_GATE_RANK, :GLA_KEY_DIM].set(gla_gate_a2[0, 0])
    a2 = a2.at[GLA_GATE_RANK:2 * GLA_GATE_RANK, GLA_KEY_DIM:].set(gla_gate_a2[0, 1]).astype(BF16)
    gb = gla_gate_b[0].reshape(1, 2 * GLA_KEY_DIM)
    gla_extras = [a1, a2, gb]
    gla_extra_specs = [
        pl.BlockSpec(a1.shape, lambda bi, i, j: (0, 0)),
        pl.BlockSpec(a2.shape, lambda bi, i, j: (0, 0)),
        pl.BlockSpec(gb.shape, lambda bi, i, j: (0, 0)),
    ]
    la_out = (2 * GLA_KEY_DIM, F32)
    qkvg0, la0 = _inproj_call(
        _gla_inproj_kernel, x, shift, scale, nw0, w_in0, gla_extras, gla_extra_specs,
        t_total=t_total, row_off=0, tm=512, tn=1024, extra_out=la_out, name="gla_inproj_x")
    qkvg0, la0 = _inproj_call(
        _gla_inproj_kernel, ctx, shift_c, scale_c, nw0, w_in0, gla_extras, gla_extra_specs,
        t_total=t_total, row_off=seq, tm=GLA_BLOCK, tn=1024, extra_out=la_out,
        alias_in=[qkvg0, la0], name="gla_inproj_ctx")

    o_f, o_b = _gla_scan_call(qkvg0, la0, seq // GLA_BLOCK)

    w_out0 = gla_w_out[0].astype(BF16)
    gn_w = gla_gn_w[0].reshape(1, GLA_DV)
    g_col_blk0 = (2 * GLA_KEY_DIM + GLA_VAL_DIM) // GLA_VAL_DIM
    x1 = _finish_call(x, gate, [o_f, o_b], qkvg0, g_col_blk0, w_out0, row_off=0, gn_w=gn_w,
                      name="gla_finish_x")
    ctx1 = _finish_call(ctx, gate_c, [o_f, o_b], qkvg0, g_col_blk0, w_out0, row_off=seq, gn_w=gn_w,
                        name="gla_finish_ctx")

    lambda_init = 0.8 - 0.6 * math.exp(-0.3 * 1)
    shift, shift_c = mod_rows(1, 0)
    scale, scale_c = mod_rows(1, 1)
    gate, _ = mod_rows(1, 2)
    nw1 = norm_w[1].reshape(1, d)
    w_in1 = diff_w_in[0].astype(BF16)
    cos, sin = _axial_rope_tables(seq)
    q_scale = math.log2(math.e) * DIFF_HEAD_DIM ** -0.5
    tm1, tn1 = 1024, 1024
    diff_extras = [diff_qn_w[0], diff_kn_w[0]]
    nrm_specs = [
        pl.BlockSpec((2, DIFF_HEAD_DIM), lambda bi, i, j: (0, 0)),
        pl.BlockSpec((2, DIFF_HEAD_DIM), lambda bi, i, j: (0, 0)),
    ]
    (qkvg1,) = _inproj_call(
        functools.partial(_diff_inproj_kernel, rope=True, tn=tn1, q_scale=q_scale),
        x1, shift, scale, nw1, w_in1, diff_extras + [cos, sin],
        nrm_specs + [pl.BlockSpec((tm1, DIFF_HEAD_DIM), lambda bi, i, j: (i, 0)),
                     pl.BlockSpec((tm1, DIFF_HEAD_DIM), lambda bi, i, j: (i, 0))],
        t_total=t_total, row_off=0, tm=tm1, tn=tn1, name="diff_inproj_x")
    (qkvg1,) = _inproj_call(
        functools.partial(_diff_inproj_kernel, rope=False, tn=tn1, q_scale=q_scale),
        ctx1, shift_c, scale_c, nw1, w_in1, diff_extras + [cos, sin],
        nrm_specs + [pl.BlockSpec((GLA_BLOCK, DIFF_HEAD_DIM), lambda bi, i, j: (0, 0)),
                     pl.BlockSpec((GLA_BLOCK, DIFF_HEAD_DIM), lambda bi, i, j: (0, 0))],
        t_total=t_total, row_off=seq, tm=GLA_BLOCK, tn=tn1, alias_in=[qkvg1],
        name="diff_inproj_ctx")

    score_bound = q_scale * DIFF_HEAD_DIM * jnp.max(
        jnp.max(jnp.abs(diff_qn_w[0]), axis=-1) * jnp.max(jnp.abs(diff_kn_w[0]), axis=-1))
    attn = functools.partial(_attn_call, qkvg1, seq, diff_lam[0], diff_subln_w[0], lambda_init)
    o_attn = lax.cond(score_bound <= ATTN_SCORE_BOUND,
                      functools.partial(attn, bounded=True), functools.partial(attn, bounded=False))

    w_out1 = diff_w_out[0].astype(BF16)
    g_col_blk1 = 3 * DIFF_WIDTH // DIFF_WIDTH
    return _finish_call(x1, gate, [o_attn], qkvg1, g_col_blk1, w_out1, row_off=0,
                        name="diff_finish_x")
```

```python
import functools
import math

import jax
import jax.numpy as jnp
from jax import lax
from jax.experimental import pallas as pl
from jax.experimental.pallas import tpu as pltpu

F32 = jnp.float32
BF16 = jnp.bfloat16

D_MODEL = 2048
DEPTH = 2
GRID_W = 64
EPS = 1e-6
GLA_HEADS = 4
GLA_KEY_DIM = D_MODEL // 2
GLA_VAL_DIM = D_MODEL
GLA_DK = GLA_KEY_DIM // GLA_HEADS
GLA_DV = GLA_VAL_DIM // GLA_HEADS
GLA_GATE_RANK = 16
GLA_GATE_NORM = 16.0
GLA_CHUNK = 64
GLA_IN_DIM = 2 * GLA_KEY_DIM + 2 * GLA_VAL_DIM
DIFF_HEAD_DIM = 128
DIFF_HEADS = D_MODEL // (2 * DIFF_HEAD_DIM)
DIFF_VAL_DIM = 2 * DIFF_HEAD_DIM
DIFF_WIDTH = DIFF_HEADS * DIFF_VAL_DIM
DIFF_IN_DIM = 4 * DIFF_WIDTH
ROPE_BASE = 10000.0

V7X_LANES = 128
V7X_SUBLANES = 8
V7X_MXU_DIM = 256
VMEM_LIMIT = 56 * 1024 * 1024

ADA_ROWS = V7X_SUBLANES
ADA_TN = 1024
GLA_BLOCK = 256
GLA_INPROJ_TILE = (512, 1024)
DIFF_INPROJ_TILE = (1024, 1024)
CTX_INPROJ_TM = 256
FINISH_TM = 256
ATTN_TQ = 512
ATTN_TK = 2048
ATTN_TK_ONLINE = 1024
ATTN_KEY_SUB = V7X_MXU_DIM
ATTN_SCORE_BOUND = 60.0


def _dot(a, b):
    return jnp.dot(a, b, preferred_element_type=F32)


def _dot_nt(a, b):
    return lax.dot_general(a, b, (((1,), (1,)), ((), ())), preferred_element_type=F32)


def _params(sem):
    return pltpu.CompilerParams(dimension_semantics=sem, vmem_limit_bytes=VMEM_LIMIT)


def _ada_kernel(c_ref, w_ref, b_ref, o_ref):
    c = c_ref[...]
    s = c * jax.nn.sigmoid(c)
    o_ref[...] = jnp.dot(s, w_ref[...], preferred_element_type=F32,
                         precision=lax.Precision.HIGHEST) + b_ref[...]


def _ada_call(cond, ada_w, ada_b):
    tn = ADA_TN
    n = ada_w.shape[-1]
    return pl.pallas_call(
        _ada_kernel,
        grid=(DEPTH, n // tn),
        in_specs=[
            pl.BlockSpec((ADA_ROWS, D_MODEL), lambda l, j: (0, 0)),
            pl.BlockSpec((None, D_MODEL, tn), lambda l, j: (l, 0, j)),
            pl.BlockSpec((None, 1, tn), lambda l, j: (l, 0, j)),
        ],
        out_specs=pl.BlockSpec((None, ADA_ROWS, tn), lambda l, j: (l, 0, j)),
        out_shape=jax.ShapeDtypeStruct((DEPTH, ADA_ROWS, n), F32),
        compiler_params=_params(("arbitrary", "arbitrary")),
        name="ada_mod",
    )(cond, ada_w, ada_b.reshape(DEPTH, 1, n))


def _normalize_into(x_ref, shift_ref, scale_ref, nw_ref, hn_ref):
    x = x_ref[...]
    ms = jnp.mean(x * x, axis=-1, keepdims=True)
    a = nw_ref[...] * (1.0 + scale_ref[...])
    hn_ref[...] = (x * lax.rsqrt(ms + EPS) * a + shift_ref[...]).astype(BF16)


def _gla_inproj_kernel(x_ref, shift_ref, scale_ref, nw_ref, w_ref, a1_ref, a2_ref, gb_ref,
                       out_ref, la_ref, hn_ref):
    j = pl.program_id(2)

    @pl.when(j == 0)
    def _():
        _normalize_into(x_ref, shift_ref, scale_ref, nw_ref, hn_ref)
        z = _dot(hn_ref[...], a1_ref[...])
        lg = _dot(z.astype(BF16), a2_ref[...]) + gb_ref[...]
        log_sig = jnp.minimum(lg, 0.0) - jnp.log(1.0 + jnp.exp(-jnp.abs(lg)))
        la_ref[...] = log_sig * (1.0 / GLA_GATE_NORM)

    out_ref[...] = _dot(hn_ref[...], w_ref[...]).astype(BF16)


def _rope_rotate(t, cos, sin_signed, first_of_pair):
    partner = jnp.where(first_of_pair, pltpu.roll(t, 96, 1), pltpu.roll(t, 32, 1))
    return t * cos + partner * sin_signed


def _diff_inproj_kernel(x_ref, shift_ref, scale_ref, nw_ref, w_ref, qn_ref, kn_ref,
                        cos_ref, sin_ref, out_ref, hn_ref, *, rope, tn, q_scale):
    j = pl.program_id(2)
    tiles_per_part = DIFF_WIDTH // tn

    @pl.when(j == 0)
    def _():
        _normalize_into(x_ref, shift_ref, scale_ref, nw_ref, hn_ref)

    acc = _dot(hn_ref[...], w_ref[...])

    def head_norm(nrm_ref, scale):
        lane = lax.broadcasted_iota(jnp.int32, (1, DIFF_HEAD_DIM), 1)
        first_of_pair = (lane // (DIFF_HEAD_DIM // 4)) % 2 == 0
        if rope:
            cos = cos_ref[...]
            sin_signed = jnp.where(first_of_pair, -sin_ref[...], sin_ref[...])
        for g in range(tn // DIFF_HEAD_DIM):
            sl = slice(g * DIFF_HEAD_DIM, (g + 1) * DIFF_HEAD_DIM)
            t = acc[:, sl]
            ms = jnp.mean(t * t, axis=-1, keepdims=True)
            t = t * lax.rsqrt(ms + EPS) * nrm_ref[g % 2:g % 2 + 1, :]
            if rope:
                t = _rope_rotate(t, cos, sin_signed, first_of_pair)
            if scale != 1.0:
                t = t * scale
            out_ref[:, sl] = t.astype(BF16)

    @pl.when(j < tiles_per_part)
    def _():
        head_norm(qn_ref, q_scale)

    @pl.when((j >= tiles_per_part) & (j < 2 * tiles_per_part))
    def _():
        head_norm(kn_ref, 1.0)

    @pl.when(j >= 2 * tiles_per_part)
    def _():
        out_ref[...] = acc.astype(BF16)


def _inproj_call(kernel, x, shift, scale, nw, w, extras, extra_specs, *, tm, tn,
                 extra_out=None, name):
    b, t, d = x.shape
    n = w.shape[-1]
    assert t % tm == 0 and n % tn == 0
    in_specs = [
        pl.BlockSpec((None, tm, d), lambda bi, i, j: (bi, i, 0)),
        pl.BlockSpec((None, 1, d), lambda bi, i, j: (bi, 0, 0)),
        pl.BlockSpec((None, 1, d), lambda bi, i, j: (bi, 0, 0)),
        pl.BlockSpec((1, d), lambda bi, i, j: (0, 0)),
        pl.BlockSpec((d, tn), lambda bi, i, j: (0, j)),
    ] + extra_specs
    out_specs = [pl.BlockSpec((None, tm, tn), lambda bi, i, j: (bi, i, j))]
    out_shape = [jax.ShapeDtypeStruct((b, t, n), BF16)]
    if extra_out is not None:
        width, dtype = extra_out
        out_specs.append(pl.BlockSpec((None, tm, width), lambda bi, i, j: (bi, i, 0)))
        out_shape.append(jax.ShapeDtypeStruct((b, t, width), dtype))
    return pl.pallas_call(
        kernel,
        grid=(b, t // tm, n // tn),
        in_specs=in_specs,
        out_specs=out_specs,
        out_shape=out_shape,
        scratch_shapes=[pltpu.VMEM((tm, d), BF16)],
        compiler_params=_params(("arbitrary", "arbitrary", "arbitrary")),
        name=name,
    )(x, shift, scale, nw, w, *extras)


def _gla_chunk_matrices():
    idx = jnp.arange(GLA_BLOCK)
    i = idx[:, None]
    j = idx[None, :]
    same = (i // GLA_CHUNK) == (j // GLA_CHUNK)
    start = (i // GLA_CHUNK) * GLA_CHUNK
    mats = []
    for direction in range(2):
        if direction == 0:
            a_q = same & (j <= i)
            ref = same & (j <= start + GLA_CHUNK // 2 - 1)
            b_k = same & (i > j)
        else:
            a_q = same & (j >= i)
            ref = same & (j >= start + GLA_CHUNK // 2)
            b_k = same & (i < j)
        a_t = a_q.astype(F32) - ref.astype(F32)
        mats.append(jnp.stack([a_q.astype(F32), a_t, b_k.astype(F32), same.astype(F32)]))
    return jnp.stack(mats).astype(BF16)


def _split_hi_lo(x):
    hi = x.astype(BF16)
    lo = (x - hi.astype(F32)).astype(BF16)
    return hi, lo


def _gla_direction(q_ref, k_ref, v_ref, la_ref, mats_ref, s_ref, o_ref, direction):
    n_chunks = GLA_BLOCK // GLA_CHUNK
    la = la_ref[...]
    la_hi, la_lo = _split_hi_lo(la)
    lat_hi, lat_lo = _split_hi_lo(la.T)
    a_q = mats_ref[direction, 0]
    a_t = mats_ref[direction, 1]
    b_k = mats_ref[direction, 2]
    same = mats_ref[direction, 3]

    e_q = _dot(a_q, la_hi) + _dot(a_q, la_lo)
    e_t = _dot(a_t, la_hi) + _dot(a_t, la_lo)
    et_k = _dot(lat_hi, b_k) + _dot(lat_lo, b_k)
    tot_t = _dot(lat_hi, same) + _dot(lat_lo, same)

    q = q_ref[...].astype(F32) * (GLA_DK ** -0.5)
    k = k_ref[...].astype(F32)
    v = v_ref[...]
    q_in = (q * jnp.exp(e_q)).astype(BF16)
    q_t = (q * jnp.exp(e_t)).astype(BF16)
    k_t = (k * jnp.exp(-e_t)).astype(BF16)
    kin_t = k.T * jnp.exp(et_k)

    scores = _dot_nt(q_t, k_t)
    scores = jnp.where(a_q > 0, scores, 0.0)
    o_intra = _dot(scores.astype(BF16), v)

    col = lax.broadcasted_iota(jnp.int32, (1, GLA_BLOCK), 1) // GLA_CHUNK
    order = range(n_chunks) if direction == 0 else range(n_chunks - 1, -1, -1)
    for c in order:
        rows = slice(c * GLA_CHUNK, (c + 1) * GLA_CHUNK)
        s = s_ref[...]
        o_c = _dot(q_in[rows], s.astype(BF16)) + o_intra[rows]
        o_ref[rows, :] = o_c.astype(BF16)
        upd = _dot(jnp.where(col == c, kin_t, 0.0).astype(BF16), v)
        decay = jnp.exp(tot_t[:, c * GLA_CHUNK:c * GLA_CHUNK + 1])
        s_ref[...] = decay * s + upd


def _gla_scan_kernel(qf_ref, kf_ref, vf_ref, laf_ref, qb_ref, kb_ref, vb_ref, lab_ref,
                     qc_ref, kc_ref, vc_ref, lacf_ref, lacb_ref, mats_ref,
                     of_ref, ob_ref, ocf_ref, ocb_ref, sf_ref, sb_ref):
    t = pl.program_id(2)

    @pl.when(t == 0)
    def _():
        sf_ref[...] = jnp.zeros_like(sf_ref)
        sb_ref[...] = jnp.zeros_like(sb_ref)
        _gla_direction(qc_ref, kc_ref, vc_ref, lacf_ref, mats_ref, sf_ref, ocf_ref, 0)
        _gla_direction(qc_ref, kc_ref, vc_ref, lacb_ref, mats_ref, sb_ref, ocb_ref, 1)

    @pl.when(t > 0)
    def _():
        _gla_direction(qf_ref, kf_ref, vf_ref, laf_ref, mats_ref, sf_ref, of_ref, 0)
        _gla_direction(qb_ref, kb_ref, vb_ref, lab_ref, mats_ref, sb_ref, ob_ref, 1)


def _gla_scan_call(qkvg_x, la_x, qkvg_c, la_c):
    b, seq, _ = qkvg_x.shape
    nb = seq // GLA_BLOCK
    assert qkvg_c.shape[1] == GLA_BLOCK
    k_off = GLA_KEY_DIM // GLA_DK
    v_off = 2 * GLA_KEY_DIM // GLA_DV

    def pos_f(t):
        return jnp.maximum(t - 1, 0)

    def pos_b(t):
        return nb - jnp.maximum(t, 1)

    def pos_c(t):
        return 0

    def specs(pos, la_col_off, with_la=True):
        out = [
            pl.BlockSpec((None, GLA_BLOCK, GLA_DK), lambda bi, h, t: (bi, pos(t), h)),
            pl.BlockSpec((None, GLA_BLOCK, GLA_DK), lambda bi, h, t: (bi, pos(t), k_off + h)),
            pl.BlockSpec((None, GLA_BLOCK, GLA_DV), lambda bi, h, t: (bi, pos(t), v_off + h)),
        ]
        if with_la:
            out.append(pl.BlockSpec((None, GLA_BLOCK, GLA_DK),
                                    lambda bi, h, t: (bi, pos(t), la_col_off + h)))
        return out

    def la_spec(pos, la_col_off):
        return pl.BlockSpec((None, GLA_BLOCK, GLA_DK), lambda bi, h, t: (bi, pos(t), la_col_off + h))

    def o_spec(pos):
        return pl.BlockSpec((None, GLA_BLOCK, GLA_DV), lambda bi, h, t: (bi, pos(t), h))

    mats = _gla_chunk_matrices()
    in_specs = (specs(pos_f, 0) + specs(pos_b, GLA_HEADS) + specs(pos_c, 0, with_la=False)
                + [la_spec(pos_c, 0), la_spec(pos_c, GLA_HEADS),
                   pl.BlockSpec(mats.shape, lambda bi, h, t: (0, 0, 0, 0))])
    ox_shape = jax.ShapeDtypeStruct((b, seq, GLA_VAL_DIM), BF16)
    oc_shape = jax.ShapeDtypeStruct((b, GLA_BLOCK, GLA_VAL_DIM), BF16)
    return pl.pallas_call(
        _gla_scan_kernel,
        grid=(b, GLA_HEADS, nb + 1),
        in_specs=in_specs,
        out_specs=[o_spec(pos_f), o_spec(pos_b), o_spec(pos_c), o_spec(pos_c)],
        out_shape=[ox_shape, ox_shape, oc_shape, oc_shape],
        scratch_shapes=[pltpu.VMEM((GLA_DK, GLA_DV), F32), pltpu.VMEM((GLA_DK, GLA_DV), F32)],
        compiler_params=_params(("arbitrary", "arbitrary", "arbitrary")),
        name="gla_scan",
    )(qkvg_x, qkvg_x, qkvg_x, la_x, qkvg_x, qkvg_x, qkvg_x, la_x,
      qkvg_c, qkvg_c, qkvg_c, la_c, la_c, mats)


def _finish_kernel(*refs, head_norm):
    if head_norm:
        x_ref, gate_ref, o1_ref, o2_ref, g_ref, gn_ref, w_ref, out_ref = refs
        o = o1_ref[...].astype(F32) + o2_ref[...].astype(F32)
        parts = []
        for h in range(GLA_HEADS):
            oh = o[:, h * GLA_DV:(h + 1) * GLA_DV]
            ms = jnp.mean(oh * oh, axis=-1, keepdims=True)
            parts.append(oh * lax.rsqrt(ms + EPS) * gn_ref[...])
        o = jnp.concatenate(parts, axis=-1)
    else:
        x_ref, gate_ref, o1_ref, g_ref, w_ref, out_ref = refs
        o = o1_ref[...].astype(F32)
    g = g_ref[...].astype(F32)
    y_in = (o * (g * jax.nn.sigmoid(g))).astype(BF16)
    y = _dot(y_in, w_ref[...])
    out_ref[...] = x_ref[...] + gate_ref[...] * y


def _finish_call(x, gate, o_list, g_arr, g_col_blk, w, *, gn_w=None, name):
    b, t, d = x.shape
    tm = FINISH_TM
    assert t % tm == 0
    width = w.shape[0]
    in_specs = [
        pl.BlockSpec((None, tm, d), lambda bi, i: (bi, i, 0)),
        pl.BlockSpec((None, 1, d), lambda bi, i: (bi, 0, 0)),
    ]
    args = [x, gate]
    for o in o_list:
        in_specs.append(pl.BlockSpec((None, tm, width), lambda bi, i: (bi, i, 0)))
        args.append(o)
    in_specs.append(pl.BlockSpec((None, tm, width), lambda bi, i: (bi, i, g_col_blk)))
    args.append(g_arr)
    if gn_w is not None:
        in_specs.append(pl.BlockSpec((1, gn_w.shape[-1]), lambda bi, i: (0, 0)))
        args.append(gn_w)
    in_specs.append(pl.BlockSpec((width, d), lambda bi, i: (0, 0)))
    args.append(w)
    return pl.pallas_call(
        functools.partial(_finish_kernel, head_norm=gn_w is not None),
        grid=(b, t // tm),
        in_specs=in_specs,
        out_specs=pl.BlockSpec((None, tm, d), lambda bi, i: (bi, i, 0)),
        out_shape=jax.ShapeDtypeStruct((b, t, d), F32),
        compiler_params=_params(("arbitrary", "arbitrary")),
        name=name,
    )(*args)


def _attn_finalize(acc1, l1, acc2, l2, lam_ref, sw_ref, o_ref, lambda_init):
    lv = lam_ref[...]
    lam = (jnp.exp(jnp.sum(lv[0:1] * lv[1:2], axis=-1, keepdims=True))
           - jnp.exp(jnp.sum(lv[2:3] * lv[3:4], axis=-1, keepdims=True)) + lambda_init)
    o = acc1 / l1 - lam * (acc2 / l2)
    ms = jnp.mean(o * o, axis=-1, keepdims=True)
    o = o * lax.rsqrt(ms + EPS) * sw_ref[...] * (1.0 - lambda_init)
    o_ref[...] = o.astype(BF16)


def _attn_online_kernel(q1_ref, q2_ref, k1_ref, k2_ref, v_ref, k1c_ref, k2c_ref, vc_ref,
                        lam_ref, sw_ref, o_ref, m_ref, l_ref, acc_ref, *, tk, lambda_init):
    n_keys = v_ref.shape[0]
    m_ref[...] = jnp.full_like(m_ref, -1e30)
    l_ref[...] = jnp.zeros_like(l_ref)
    acc_ref[...] = jnp.zeros_like(acc_ref)

    def update(mp, q_ref, k_blk, v_blk):
        s = _dot_nt(q_ref[...], k_blk)
        m_old = m_ref[mp]
        m_new = jnp.maximum(m_old, jnp.max(s, axis=1, keepdims=True))
        alpha = jnp.exp2(m_old - m_new)
        p = jnp.exp2(s - m_new)
        l_ref[mp] = alpha * l_ref[mp] + jnp.sum(p, axis=1, keepdims=True)
        acc_ref[mp] = alpha * acc_ref[mp] + _dot(p.astype(BF16), v_blk)
        m_ref[mp] = m_new

    def body(c, carry):
        off = pl.multiple_of(c * tk, tk)
        v_blk = v_ref[pl.ds(off, tk), :]
        update(0, q1_ref, k1_ref[pl.ds(off, tk), :], v_blk)
        update(1, q2_ref, k2_ref[pl.ds(off, tk), :], v_blk)
        return carry

    lax.fori_loop(0, n_keys // tk, body, 0)
    update(0, q1_ref, k1c_ref[...], vc_ref[...])
    update(1, q2_ref, k2c_ref[...], vc_ref[...])
    _attn_finalize(acc_ref[0], l_ref[0], acc_ref[1], l_ref[1], lam_ref, sw_ref, o_ref, lambda_init)


def _attn_bounded_kernel(q1_ref, q2_ref, k1_ref, k2_ref, v_ref, k1c_ref, k2c_ref, vc_ref,
                         lam_ref, sw_ref, o_ref, l_ref, acc_ref, *, tk, lambda_init):
    n_keys = v_ref.shape[0]
    l_ref[...] = jnp.zeros_like(l_ref)
    acc_ref[...] = jnp.zeros_like(acc_ref)

    def sub_block(q, k_blk, v_blk):
        s = _dot_nt(q, k_blk)
        p = jnp.exp2(s)
        part = p[:, :V7X_LANES] + p[:, V7X_LANES:]
        return part, _dot(p.astype(BF16), v_blk)

    def body(c, carry):
        for mp, (q_ref, k_ref) in enumerate(((q1_ref, k1_ref), (q2_ref, k2_ref))):
            q = q_ref[...]
            acc = None
            lsum = None
            for sb in range(tk // ATTN_KEY_SUB):
                off = pl.multiple_of(c * tk + sb * ATTN_KEY_SUB, ATTN_KEY_SUB)
                part, d = sub_block(q, k_ref[pl.ds(off, ATTN_KEY_SUB), :],
                                    v_ref[pl.ds(off, ATTN_KEY_SUB), :])
                lsum = part if lsum is None else lsum + part
                acc = d if acc is None else acc + d
            l_ref[mp] += lsum
            acc_ref[mp] += acc
        return carry

    lax.fori_loop(0, n_keys // tk, body, 0)

    for mp, (q_ref, kc_ref) in enumerate(((q1_ref, k1c_ref), (q2_ref, k2c_ref))):
        part, d = sub_block(q_ref[...], kc_ref[...], vc_ref[...])
        l_ref[mp] += part
        acc_ref[mp] += d

    l1 = jnp.sum(l_ref[0], axis=-1, keepdims=True)
    l2 = jnp.sum(l_ref[1], axis=-1, keepdims=True)
    _attn_finalize(acc_ref[0], l1, acc_ref[1], l2, lam_ref, sw_ref, o_ref, lambda_init)


def _attn_call(qkvg_x, qkvg_c, lam_v, subln_w, lambda_init, *, bounded):
    b, seq, _ = qkvg_x.shape
    n_ctx = qkvg_c.shape[1]
    tq = ATTN_TQ
    tk = ATTN_TK if bounded else ATTN_TK_ONLINE
    assert seq % tq == 0 and seq % tk == 0 and tk % ATTN_KEY_SUB == 0 and n_ctx == ATTN_KEY_SUB
    d = DIFF_HEAD_DIM
    k_off = DIFF_WIDTH // d
    v_off = 2 * DIFF_WIDTH // DIFF_VAL_DIM
    in_specs = [
        pl.BlockSpec((None, tq, d), lambda bi, h, i: (bi, i, 2 * h)),
        pl.BlockSpec((None, tq, d), lambda bi, h, i: (bi, i, 2 * h + 1)),
        pl.BlockSpec((None, seq, d), lambda bi, h, i: (bi, 0, k_off + 2 * h)),
        pl.BlockSpec((None, seq, d), lambda bi, h, i: (bi, 0, k_off + 2 * h + 1)),
        pl.BlockSpec((None, seq, DIFF_VAL_DIM), lambda bi, h, i: (bi, 0, v_off + h)),
        pl.BlockSpec((None, n_ctx, d), lambda bi, h, i: (bi, 0, k_off + 2 * h)),
        pl.BlockSpec((None, n_ctx, d), lambda bi, h, i: (bi, 0, k_off + 2 * h + 1)),
        pl.BlockSpec((None, n_ctx, DIFF_VAL_DIM), lambda bi, h, i: (bi, 0, v_off + h)),
        pl.BlockSpec(lam_v.shape, lambda bi, h, i: (0, 0)),
        pl.BlockSpec((1, DIFF_VAL_DIM), lambda bi, h, i: (0, 0)),
    ]
    if bounded:
        body = functools.partial(_attn_bounded_kernel, tk=tk, lambda_init=lambda_init)
        scratch = [pltpu.VMEM((2, tq, V7X_LANES), F32), pltpu.VMEM((2, tq, DIFF_VAL_DIM), F32)]
    else:
        body = functools.partial(_attn_online_kernel, tk=tk, lambda_init=lambda_init)
        scratch = [pltpu.VMEM((2, tq, 1), F32), pltpu.VMEM((2, tq, 1), F32),
                   pltpu.VMEM((2, tq, DIFF_VAL_DIM), F32)]
    return pl.pallas_call(
        body,
        grid=(b, DIFF_HEADS, seq // tq),
        in_specs=in_specs,
        out_specs=pl.BlockSpec((None, tq, DIFF_VAL_DIM), lambda bi, h, i: (bi, i, h)),
        out_shape=jax.ShapeDtypeStruct((b, seq, DIFF_WIDTH), BF16),
        scratch_shapes=scratch,
        compiler_params=_params(("arbitrary", "arbitrary", "arbitrary")),
        name="diff_attn_bounded" if bounded else "diff_attn_online",
    )(qkvg_x, qkvg_x, qkvg_x, qkvg_x, qkvg_x, qkvg_c, qkvg_c, qkvg_c,
      lam_v, subln_w.reshape(1, DIFF_VAL_DIM))


def _axial_rope_tables(n_tokens):
    rows_n = n_tokens // GRID_W
    row = jnp.repeat(jnp.arange(rows_n), GRID_W).astype(F32)
    col = jnp.tile(jnp.arange(GRID_W), rows_n).astype(F32)
    half = DIFF_HEAD_DIM // 2
    inv_freq = ROPE_BASE ** (-jnp.arange(0, half, 2, dtype=F32) / half)
    ang_r = row[:, None] * inv_freq
    ang_c = col[:, None] * inv_freq
    ang = jnp.concatenate([ang_r, ang_r, ang_c, ang_c], axis=-1)
    return jnp.cos(ang), jnp.sin(ang)


def kernel(x, c, ctx, c_ctx, norm_w, ada_w, ada_b, gla_w_in, gla_gate_a1, gla_gate_a2, gla_gate_b,
           gla_gn_w, gla_w_out, diff_w_in, diff_qn_w, diff_kn_w, diff_lam, diff_subln_w, diff_w_out):
    b, seq, d = x.shape
    ctx_len = ctx.shape[1]
    assert ctx_len == GLA_BLOCK and seq % GLA_BLOCK == 0 and d == D_MODEL

    cond = jnp.zeros((ADA_ROWS, d), F32).at[:b].set(c).at[b].set(c_ctx)
    mod = _ada_call(cond, ada_w, ada_b)

    def mod_rows(layer, part):
        m = mod[layer, :, part * d:(part + 1) * d]
        lat = m[:b, None, :]
        cx = jnp.broadcast_to(m[b][None, None, :], (b, 1, d))
        return lat, cx

    shift, shift_c = mod_rows(0, 0)
    scale, scale_c = mod_rows(0, 1)
    gate, gate_c = mod_rows(0, 2)
    nw0 = norm_w[0].reshape(1, d)
    w_in0 = gla_w_in[0].astype(BF16)
    a1 = jnp.zeros((d, V7X_LANES), F32)
    a1 = a1.at[:, :GLA_GATE_RANK].set(gla_gate_a1[0, 0]).at[:, GLA_GATE_RANK:2 * GLA_GATE_RANK].set(
        gla_gate_a1[0, 1]).astype(BF16)
    a2 = jnp.zeros((V7X_LANES, 2 * GLA_KEY_DIM), F32)
    a2 = a2.at[:GLA_GATE_RANK, :GLA_KEY_DIM].set(gla_gate_a2[0, 0])
    a2 = a2.at[GLA_GATE_RANK:2 * GLA_GATE_RANK, GLA_KEY_DIM:].set(gla_gate_a2[0, 1]).astype(BF16)
    gb = gla_gate_b[0].reshape(1, 2 * GLA_KEY_DIM)
    gla_extras = [a1, a2, gb]
    gla_extra_specs = [
        pl.BlockSpec(a1.shape, lambda bi, i, j: (0, 0)),
        pl.BlockSpec(a2.shape, lambda bi, i, j: (0, 0)),
        pl.BlockSpec(gb.shape, lambda bi, i, j: (0, 0)),
    ]
    la_out = (2 * GLA_KEY_DIM, F32)
    tm0, tn0 = GLA_INPROJ_TILE
    qkvg0_x, la0_x = _inproj_call(
        _gla_inproj_kernel, x, shift, scale, nw0, w_in0, gla_extras, gla_extra_specs,
        tm=tm0, tn=tn0, extra_out=la_out, name="gla_inproj_x")
    qkvg0_c, la0_c = _inproj_call(
        _gla_inproj_kernel, ctx, shift_c, scale_c, nw0, w_in0, gla_extras, gla_extra_specs,
        tm=CTX_INPROJ_TM, tn=tn0, extra_out=la_out, name="gla_inproj_ctx")

    o_f, o_b, oc_f, oc_b = _gla_scan_call(qkvg0_x, la0_x, qkvg0_c, la0_c)

    w_out0 = gla_w_out[0].astype(BF16)
    gn_w = gla_gn_w[0].reshape(1, GLA_DV)
    g_col_blk0 = (2 * GLA_KEY_DIM + GLA_VAL_DIM) // GLA_VAL_DIM
    x1 = _finish_call(x, gate, [o_f, o_b], qkvg0_x, g_col_blk0, w_out0, gn_w=gn_w,
                      name="gla_finish_x")
    ctx1 = _finish_call(ctx, gate_c, [oc_f, oc_b], qkvg0_c, g_col_blk0, w_out0, gn_w=gn_w,
                        name="gla_finish_ctx")

    lambda_init = 0.8 - 0.6 * math.exp(-0.3 * 1)
    shift, shift_c = mod_rows(1, 0)
    scale, scale_c = mod_rows(1, 1)
    gate, _ = mod_rows(1, 2)
    nw1 = norm_w[1].reshape(1, d)
    w_in1 = diff_w_in[0].astype(BF16)
    cos, sin = _axial_rope_tables(seq)
    q_scale = math.log2(math.e) * DIFF_HEAD_DIM ** -0.5
    tm1, tn1 = DIFF_INPROJ_TILE
    diff_extras = [diff_qn_w[0], diff_kn_w[0]]
    nrm_specs = [
        pl.BlockSpec((2, DIFF_HEAD_DIM), lambda bi, i, j: (0, 0)),
        pl.BlockSpec((2, DIFF_HEAD_DIM), lambda bi, i, j: (0, 0)),
    ]
    (qkvg1_x,) = _inproj_call(
        functools.partial(_diff_inproj_kernel, rope=True, tn=tn1, q_scale=q_scale),
        x1, shift, scale, nw1, w_in1, diff_extras + [cos, sin],
        nrm_specs + [pl.BlockSpec((tm1, DIFF_HEAD_DIM), lambda bi, i, j: (i, 0)),
                     pl.BlockSpec((tm1, DIFF_HEAD_DIM), lambda bi, i, j: (i, 0))],
        tm=tm1, tn=tn1, name="diff_inproj_x")
    (qkvg1_c,) = _inproj_call(
        functools.partial(_diff_inproj_kernel, rope=False, tn=tn1, q_scale=q_scale),
        ctx1, shift_c, scale_c, nw1, w_in1, diff_extras + [cos, sin],
        nrm_specs + [pl.BlockSpec((CTX_INPROJ_TM, DIFF_HEAD_DIM), lambda bi, i, j: (0, 0)),
                     pl.BlockSpec((CTX_INPROJ_TM, DIFF_HEAD_DIM), lambda bi, i, j: (0, 0))],
        tm=CTX_INPROJ_TM, tn=tn1, name="diff_inproj_ctx")

    score_bound = q_scale * DIFF_HEAD_DIM * jnp.max(
        jnp.max(jnp.abs(diff_qn_w[0]), axis=-1) * jnp.max(jnp.abs(diff_kn_w[0]), axis=-1))
    attn = functools.partial(_attn_call, qkvg1_x, qkvg1_c, diff_lam[0], diff_subln_w[0], lambda_init)
    o_attn = lax.cond(score_bound <= ATTN_SCORE_BOUND,
                      functools.partial(attn, bounded=True), functools.partial(attn, bounded=False))

    w_out1 = diff_w_out[0].astype(BF16)
    g_col_blk1 = 3 * DIFF_WIDTH // DIFF_WIDTH
    return _finish_call(x1, gate, [o_attn], qkvg1_x, g_col_blk1, w_out1, name="diff_finish_x")
```

```python
import functools
import math

import jax
import jax.numpy as jnp
from jax import lax
from jax.experimental import pallas as pl
from jax.experimental.pallas import tpu as pltpu

F32 = jnp.float32
BF16 = jnp.bfloat16

D_MODEL = 2048
DEPTH = 2
GRID_W = 64
EPS = 1e-6
GLA_HEADS = 4
GLA_KEY_DIM = D_MODEL // 2
GLA_VAL_DIM = D_MODEL
GLA_DK = GLA_KEY_DIM // GLA_HEADS
GLA_DV = GLA_VAL_DIM // GLA_HEADS
GLA_GATE_RANK = 16
GLA_GATE_NORM = 16.0
GLA_CHUNK = 64
GLA_IN_DIM = 2 * GLA_KEY_DIM + 2 * GLA_VAL_DIM
DIFF_HEAD_DIM = 128
DIFF_HEADS = D_MODEL // (2 * DIFF_HEAD_DIM)
DIFF_VAL_DIM = 2 * DIFF_HEAD_DIM
DIFF_WIDTH = DIFF_HEADS * DIFF_VAL_DIM
DIFF_IN_DIM = 4 * DIFF_WIDTH
ROPE_BASE = 10000.0

V7X_LANES = 128
V7X_SUBLANES = 8
V7X_MXU_DIM = 256
VMEM_LIMIT = 56 * 1024 * 1024

ADA_ROWS = V7X_SUBLANES
ADA_TN = 1024
GLA_BLOCK = 256
GLA_INPROJ_TILE = (512, 1024)
DIFF_INPROJ_TILE = (1024, 1024)
CTX_INPROJ_TM = 256
FINISH_TM = 256
ATTN_TQ = 1024
ATTN_TK = 4096
ATTN_TK_ONLINE = 1024
ATTN_KEY_SUB = V7X_MXU_DIM
ATTN_SCORE_BOUND = 60.0


def _dot(a, b):
    return jnp.dot(a, b, preferred_element_type=F32)


def _dot_nt(a, b):
    return lax.dot_general(a, b, (((1,), (1,)), ((), ())), preferred_element_type=F32)


def _params(sem):
    return pltpu.CompilerParams(dimension_semantics=sem, vmem_limit_bytes=VMEM_LIMIT)


def _ada_kernel(c_ref, w_ref, b_ref, o_ref):
    c = c_ref[...]
    s = c * jax.nn.sigmoid(c)
    o_ref[...] = jnp.dot(s, w_ref[...], preferred_element_type=F32,
                         precision=lax.Precision.HIGHEST) + b_ref[...]


def _ada_call(cond, ada_w, ada_b):
    tn = ADA_TN
    n = ada_w.shape[-1]
    return pl.pallas_call(
        _ada_kernel,
        grid=(DEPTH, n // tn),
        in_specs=[
            pl.BlockSpec((ADA_ROWS, D_MODEL), lambda l, j: (0, 0)),
            pl.BlockSpec((None, D_MODEL, tn), lambda l, j: (l, 0, j)),
            pl.BlockSpec((None, 1, tn), lambda l, j: (l, 0, j)),
        ],
        out_specs=pl.BlockSpec((None, ADA_ROWS, tn), lambda l, j: (l, 0, j)),
        out_shape=jax.ShapeDtypeStruct((DEPTH, ADA_ROWS, n), F32),
        compiler_params=_params(("arbitrary", "arbitrary")),
        name="ada_mod",
    )(cond, ada_w, ada_b.reshape(DEPTH, 1, n))


def _normalize_into(x_ref, shift_ref, scale_ref, nw_ref, hn_ref):
    x = x_ref[...]
    ms = jnp.mean(x * x, axis=-1, keepdims=True)
    a = nw_ref[...] * (1.0 + scale_ref[...])
    hn_ref[...] = (x * lax.rsqrt(ms + EPS) * a + shift_ref[...]).astype(BF16)


def _gla_inproj_kernel(x_ref, shift_ref, scale_ref, nw_ref, w_ref, a1_ref, a2_ref, gb_ref,
                       out_ref, la_ref, hn_ref):
    j = pl.program_id(2)

    @pl.when(j == 0)
    def _():
        _normalize_into(x_ref, shift_ref, scale_ref, nw_ref, hn_ref)
        z = _dot(hn_ref[...], a1_ref[...])
        lg = _dot(z.astype(BF16), a2_ref[...]) + gb_ref[...]
        log_sig = jnp.minimum(lg, 0.0) - jnp.log(1.0 + jnp.exp(-jnp.abs(lg)))
        la_ref[...] = log_sig * (1.0 / GLA_GATE_NORM)

    out_ref[...] = _dot(hn_ref[...], w_ref[...]).astype(BF16)


def _rope_rotate(t, cos, sin_signed, first_of_pair):
    partner = jnp.where(first_of_pair, pltpu.roll(t, 96, 1), pltpu.roll(t, 32, 1))
    return t * cos + partner * sin_signed


def _diff_inproj_kernel(x_ref, shift_ref, scale_ref, nw_ref, w_ref, qn_ref, kn_ref,
                        cos_ref, sin_ref, out_ref, hn_ref, *, rope, tn, q_scale):
    j = pl.program_id(2)
    tiles_per_part = DIFF_WIDTH // tn

    @pl.when(j == 0)
    def _():
        _normalize_into(x_ref, shift_ref, scale_ref, nw_ref, hn_ref)

    acc = _dot(hn_ref[...], w_ref[...])

    def head_norm(nrm_ref, scale):
        lane = lax.broadcasted_iota(jnp.int32, (1, DIFF_HEAD_DIM), 1)
        first_of_pair = (lane // (DIFF_HEAD_DIM // 4)) % 2 == 0
        if rope:
            cos = cos_ref[...]
            sin_signed = jnp.where(first_of_pair, -sin_ref[...], sin_ref[...])
        for g in range(tn // DIFF_HEAD_DIM):
            sl = slice(g * DIFF_HEAD_DIM, (g + 1) * DIFF_HEAD_DIM)
            t = acc[:, sl]
            ms = jnp.mean(t * t, axis=-1, keepdims=True)
            t = t * lax.rsqrt(ms + EPS) * nrm_ref[g % 2:g % 2 + 1, :]
            if rope:
                t = _rope_rotate(t, cos, sin_signed, first_of_pair)
            if scale != 1.0:
                t = t * scale
            out_ref[:, sl] = t.astype(BF16)

    @pl.when(j < tiles_per_part)
    def _():
        head_norm(qn_ref, q_scale)

    @pl.when((j >= tiles_per_part) & (j < 2 * tiles_per_part))
    def _():
        head_norm(kn_ref, 1.0)

    @pl.when(j >= 2 * tiles_per_part)
    def _():
        out_ref[...] = acc.astype(BF16)


def _inproj_call(kernel, x, shift, scale, nw, w, extras, extra_specs, *, tm, tn,
                 extra_out=None, name):
    b, t, d = x.shape
    n = w.shape[-1]
    assert t % tm == 0 and n % tn == 0
    in_specs = [
        pl.BlockSpec((None, tm, d), lambda bi, i, j: (bi, i, 0)),
        pl.BlockSpec((None, 1, d), lambda bi, i, j: (bi, 0, 0)),
        pl.BlockSpec((None, 1, d), lambda bi, i, j: (bi, 0, 0)),
        pl.BlockSpec((1, d), lambda bi, i, j: (0, 0)),
        pl.BlockSpec((d, tn), lambda bi, i, j: (0, j)),
    ] + extra_specs
    out_specs = [pl.BlockSpec((None, tm, tn), lambda bi, i, j: (bi, i, j))]
    out_shape = [jax.ShapeDtypeStruct((b, t, n), BF16)]
    if extra_out is not None:
        width, dtype = extra_out
        out_specs.append(pl.BlockSpec((None, tm, width), lambda bi, i, j: (bi, i, 0)))
        out_shape.append(jax.ShapeDtypeStruct((b, t, width), dtype))
    return pl.pallas_call(
        kernel,
        grid=(b, t // tm, n // tn),
        in_specs=in_specs,
        out_specs=out_specs,
        out_shape=out_shape,
        scratch_shapes=[pltpu.VMEM((tm, d), BF16)],
        compiler_params=_params(("arbitrary", "arbitrary", "arbitrary")),
        name=name,
    )(x, shift, scale, nw, w, *extras)


def _gla_chunk_matrices():
    idx = jnp.arange(GLA_BLOCK)
    i = idx[:, None]
    j = idx[None, :]
    same = (i // GLA_CHUNK) == (j // GLA_CHUNK)
    start = (i // GLA_CHUNK) * GLA_CHUNK
    mats = []
    for direction in range(2):
        if direction == 0:
            a_q = same & (j <= i)
            ref = same & (j <= start + GLA_CHUNK // 2 - 1)
            b_k = same & (i > j)
        else:
            a_q = same & (j >= i)
            ref = same & (j >= start + GLA_CHUNK // 2)
            b_k = same & (i < j)
        a_t = a_q.astype(F32) - ref.astype(F32)
        mats.append(jnp.stack([a_q.astype(F32), a_t, b_k.astype(F32), same.astype(F32)]))
    return jnp.stack(mats).astype(BF16)


def _split_hi_lo(x):
    hi = x.astype(BF16)
    lo = (x - hi.astype(F32)).astype(BF16)
    return hi, lo


def _gla_direction(q_ref, k_ref, v_ref, la_ref, mats_ref, s_ref, o_ref, direction):
    n_chunks = GLA_BLOCK // GLA_CHUNK
    la = la_ref[...]
    la_hi, la_lo = _split_hi_lo(la)
    lat_hi, lat_lo = _split_hi_lo(la.T)
    a_q = mats_ref[direction, 0]
    a_t = mats_ref[direction, 1]
    b_k = mats_ref[direction, 2]
    same = mats_ref[direction, 3]

    e_q = _dot(a_q, la_hi) + _dot(a_q, la_lo)
    e_t = _dot(a_t, la_hi) + _dot(a_t, la_lo)
    et_k = _dot(lat_hi, b_k) + _dot(lat_lo, b_k)
    tot_t = _dot(lat_hi, same) + _dot(lat_lo, same)

    q = q_ref[...].astype(F32) * (GLA_DK ** -0.5)
    k = k_ref[...].astype(F32)
    v = v_ref[...]
    q_in = (q * jnp.exp(e_q)).astype(BF16)
    q_t = (q * jnp.exp(e_t)).astype(BF16)
    k_t = (k * jnp.exp(-e_t)).astype(BF16)
    kin_t = k.T * jnp.exp(et_k)

    scores = _dot_nt(q_t, k_t)
    scores = jnp.where(a_q > 0, scores, 0.0)
    o_intra = _dot(scores.astype(BF16), v)

    col = lax.broadcasted_iota(jnp.int32, (1, GLA_BLOCK), 1) // GLA_CHUNK
    order = range(n_chunks) if direction == 0 else range(n_chunks - 1, -1, -1)
    for c in order:
        rows = slice(c * GLA_CHUNK, (c + 1) * GLA_CHUNK)
        s = s_ref[...]
        o_c = _dot(q_in[rows], s.astype(BF16)) + o_intra[rows]
        o_ref[rows, :] = o_c.astype(BF16)
        upd = _dot(jnp.where(col == c, kin_t, 0.0).astype(BF16), v)
        decay = jnp.exp(tot_t[:, c * GLA_CHUNK:c * GLA_CHUNK + 1])
        s_ref[...] = decay * s + upd


def _gla_scan_kernel(qf_ref, kf_ref, vf_ref, laf_ref, qb_ref, kb_ref, vb_ref, lab_ref,
                     qc_ref, kc_ref, vc_ref, lacf_ref, lacb_ref, mats_ref,
                     of_ref, ob_ref, ocf_ref, ocb_ref, sf_ref, sb_ref):
    t = pl.program_id(2)

    @pl.when(t == 0)
    def _():
        sf_ref[...] = jnp.zeros_like(sf_ref)
        sb_ref[...] = jnp.zeros_like(sb_ref)
        _gla_direction(qc_ref, kc_ref, vc_ref, lacf_ref, mats_ref, sf_ref, ocf_ref, 0)
        _gla_direction(qc_ref, kc_ref, vc_ref, lacb_ref, mats_ref, sb_ref, ocb_ref, 1)

    @pl.when(t > 0)
    def _():
        _gla_direction(qf_ref, kf_ref, vf_ref, laf_ref, mats_ref, sf_ref, of_ref, 0)
        _gla_direction(qb_ref, kb_ref, vb_ref, lab_ref, mats_ref, sb_ref, ob_ref, 1)


def _gla_scan_call(qkvg_x, la_x, qkvg_c, la_c):
    b, seq, _ = qkvg_x.shape
    nb = seq // GLA_BLOCK
    assert qkvg_c.shape[1] == GLA_BLOCK
    k_off = GLA_KEY_DIM // GLA_DK
    v_off = 2 * GLA_KEY_DIM // GLA_DV

    def pos_f(t):
        return jnp.maximum(t - 1, 0)

    def pos_b(t):
        return nb - jnp.maximum(t, 1)

    def pos_c(t):
        return 0

    def specs(pos, la_col_off, with_la=True):
        out = [
            pl.BlockSpec((None, GLA_BLOCK, GLA_DK), lambda bi, h, t: (bi, pos(t), h)),
            pl.BlockSpec((None, GLA_BLOCK, GLA_DK), lambda bi, h, t: (bi, pos(t), k_off + h)),
            pl.BlockSpec((None, GLA_BLOCK, GLA_DV), lambda bi, h, t: (bi, pos(t), v_off + h)),
        ]
        if with_la:
            out.append(pl.BlockSpec((None, GLA_BLOCK, GLA_DK),
                                    lambda bi, h, t: (bi, pos(t), la_col_off + h)))
        return out

    def la_spec(pos, la_col_off):
        return pl.BlockSpec((None, GLA_BLOCK, GLA_DK), lambda bi, h, t: (bi, pos(t), la_col_off + h))

    def o_spec(pos):
        return pl.BlockSpec((None, GLA_BLOCK, GLA_DV), lambda bi, h, t: (bi, pos(t), h))

    mats = _gla_chunk_matrices()
    in_specs = (specs(pos_f, 0) + specs(pos_b, GLA_HEADS) + specs(pos_c, 0, with_la=False)
                + [la_spec(pos_c, 0), la_spec(pos_c, GLA_HEADS),
                   pl.BlockSpec(mats.shape, lambda bi, h, t: (0, 0, 0, 0))])
    ox_shape = jax.ShapeDtypeStruct((b, seq, GLA_VAL_DIM), BF16)
    oc_shape = jax.ShapeDtypeStruct((b, GLA_BLOCK, GLA_VAL_DIM), BF16)
    return pl.pallas_call(
        _gla_scan_kernel,
        grid=(b, GLA_HEADS, nb + 1),
        in_specs=in_specs,
        out_specs=[o_spec(pos_f), o_spec(pos_b), o_spec(pos_c), o_spec(pos_c)],
        out_shape=[ox_shape, ox_shape, oc_shape, oc_shape],
        scratch_shapes=[pltpu.VMEM((GLA_DK, GLA_DV), F32), pltpu.VMEM((GLA_DK, GLA_DV), F32)],
        compiler_params=_params(("arbitrary", "arbitrary", "arbitrary")),
        name="gla_scan",
    )(qkvg_x, qkvg_x, qkvg_x, la_x, qkvg_x, qkvg_x, qkvg_x, la_x,
      qkvg_c, qkvg_c, qkvg_c, la_c, la_c, mats)


def _finish_kernel(*refs, head_norm):
    if head_norm:
        x_ref, gate_ref, o1_ref, o2_ref, g_ref, gn_ref, w_ref, out_ref = refs
        o = o1_ref[...].astype(F32) + o2_ref[...].astype(F32)
        parts = []
        for h in range(GLA_HEADS):
            oh = o[:, h * GLA_DV:(h + 1) * GLA_DV]
            ms = jnp.mean(oh * oh, axis=-1, keepdims=True)
            parts.append(oh * lax.rsqrt(ms + EPS) * gn_ref[...])
        o = jnp.concatenate(parts, axis=-1)
    else:
        x_ref, gate_ref, o1_ref, g_ref, w_ref, out_ref = refs
        o = o1_ref[...].astype(F32)
    g = g_ref[...].astype(F32)
    y_in = (o * (g * jax.nn.sigmoid(g))).astype(BF16)
    y = _dot(y_in, w_ref[...])
    out_ref[...] = x_ref[...] + gate_ref[...] * y


def _finish_call(x, gate, o_list, g_arr, g_col_blk, w, *, gn_w=None, name):
    b, t, d = x.shape
    tm = FINISH_TM
    assert t % tm == 0
    width = w.shape[0]
    in_specs = [
        pl.BlockSpec((None, tm, d), lambda bi, i: (bi, i, 0)),
        pl.BlockSpec((None, 1, d), lambda bi, i: (bi, 0, 0)),
    ]
    args = [x, gate]
    for o in o_list:
        in_specs.append(pl.BlockSpec((None, tm, width), lambda bi, i: (bi, i, 0)))
        args.append(o)
    in_specs.append(pl.BlockSpec((None, tm, width), lambda bi, i: (bi, i, g_col_blk)))
    args.append(g_arr)
    if gn_w is not None:
        in_specs.append(pl.BlockSpec((1, gn_w.shape[-1]), lambda bi, i: (0, 0)))
        args.append(gn_w)
    in_specs.append(pl.BlockSpec((width, d), lambda bi, i: (0, 0)))
    args.append(w)
    return pl.pallas_call(
        functools.partial(_finish_kernel, head_norm=gn_w is not None),
        grid=(b, t // tm),
        in_specs=in_specs,
        out_specs=pl.BlockSpec((None, tm, d), lambda bi, i: (bi, i, 0)),
        out_shape=jax.ShapeDtypeStruct((b, t, d), F32),
        compiler_params=_params(("arbitrary", "arbitrary")),
        name=name,
    )(*args)


def _attn_finalize(acc1, l1, acc2, l2, lam_ref, sw_ref, o_ref, lambda_init):
    lv = lam_ref[...]
    lam = (jnp.exp(jnp.sum(lv[0:1] * lv[1:2], axis=-1, keepdims=True))
           - jnp.exp(jnp.sum(lv[2:3] * lv[3:4], axis=-1, keepdims=True)) + lambda_init)
    o = acc1 / l1 - lam * (acc2 / l2)
    ms = jnp.mean(o * o, axis=-1, keepdims=True)
    o = o * lax.rsqrt(ms + EPS) * sw_ref[...] * (1.0 - lambda_init)
    o_ref[...] = o.astype(BF16)


def _attn_online_kernel(q1_ref, q2_ref, k1_ref, k2_ref, v_ref, k1c_ref, k2c_ref, vc_ref,
                        lam_ref, sw_ref, o_ref, m_ref, l_ref, acc_ref, *, tk, lambda_init):
    n_keys = v_ref.shape[0]
    m_ref[...] = jnp.full_like(m_ref, -1e30)
    l_ref[...] = jnp.zeros_like(l_ref)
    acc_ref[...] = jnp.zeros_like(acc_ref)

    def update(mp, q_ref, k_blk, v_blk):
        s = _dot_nt(q_ref[...], k_blk)
        m_old = m_ref[mp]
        m_new = jnp.maximum(m_old, jnp.max(s, axis=1, keepdims=True))
        alpha = jnp.exp2(m_old - m_new)
        p = jnp.exp2(s - m_new)
        l_ref[mp] = alpha * l_ref[mp] + jnp.sum(p, axis=1, keepdims=True)
        acc_ref[mp] = alpha * acc_ref[mp] + _dot(p.astype(BF16), v_blk)
        m_ref[mp] = m_new

    def body(c, carry):
        off = pl.multiple_of(c * tk, tk)
        v_blk = v_ref[pl.ds(off, tk), :]
        update(0, q1_ref, k1_ref[pl.ds(off, tk), :], v_blk)
        update(1, q2_ref, k2_ref[pl.ds(off, tk), :], v_blk)
        return carry

    lax.fori_loop(0, n_keys // tk, body, 0)
    update(0, q1_ref, k1c_ref[...], vc_ref[...])
    update(1, q2_ref, k2c_ref[...], vc_ref[...])
    _attn_finalize(acc_ref[0], l_ref[0], acc_ref[1], l_ref[1], lam_ref, sw_ref, o_ref, lambda_init)


def _attn_bounded_kernel(q1_ref, q2_ref, k1_ref, k2_ref, v_ref, k1c_ref, k2c_ref, vc_ref,
                         lam_ref, sw_ref, o_ref, l_ref, acc_ref, *, tk, lambda_init):
    n_keys = v_ref.shape[0]
    l_ref[...] = jnp.zeros_like(l_ref)
    acc_ref[...] = jnp.zeros_like(acc_ref)

    def sub_block(q, k_blk, v_blk):
        s = _dot_nt(q, k_blk)
        p = jnp.exp2(s)
        part = p[:, :V7X_LANES] + p[:, V7X_LANES:]
        return part, _dot(p.astype(BF16), v_blk)

    def body(c, carry):
        for mp, (q_ref, k_ref) in enumerate(((q1_ref, k1_ref), (q2_ref, k2_ref))):
            q = q_ref[...]
            acc = None
            lsum = None
            for sb in range(tk // ATTN_KEY_SUB):
                off = pl.multiple_of(c * tk + sb * ATTN_KEY_SUB, ATTN_KEY_SUB)
                part, d = sub_block(q, k_ref[pl.ds(off, ATTN_KEY_SUB), :],
                                    v_ref[pl.ds(off, ATTN_KEY_SUB), :])
                lsum = part if lsum is None else lsum + part
                acc = d if acc is None else acc + d
            l_ref[mp] += lsum
            acc_ref[mp] += acc
        return carry

    lax.fori_loop(0, n_keys // tk, body, 0)

    for mp, (q_ref, kc_ref) in enumerate(((q1_ref, k1c_ref), (q2_ref, k2c_ref))):
        part, d = sub_block(q_ref[...], kc_ref[...], vc_ref[...])
        l_ref[mp] += part
        acc_ref[mp] += d

    l1 = jnp.sum(l_ref[0], axis=-1, keepdims=True)
    l2 = jnp.sum(l_ref[1], axis=-1, keepdims=True)
    _attn_finalize(acc_ref[0], l1, acc_ref[1], l2, lam_ref, sw_ref, o_ref, lambda_init)


def _attn_call(qkvg_x, qkvg_c, lam_v, subln_w, lambda_init, *, bounded):
    b, seq, _ = qkvg_x.shape
    n_ctx = qkvg_c.shape[1]
    tq = ATTN_TQ
    tk = ATTN_TK if bounded else ATTN_TK_ONLINE
    assert seq % tq == 0 and seq % tk == 0 and tk % ATTN_KEY_SUB == 0 and n_ctx == ATTN_KEY_SUB
    d = DIFF_HEAD_DIM
    k_off = DIFF_WIDTH // d
    v_off = 2 * DIFF_WIDTH // DIFF_VAL_DIM
    in_specs = [
        pl.BlockSpec((None, tq, d), lambda bi, h, i: (bi, i, 2 * h)),
        pl.BlockSpec((None, tq, d), lambda bi, h, i: (bi, i, 2 * h + 1)),
        pl.BlockSpec((None, seq, d), lambda bi, h, i: (bi, 0, k_off + 2 * h)),
        pl.BlockSpec((None, seq, d), lambda bi, h, i: (bi, 0, k_off + 2 * h + 1)),
        pl.BlockSpec((None, seq, DIFF_VAL_DIM), lambda bi, h, i: (bi, 0, v_off + h)),
        pl.BlockSpec((None, n_ctx, d), lambda bi, h, i: (bi, 0, k_off + 2 * h)),
        pl.BlockSpec((None, n_ctx, d), lambda bi, h, i: (bi, 0, k_off + 2 * h + 1)),
        pl.BlockSpec((None, n_ctx, DIFF_VAL_DIM), lambda bi, h, i: (bi, 0, v_off + h)),
        pl.BlockSpec(lam_v.shape, lambda bi, h, i: (0, 0)),
        pl.BlockSpec((1, DIFF_VAL_DIM), lambda bi, h, i: (0, 0)),
    ]
    if bounded:
        body = functools.partial(_attn_bounded_kernel, tk=tk, lambda_init=lambda_init)
        scratch = [pltpu.VMEM((2, tq, V7X_LANES), F32), pltpu.VMEM((2, tq, DIFF_VAL_DIM), F32)]
    else:
        body = functools.partial(_attn_online_kernel, tk=tk, lambda_init=lambda_init)
        scratch = [pltpu.VMEM((2, tq, 1), F32), pltpu.VMEM((2, tq, 1), F32),
                   pltpu.VMEM((2, tq, DIFF_VAL_DIM), F32)]
    return pl.pallas_call(
        body,
        grid=(b, DIFF_HEADS, seq // tq),
        in_specs=in_specs,
        out_specs=pl.BlockSpec((None, tq, DIFF_VAL_DIM), lambda bi, h, i: (bi, i, h)),
        out_shape=jax.ShapeDtypeStruct((b, seq, DIFF_WIDTH), BF16),
        scratch_shapes=scratch,
        compiler_params=_params(("arbitrary", "arbitrary", "arbitrary")),
        name="diff_attn_bounded" if bounded else "diff_attn_online",
    )(qkvg_x, qkvg_x, qkvg_x, qkvg_x, qkvg_x, qkvg_c, qkvg_c, qkvg_c,
      lam_v, subln_w.reshape(1, DIFF_VAL_DIM))


def _axial_rope_tables(n_tokens):
    rows_n = n_tokens // GRID_W
    row = jnp.repeat(jnp.arange(rows_n), GRID_W).astype(F32)
    col = jnp.tile(jnp.arange(GRID_W), rows_n).astype(F32)
    half = DIFF_HEAD_DIM // 2
    inv_freq = ROPE_BASE ** (-jnp.arange(0, half, 2, dtype=F32) / half)
    ang_r = row[:, None] * inv_freq
    ang_c = col[:, None] * inv_freq
    ang = jnp.concatenate([ang_r, ang_r, ang_c, ang_c], axis=-1)
    return jnp.cos(ang), jnp.sin(ang)


def kernel(x, c, ctx, c_ctx, norm_w, ada_w, ada_b, gla_w_in, gla_gate_a1, gla_gate_a2, gla_gate_b,
           gla_gn_w, gla_w_out, diff_w_in, diff_qn_w, diff_kn_w, diff_lam, diff_subln_w, diff_w_out):
    b, seq, d = x.shape
    ctx_len = ctx.shape[1]
    assert ctx_len == GLA_BLOCK and seq % GLA_BLOCK == 0 and d == D_MODEL

    cond = jnp.zeros((ADA_ROWS, d), F32).at[:b].set(c).at[b].set(c_ctx)
    mod = _ada_call(cond, ada_w, ada_b)

    def mod_rows(layer, part):
        m = mod[layer, :, part * d:(part + 1) * d]
        lat = m[:b, None, :]
        cx = jnp.broadcast_to(m[b][None, None, :], (b, 1, d))
        return lat, cx

    shift, shift_c = mod_rows(0, 0)
    scale, scale_c = mod_rows(0, 1)
    gate, gate_c = mod_rows(0, 2)
    nw0 = norm_w[0].reshape(1, d)
    w_in0 = gla_w_in[0].astype(BF16)
    a1 = jnp.zeros((d, V7X_LANES), F32)
    a1 = a1.at[:, :GLA_GATE_RANK].set(gla_gate_a1[0, 0]).at[:, GLA_GATE_RANK:2 * GLA_GATE_RANK].set(
        gla_gate_a1[0, 1]).astype(BF16)
    a2 = jnp.zeros((V7X_LANES, 2 * GLA_KEY_DIM), F32)
    a2 = a2.at[:GLA_GATE_RANK, :GLA_KEY_DIM].set(gla_gate_a2[0, 0])
    a2 = a2.at[GLA_GATE_RANK:2 * GLA_GATE_RANK, GLA_KEY_DIM:].set(gla_gate_a2[0, 1]).astype(BF16)
    gb = gla_gate_b[0].reshape(1, 2 * GLA_KEY_DIM)
    gla_extras = [a1, a2, gb]
    gla_extra_specs = [
        pl.BlockSpec(a1.shape, lambda bi, i, j: (0, 0)),
        pl.BlockSpec(a2.shape, lambda bi, i, j: (0, 0)),
        pl.BlockSpec(gb.shape, lambda bi, i, j: (0, 0)),
    ]
    la_out = (2 * GLA_KEY_DIM, F32)
    tm0, tn0 = GLA_INPROJ_TILE
    qkvg0_x, la0_x = _inproj_call(
        _gla_inproj_kernel, x, shift, scale, nw0, w_in0, gla_extras, gla_extra_specs,
        tm=tm0, tn=tn0, extra_out=la_out, name="gla_inproj_x")
    qkvg0_c, la0_c = _inproj_call(
        _gla_inproj_kernel, ctx, shift_c, scale_c, nw0, w_in0, gla_extras, gla_extra_specs,
        tm=CTX_INPROJ_TM, tn=tn0, extra_out=la_out, name="gla_inproj_ctx")

    o_f, o_b, oc_f, oc_b = _gla_scan_call(qkvg0_x, la0_x, qkvg0_c, la0_c)

    w_out0 = gla_w_out[0].astype(BF16)
    gn_w = gla_gn_w[0].reshape(1, GLA_DV)
    g_col_blk0 = (2 * GLA_KEY_DIM + GLA_VAL_DIM) // GLA_VAL_DIM
    x1 = _finish_call(x, gate, [o_f, o_b], qkvg0_x, g_col_blk0, w_out0, gn_w=gn_w,
                      name="gla_finish_x")
    ctx1 = _finish_call(ctx, gate_c, [oc_f, oc_b], qkvg0_c, g_col_blk0, w_out0, gn_w=gn_w,
                        name="gla_finish_ctx")

    lambda_init = 0.8 - 0.6 * math.exp(-0.3 * 1)
    shift, shift_c = mod_rows(1, 0)
    scale, scale_c = mod_rows(1, 1)
    gate, _ = mod_rows(1, 2)
    nw1 = norm_w[1].reshape(1, d)
    w_in1 = diff_w_in[0].astype(BF16)
    cos, sin = _axial_rope_tables(seq)
    q_scale = math.log2(math.e) * DIFF_HEAD_DIM ** -0.5
    tm1, tn1 = DIFF_INPROJ_TILE
    diff_extras = [diff_qn_w[0], diff_kn_w[0]]
    nrm_specs = [
        pl.BlockSpec((2, DIFF_HEAD_DIM), lambda bi, i, j: (0, 0)),
        pl.BlockSpec((2, DIFF_HEAD_DIM), lambda bi, i, j: (0, 0)),
    ]
    (qkvg1_x,) = _inproj_call(
        functools.partial(_diff_inproj_kernel, rope=True, tn=tn1, q_scale=q_scale),
        x1, shift, scale, nw1, w_in1, diff_extras + [cos, sin],
        nrm_specs + [pl.BlockSpec((tm1, DIFF_HEAD_DIM), lambda bi, i, j: (i, 0)),
                     pl.BlockSpec((tm1, DIFF_HEAD_DIM), lambda bi, i, j: (i, 0))],
        tm=tm1, tn=tn1, name="diff_inproj_x")
    (qkvg1_c,) = _inproj_call(
        functools.partial(_diff_inproj_kernel, rope=False, tn=tn1, q_scale=q_scale),
        ctx1, shift_c, scale_c, nw1, w_in1, diff_extras + [cos, sin],
        nrm_specs + [pl.BlockSpec((CTX_INPROJ_TM, DIFF_HEAD_DIM), lambda bi, i, j: (0, 0)),
                     pl.BlockSpec((CTX_INPROJ_TM, DIFF_HEAD_DIM), lambda bi, i, j: (0, 0))],
        tm=CTX_INPROJ_TM, tn=tn1, name="diff_inproj_ctx")

    score_bound = q_scale * DIFF_HEAD_DIM * jnp.max(
        jnp.max(jnp.abs(diff_qn_w[0]), axis=-1) * jnp.max(jnp.abs(diff_kn_w[0]), axis=-1))
    attn = functools.partial(_attn_call, qkvg1_x, qkvg1_c, diff_lam[0], diff_subln_w[0], lambda_init)
    o_attn = lax.cond(score_bound <= ATTN_SCORE_BOUND,
                      functools.partial(attn, bounded=True), functools.partial(attn, bounded=False))

    w_out1 = diff_w_out[0].astype(BF16)
    g_col_blk1 = 3 * DIFF_WIDTH // DIFF_WIDTH
    return _finish_call(x1, gate, [o_attn], qkvg1_x, g_col_blk1, w_out1, name="diff_finish_x")
```

```python
import functools
import math

import jax
import jax.numpy as jnp
from jax import lax
from jax.experimental import pallas as pl
from jax.experimental.pallas import tpu as pltpu

F32 = jnp.float32
BF16 = jnp.bfloat16

D_MODEL = 2048
DEPTH = 2
GRID_W = 64
EPS = 1e-6
GLA_HEADS = 4
GLA_KEY_DIM = D_MODEL // 2
GLA_VAL_DIM = D_MODEL
GLA_DK = GLA_KEY_DIM // GLA_HEADS
GLA_DV = GLA_VAL_DIM // GLA_HEADS
GLA_GATE_RANK = 16
GLA_GATE_NORM = 16.0
GLA_CHUNK = 64
GLA_IN_DIM = 2 * GLA_KEY_DIM + 2 * GLA_VAL_DIM
DIFF_HEAD_DIM = 128
DIFF_HEADS = D_MODEL // (2 * DIFF_HEAD_DIM)
DIFF_VAL_DIM = 2 * DIFF_HEAD_DIM
DIFF_WIDTH = DIFF_HEADS * DIFF_VAL_DIM
DIFF_IN_DIM = 4 * DIFF_WIDTH
ROPE_BASE = 10000.0

V7X_LANES = 128
V7X_SUBLANES = 8
V7X_MXU_DIM = 256
VMEM_LIMIT = 56 * 1024 * 1024

ADA_ROWS = V7X_SUBLANES
ADA_TN = 1024
GLA_BLOCK = 256
GLA_INPROJ_TILE = (512, 1024)
DIFF_INPROJ_TILE = (1024, 1024)
CTX_INPROJ_TM = 256
FINISH_TM = 256
ATTN_TQ = 1024
ATTN_TK = 4096
ATTN_TK_ONLINE = 1024
ATTN_KEY_SUB = V7X_MXU_DIM
ATTN_SCORE_BOUND = 60.0


def _dot(a, b):
    return jnp.dot(a, b, preferred_element_type=F32)


def _dot_nt(a, b):
    return lax.dot_general(a, b, (((1,), (1,)), ((), ())), preferred_element_type=F32)


def _params(sem):
    return pltpu.CompilerParams(dimension_semantics=sem, vmem_limit_bytes=VMEM_LIMIT)


def _ada_kernel(c_ref, w_ref, b_ref, o_ref):
    c = c_ref[...]
    s = c * jax.nn.sigmoid(c)
    o_ref[...] = jnp.dot(s, w_ref[...], preferred_element_type=F32,
                         precision=lax.Precision.HIGHEST) + b_ref[...]


def _ada_call(cond, ada_w, ada_b):
    tn = ADA_TN
    n = ada_w.shape[-1]
    return pl.pallas_call(
        _ada_kernel,
        grid=(DEPTH, n // tn),
        in_specs=[
            pl.BlockSpec((ADA_ROWS, D_MODEL), lambda l, j: (0, 0)),
            pl.BlockSpec((None, D_MODEL, tn), lambda l, j: (l, 0, j)),
            pl.BlockSpec((None, 1, tn), lambda l, j: (l, 0, j)),
        ],
        out_specs=pl.BlockSpec((None, ADA_ROWS, tn), lambda l, j: (l, 0, j)),
        out_shape=jax.ShapeDtypeStruct((DEPTH, ADA_ROWS, n), F32),
        compiler_params=_params(("arbitrary", "arbitrary")),
        name="ada_mod",
    )(cond, ada_w, ada_b.reshape(DEPTH, 1, n))


def _normalize_into(x_ref, shift_ref, scale_ref, nw_ref, hn_ref):
    x = x_ref[...]
    ms = jnp.mean(x * x, axis=-1, keepdims=True)
    a = nw_ref[...] * (1.0 + scale_ref[...])
    hn_ref[...] = (x * lax.rsqrt(ms + EPS) * a + shift_ref[...]).astype(BF16)


def _gla_inproj_kernel(x_ref, shift_ref, scale_ref, nw_ref, w_ref, a1_ref, a2_ref, gb_ref,
                       out_ref, la_ref, hn_ref):
    j = pl.program_id(2)

    @pl.when(j == 0)
    def _():
        _normalize_into(x_ref, shift_ref, scale_ref, nw_ref, hn_ref)
        z = _dot(hn_ref[...], a1_ref[...])
        lg = _dot(z.astype(BF16), a2_ref[...]) + gb_ref[...]
        log_sig = jnp.minimum(lg, 0.0) - jnp.log(1.0 + jnp.exp(-jnp.abs(lg)))
        la_ref[...] = log_sig * (1.0 / GLA_GATE_NORM)

    out_ref[...] = _dot(hn_ref[...], w_ref[...]).astype(BF16)


def _rope_rotate(t, cos, sin_signed, first_of_pair):
    partner = jnp.where(first_of_pair, pltpu.roll(t, 96, 1), pltpu.roll(t, 32, 1))
    return t * cos + partner * sin_signed


def _diff_inproj_kernel(x_ref, shift_ref, scale_ref, nw_ref, w_ref, qn_ref, kn_ref,
                        cos_ref, sin_ref, out_ref, hn_ref, *, rope, tn, q_scale):
    j = pl.program_id(2)
    tiles_per_part = DIFF_WIDTH // tn

    @pl.when(j == 0)
    def _():
        _normalize_into(x_ref, shift_ref, scale_ref, nw_ref, hn_ref)

    acc = _dot(hn_ref[...], w_ref[...])

    def head_norm(nrm_ref, scale):
        lane = lax.broadcasted_iota(jnp.int32, (1, DIFF_HEAD_DIM), 1)
        first_of_pair = (lane // (DIFF_HEAD_DIM // 4)) % 2 == 0
        if rope:
            cos = cos_ref[...]
            sin_signed = jnp.where(first_of_pair, -sin_ref[...], sin_ref[...])
        for g in range(tn // DIFF_HEAD_DIM):
            sl = slice(g * DIFF_HEAD_DIM, (g + 1) * DIFF_HEAD_DIM)
            t = acc[:, sl]
            ms = jnp.mean(t * t, axis=-1, keepdims=True)
            t = t * lax.rsqrt(ms + EPS) * nrm_ref[g % 2:g % 2 + 1, :]
            if rope:
                t = _rope_rotate(t, cos, sin_signed, first_of_pair)
            if scale != 1.0:
                t = t * scale
            out_ref[:, sl] = t.astype(BF16)

    @pl.when(j < tiles_per_part)
    def _():
        head_norm(qn_ref, q_scale)

    @pl.when((j >= tiles_per_part) & (j < 2 * tiles_per_part))
    def _():
        head_norm(kn_ref, 1.0)

    @pl.when(j >= 2 * tiles_per_part)
    def _():
        out_ref[...] = acc.astype(BF16)


def _inproj_call(kernel, x, shift, scale, nw, w, extras, extra_specs, *, tm, tn,
                 extra_out=None, name):
    b, t, d = x.shape
    n = w.shape[-1]
    assert t % tm == 0 and n % tn == 0
    in_specs = [
        pl.BlockSpec((None, tm, d), lambda bi, i, j: (bi, i, 0)),
        pl.BlockSpec((None, 1, d), lambda bi, i, j: (bi, 0, 0)),
        pl.BlockSpec((None, 1, d), lambda bi, i, j: (bi, 0, 0)),
        pl.BlockSpec((1, d), lambda bi, i, j: (0, 0)),
        pl.BlockSpec((d, tn), lambda bi, i, j: (0, j)),
    ] + extra_specs
    out_specs = [pl.BlockSpec((None, tm, tn), lambda bi, i, j: (bi, i, j))]
    out_shape = [jax.ShapeDtypeStruct((b, t, n), BF16)]
    if extra_out is not None:
        width, dtype = extra_out
        out_specs.append(pl.BlockSpec((None, tm, width), lambda bi, i, j: (bi, i, 0)))
        out_shape.append(jax.ShapeDtypeStruct((b, t, width), dtype))
    return pl.pallas_call(
        kernel,
        grid=(b, t // tm, n // tn),
        in_specs=in_specs,
        out_specs=out_specs,
        out_shape=out_shape,
        scratch_shapes=[pltpu.VMEM((tm, d), BF16)],
        compiler_params=_params(("arbitrary", "arbitrary", "arbitrary")),
        name=name,
    )(x, shift, scale, nw, w, *extras)


def _gla_tri_matrices():
    idx = jnp.arange(GLA_BLOCK)
    i, j = idx[:, None], idx[None, :]
    same = (i // GLA_CHUNK) == (j // GLA_CHUNK)
    lower, upper = j <= i, j >= i
    return jnp.stack([jnp.stack([lower, lower & same]),
                      jnp.stack([upper, upper & same])]).astype(BF16)


def _split_hi_lo(x):
    hi = x.astype(BF16)
    lo = (x - hi.astype(F32)).astype(BF16)
    return hi, lo


def _dot_tn(a, b):
    return lax.dot_general(a, b, (((0,), (0,)), ((), ())), preferred_element_type=F32)


def _gla_direction(q_ref, k_ref, v_ref, la_ref, tri_ref, st_ref, o_ref, direction):
    fwd = direction == 0
    cs = GLA_CHUNK
    n_sub = GLA_BLOCK // cs
    la_hi, la_lo = _split_hi_lo(la_ref[...])
    tri = tri_ref[direction, 0]
    b = _dot(tri, la_hi) + _dot(tri, la_lo)
    tot = b[GLA_BLOCK - 1:GLA_BLOCK] if fwd else b[0:1]

    def per_chunk_rows(row_of_chunk):
        return jnp.concatenate(
            [jnp.broadcast_to(b[row_of_chunk(c):row_of_chunk(c) + 1], (cs, GLA_DK))
             for c in range(n_sub)], axis=0)

    if fwd:
        mid = per_chunk_rows(lambda c: c * cs + cs // 2 - 1)
        edge = per_chunk_rows(lambda c: max(c * cs - 1, 0))
    else:
        mid = per_chunk_rows(lambda c: c * cs + cs // 2)
        edge = per_chunk_rows(lambda c: min((c + 1) * cs, GLA_BLOCK - 1))

    q = q_ref[...].astype(F32) * (GLA_DK ** -0.5)
    k = k_ref[...].astype(F32)
    v = v_ref[...]
    st = st_ref[...]
    q_in = (q * jnp.exp(b)).astype(BF16)
    k_in = (k * jnp.exp(tot - b)).astype(BF16)
    q_d = (q * jnp.exp(b - mid)).astype(BF16)
    k_d = (k * jnp.exp(mid - b)).astype(BF16)
    q_o = (q * jnp.exp(b - edge)).astype(BF16)

    scores = jnp.where(tri_ref[direction, 1] > 0, _dot_nt(q_d, k_d), 0.0)
    tok = lax.broadcasted_iota(jnp.int32, (GLA_BLOCK, 1), 0)
    off = []
    for c in range(n_sub):
        lo, hi = c * cs, (c + 1) * cs
        first, last, ref_row = (0, lo, lo - 1) if fwd else (hi, GLA_BLOCK, hi)
        if last > first:
            expo = jnp.where((tok >= first) & (tok < last), b[ref_row:ref_row + 1] - b, -1e30)
            k_o = (k * jnp.exp(expo)).astype(BF16)
            off.append(_dot_nt(q_o[lo:hi], k_o))
        else:
            off.append(jnp.zeros((cs, GLA_BLOCK), F32))
    scores = scores + jnp.concatenate(off, axis=0)

    o = _dot_nt(q_in, st.astype(BF16)) + _dot(scores.astype(BF16), v)
    o_ref[...] = o.astype(BF16)
    st_ref[...] = st * jnp.exp(tot) + _dot_tn(v, k_in)


def _gla_scan_kernel(qf_ref, kf_ref, vf_ref, laf_ref, qb_ref, kb_ref, vb_ref, lab_ref,
                     qc_ref, kc_ref, vc_ref, lacf_ref, lacb_ref, tri_ref,
                     of_ref, ob_ref, ocf_ref, ocb_ref, sf_ref, sb_ref):
    t = pl.program_id(2)

    @pl.when(t == 0)
    def _():
        sf_ref[...] = jnp.zeros_like(sf_ref)
        sb_ref[...] = jnp.zeros_like(sb_ref)
        _gla_direction(qc_ref, kc_ref, vc_ref, lacf_ref, tri_ref, sf_ref, ocf_ref, 0)
        _gla_direction(qc_ref, kc_ref, vc_ref, lacb_ref, tri_ref, sb_ref, ocb_ref, 1)

    @pl.when(t > 0)
    def _():
        _gla_direction(qf_ref, kf_ref, vf_ref, laf_ref, tri_ref, sf_ref, of_ref, 0)
        _gla_direction(qb_ref, kb_ref, vb_ref, lab_ref, tri_ref, sb_ref, ob_ref, 1)


def _gla_scan_call(qkvg_x, la_x, qkvg_c, la_c):
    b, seq, _ = qkvg_x.shape
    nb = seq // GLA_BLOCK
    assert qkvg_c.shape[1] == GLA_BLOCK
    k_off = GLA_KEY_DIM // GLA_DK
    v_off = 2 * GLA_KEY_DIM // GLA_DV

    def pos_f(t):
        return jnp.maximum(t - 1, 0)

    def pos_b(t):
        return nb - jnp.maximum(t, 1)

    def pos_c(t):
        return 0

    def specs(pos, la_col_off, with_la=True):
        out = [
            pl.BlockSpec((None, GLA_BLOCK, GLA_DK), lambda bi, h, t: (bi, pos(t), h)),
            pl.BlockSpec((None, GLA_BLOCK, GLA_DK), lambda bi, h, t: (bi, pos(t), k_off + h)),
            pl.BlockSpec((None, GLA_BLOCK, GLA_DV), lambda bi, h, t: (bi, pos(t), v_off + h)),
        ]
        if with_la:
            out.append(pl.BlockSpec((None, GLA_BLOCK, GLA_DK),
                                    lambda bi, h, t: (bi, pos(t), la_col_off + h)))
        return out

    def la_spec(pos, la_col_off):
        return pl.BlockSpec((None, GLA_BLOCK, GLA_DK), lambda bi, h, t: (bi, pos(t), la_col_off + h))

    def o_spec(pos):
        return pl.BlockSpec((None, GLA_BLOCK, GLA_DV), lambda bi, h, t: (bi, pos(t), h))

    tri = _gla_tri_matrices()
    in_specs = (specs(pos_f, 0) + specs(pos_b, GLA_HEADS) + specs(pos_c, 0, with_la=False)
                + [la_spec(pos_c, 0), la_spec(pos_c, GLA_HEADS),
                   pl.BlockSpec(tri.shape, lambda bi, h, t: (0, 0, 0, 0))])
    ox_shape = jax.ShapeDtypeStruct((b, seq, GLA_VAL_DIM), BF16)
    oc_shape = jax.ShapeDtypeStruct((b, GLA_BLOCK, GLA_VAL_DIM), BF16)
    return pl.pallas_call(
        _gla_scan_kernel,
        grid=(b, GLA_HEADS, nb + 1),
        in_specs=in_specs,
        out_specs=[o_spec(pos_f), o_spec(pos_b), o_spec(pos_c), o_spec(pos_c)],
        out_shape=[ox_shape, ox_shape, oc_shape, oc_shape],
        scratch_shapes=[pltpu.VMEM((GLA_DV, GLA_DK), F32), pltpu.VMEM((GLA_DV, GLA_DK), F32)],
        compiler_params=_params(("arbitrary", "arbitrary", "arbitrary")),
        name="gla_scan",
    )(qkvg_x, qkvg_x, qkvg_x, la_x, qkvg_x, qkvg_x, qkvg_x, la_x,
      qkvg_c, qkvg_c, qkvg_c, la_c, la_c, tri)


def _finish_kernel(*refs, head_norm):
    if head_norm:
        x_ref, gate_ref, o1_ref, o2_ref, g_ref, gn_ref, w_ref, out_ref = refs
        o = o1_ref[...].astype(F32) + o2_ref[...].astype(F32)
        parts = []
        for h in range(GLA_HEADS):
            oh = o[:, h * GLA_DV:(h + 1) * GLA_DV]
            ms = jnp.mean(oh * oh, axis=-1, keepdims=True)
            parts.append(oh * lax.rsqrt(ms + EPS) * gn_ref[...])
        o = jnp.concatenate(parts, axis=-1)
    else:
        x_ref, gate_ref, o1_ref, g_ref, w_ref, out_ref = refs
        o = o1_ref[...].astype(F32)
    g = g_ref[...].astype(F32)
    y_in = (o * (g * jax.nn.sigmoid(g))).astype(BF16)
    y = _dot(y_in, w_ref[...])
    out_ref[...] = x_ref[...] + gate_ref[...] * y


def _finish_call(x, gate, o_list, g_arr, g_col_blk, w, *, gn_w=None, name):
    b, t, d = x.shape
    tm = FINISH_TM
    assert t % tm == 0
    width = w.shape[0]
    in_specs = [
        pl.BlockSpec((None, tm, d), lambda bi, i: (bi, i, 0)),
        pl.BlockSpec((None, 1, d), lambda bi, i: (bi, 0, 0)),
    ]
    args = [x, gate]
    for o in o_list:
        in_specs.append(pl.BlockSpec((None, tm, width), lambda bi, i: (bi, i, 0)))
        args.append(o)
    in_specs.append(pl.BlockSpec((None, tm, width), lambda bi, i: (bi, i, g_col_blk)))
    args.append(g_arr)
    if gn_w is not None:
        in_specs.append(pl.BlockSpec((1, gn_w.shape[-1]), lambda bi, i: (0, 0)))
        args.append(gn_w)
    in_specs.append(pl.BlockSpec((width, d), lambda bi, i: (0, 0)))
    args.append(w)
    return pl.pallas_call(
        functools.partial(_finish_kernel, head_norm=gn_w is not None),
        grid=(b, t // tm),
        in_specs=in_specs,
        out_specs=pl.BlockSpec((None, tm, d), lambda bi, i: (bi, i, 0)),
        out_shape=jax.ShapeDtypeStruct((b, t, d), F32),
        compiler_params=_params(("arbitrary", "arbitrary")),
        name=name,
    )(*args)


def _attn_finalize(acc1, l1, acc2, l2, lam_ref, sw_ref, o_ref, lambda_init):
    lv = lam_ref[...]
    lam = (jnp.exp(jnp.sum(lv[0:1] * lv[1:2], axis=-1, keepdims=True))
           - jnp.exp(jnp.sum(lv[2:3] * lv[3:4], axis=-1, keepdims=True)) + lambda_init)
    o = acc1 / l1 - lam * (acc2 / l2)
    ms = jnp.mean(o * o, axis=-1, keepdims=True)
    o = o * lax.rsqrt(ms + EPS) * sw_ref[...] * (1.0 - lambda_init)
    o_ref[...] = o.astype(BF16)


def _attn_online_kernel(q1_ref, q2_ref, k1_ref, k2_ref, v_ref, k1c_ref, k2c_ref, vc_ref,
                        lam_ref, sw_ref, o_ref, m_ref, l_ref, acc_ref, *, tk, lambda_init):
    n_keys = v_ref.shape[0]
    m_ref[...] = jnp.full_like(m_ref, -1e30)
    l_ref[...] = jnp.zeros_like(l_ref)
    acc_ref[...] = jnp.zeros_like(acc_ref)

    def update(mp, q_ref, k_blk, v_blk):
        s = _dot_nt(q_ref[...], k_blk)
        m_old = m_ref[mp]
        m_new = jnp.maximum(m_old, jnp.max(s, axis=1, keepdims=True))
        alpha = jnp.exp2(m_old - m_new)
        p = jnp.exp2(s - m_new)
        l_ref[mp] = alpha * l_ref[mp] + jnp.sum(p, axis=1, keepdims=True)
        acc_ref[mp] = alpha * acc_ref[mp] + _dot(p.astype(BF16), v_blk)
        m_ref[mp] = m_new

    def body(c, carry):
        off = pl.multiple_of(c * tk, tk)
        v_blk = v_ref[pl.ds(off, tk), :]
        update(0, q1_ref, k1_ref[pl.ds(off, tk), :], v_blk)
        update(1, q2_ref, k2_ref[pl.ds(off, tk), :], v_blk)
        return carry

    lax.fori_loop(0, n_keys // tk, body, 0)
    update(0, q1_ref, k1c_ref[...], vc_ref[...])
    update(1, q2_ref, k2c_ref[...], vc_ref[...])
    _attn_finalize(acc_ref[0], l_ref[0], acc_ref[1], l_ref[1], lam_ref, sw_ref, o_ref, lambda_init)


def _attn_bounded_kernel(q1_ref, q2_ref, k1_ref, k2_ref, v_ref, k1c_ref, k2c_ref, vc_ref,
                         lam_ref, sw_ref, o_ref, l_ref, acc_ref, *, tk, lambda_init):
    n_keys = v_ref.shape[0]
    l_ref[...] = jnp.zeros_like(l_ref)
    acc_ref[...] = jnp.zeros_like(acc_ref)

    def sub_block(q, k_blk, v_blk):
        s = _dot_nt(q, k_blk)
        p = jnp.exp2(s)
        part = p[:, :V7X_LANES] + p[:, V7X_LANES:]
        return part, _dot(p.astype(BF16), v_blk)

    def body(c, carry):
        for mp, (q_ref, k_ref) in enumerate(((q1_ref, k1_ref), (q2_ref, k2_ref))):
            q = q_ref[...]
            acc = None
            lsum = None
            for sb in range(tk // ATTN_KEY_SUB):
                off = pl.multiple_of(c * tk + sb * ATTN_KEY_SUB, ATTN_KEY_SUB)
                part, d = sub_block(q, k_ref[pl.ds(off, ATTN_KEY_SUB), :],
                                    v_ref[pl.ds(off, ATTN_KEY_SUB), :])
                lsum = part if lsum is None else lsum + part
                acc = d if acc is None else acc + d
            l_ref[mp] += lsum
            acc_ref[mp] += acc
        return carry

    lax.fori_loop(0, n_keys // tk, body, 0)

    for mp, (q_ref, kc_ref) in enumerate(((q1_ref, k1c_ref), (q2_ref, k2c_ref))):
        part, d = sub_block(q_ref[...], kc_ref[...], vc_ref[...])
        l_ref[mp] += part
        acc_ref[mp] += d

    l1 = jnp.sum(l_ref[0], axis=-1, keepdims=True)
    l2 = jnp.sum(l_ref[1], axis=-1, keepdims=True)
    _attn_finalize(acc_ref[0], l1, acc_ref[1], l2, lam_ref, sw_ref, o_ref, lambda_init)


def _attn_call(qkvg_x, qkvg_c, lam_v, subln_w, lambda_init, *, bounded):
    b, seq, _ = qkvg_x.shape
    n_ctx = qkvg_c.shape[1]
    tq = ATTN_TQ
    tk = ATTN_TK if bounded else ATTN_TK_ONLINE
    assert seq % tq == 0 and seq % tk == 0 and tk % ATTN_KEY_SUB == 0 and n_ctx == ATTN_KEY_SUB
    d = DIFF_HEAD_DIM
    k_off = DIFF_WIDTH // d
    v_off = 2 * DIFF_WIDTH // DIFF_VAL_DIM
    in_specs = [
        pl.BlockSpec((None, tq, d), lambda bi, h, i: (bi, i, 2 * h)),
        pl.BlockSpec((None, tq, d), lambda bi, h, i: (bi, i, 2 * h + 1)),
        pl.BlockSpec((None, seq, d), lambda bi, h, i: (bi, 0, k_off + 2 * h)),
        pl.BlockSpec((None, seq, d), lambda bi, h, i: (bi, 0, k_off + 2 * h + 1)),
        pl.BlockSpec((None, seq, DIFF_VAL_DIM), lambda bi, h, i: (bi, 0, v_off + h)),
        pl.BlockSpec((None, n_ctx, d), lambda bi, h, i: (bi, 0, k_off + 2 * h)),
        pl.BlockSpec((None, n_ctx, d), lambda bi, h, i: (bi, 0, k_off + 2 * h + 1)),
        pl.BlockSpec((None, n_ctx, DIFF_VAL_DIM), lambda bi, h, i: (bi, 0, v_off + h)),
        pl.BlockSpec(lam_v.shape, lambda bi, h, i: (0, 0)),
        pl.BlockSpec((1, DIFF_VAL_DIM), lambda bi, h, i: (0, 0)),
    ]
    if bounded:
        body = functools.partial(_attn_bounded_kernel, tk=tk, lambda_init=lambda_init)
        scratch = [pltpu.VMEM((2, tq, V7X_LANES), F32), pltpu.VMEM((2, tq, DIFF_VAL_DIM), F32)]
    else:
        body = functools.partial(_attn_online_kernel, tk=tk, lambda_init=lambda_init)
        scratch = [pltpu.VMEM((2, tq, 1), F32), pltpu.VMEM((2, tq, 1), F32),
                   pltpu.VMEM((2, tq, DIFF_VAL_DIM), F32)]
    return pl.pallas_call(
        body,
        grid=(b, DIFF_HEADS, seq // tq),
        in_specs=in_specs,
        out_specs=pl.BlockSpec((None, tq, DIFF_VAL_DIM), lambda bi, h, i: (bi, i, h)),
        out_shape=jax.ShapeDtypeStruct((b, seq, DIFF_WIDTH), BF16),
        scratch_shapes=scratch,
        compiler_params=_params(("arbitrary", "arbitrary", "arbitrary")),
        name="diff_attn_bounded" if bounded else "diff_attn_online",
    )(qkvg_x, qkvg_x, qkvg_x, qkvg_x, qkvg_x, qkvg_c, qkvg_c, qkvg_c,
      lam_v, subln_w.reshape(1, DIFF_VAL_DIM))


def _axial_rope_tables(n_tokens):
    rows_n = n_tokens // GRID_W
    row = jnp.repeat(jnp.arange(rows_n), GRID_W).astype(F32)
    col = jnp.tile(jnp.arange(GRID_W), rows_n).astype(F32)
    half = DIFF_HEAD_DIM // 2
    inv_freq = ROPE_BASE ** (-jnp.arange(0, half, 2, dtype=F32) / half)
    ang_r = row[:, None] * inv_freq
    ang_c = col[:, None] * inv_freq
    ang = jnp.concatenate([ang_r, ang_r, ang_c, ang_c], axis=-1)
    return jnp.cos(ang), jnp.sin(ang)


def kernel(x, c, ctx, c_ctx, norm_w, ada_w, ada_b, gla_w_in, gla_gate_a1, gla_gate_a2, gla_gate_b,
           gla_gn_w, gla_w_out, diff_w_in, diff_qn_w, diff_kn_w, diff_lam, diff_subln_w, diff_w_out):
    b, seq, d = x.shape
    ctx_len = ctx.shape[1]
    assert ctx_len == GLA_BLOCK and seq % GLA_BLOCK == 0 and d == D_MODEL

    cond = jnp.zeros((ADA_ROWS, d), F32).at[:b].set(c).at[b].set(c_ctx)
    mod = _ada_call(cond, ada_w, ada_b)

    def mod_rows(layer, part):
        m = mod[layer, :, part * d:(part + 1) * d]
        lat = m[:b, None, :]
        cx = jnp.broadcast_to(m[b][None, None, :], (b, 1, d))
        return lat, cx

    shift, shift_c = mod_rows(0, 0)
    scale, scale_c = mod_rows(0, 1)
    gate, gate_c = mod_rows(0, 2)
    nw0 = norm_w[0].reshape(1, d)
    w_in0 = gla_w_in[0].astype(BF16)
    a1 = jnp.zeros((d, V7X_LANES), F32)
    a1 = a1.at[:, :GLA_GATE_RANK].set(gla_gate_a1[0, 0]).at[:, GLA_GATE_RANK:2 * GLA_GATE_RANK].set(
        gla_gate_a1[0, 1]).astype(BF16)
    a2 = jnp.zeros((V7X_LANES, 2 * GLA_KEY_DIM), F32)
    a2 = a2.at[:GLA_GATE_RANK, :GLA_KEY_DIM].set(gla_gate_a2[0, 0])
    a2 = a2.at[GLA_GATE_RANK:2 * GLA_GATE_RANK, GLA_KEY_DIM:].set(gla_gate_a2[0, 1]).astype(BF16)
    gb = gla_gate_b[0].reshape(1, 2 * GLA_KEY_DIM)
    gla_extras = [a1, a2, gb]
    gla_extra_specs = [
        pl.BlockSpec(a1.shape, lambda bi, i, j: (0, 0)),
        pl.BlockSpec(a2.shape, lambda bi, i, j: (0, 0)),
        pl.BlockSpec(gb.shape, lambda bi, i, j: (0, 0)),
    ]
    la_out = (2 * GLA_KEY_DIM, F32)
    tm0, tn0 = GLA_INPROJ_TILE
    qkvg0_x, la0_x = _inproj_call(
        _gla_inproj_kernel, x, shift, scale, nw0, w_in0, gla_extras, gla_extra_specs,
        tm=tm0, tn=tn0, extra_out=la_out, name="gla_inproj_x")
    qkvg0_c, la0_c = _inproj_call(
        _gla_inproj_kernel, ctx, shift_c, scale_c, nw0, w_in0, gla_extras, gla_extra_specs,
        tm=CTX_INPROJ_TM, tn=tn0, extra_out=la_out, name="gla_inproj_ctx")

    o_f, o_b, oc_f, oc_b = _gla_scan_call(qkvg0_x, la0_x, qkvg0_c, la0_c)

    w_out0 = gla_w_out[0].astype(BF16)
    gn_w = gla_gn_w[0].reshape(1, GLA_DV)
    g_col_blk0 = (2 * GLA_KEY_DIM + GLA_VAL_DIM) // GLA_VAL_DIM
    x1 = _finish_call(x, gate, [o_f, o_b], qkvg0_x, g_col_blk0, w_out0, gn_w=gn_w,
                      name="gla_finish_x")
    ctx1 = _finish_call(ctx, gate_c, [oc_f, oc_b], qkvg0_c, g_col_blk0, w_out0, gn_w=gn_w,
                        name="gla_finish_ctx")

    lambda_init = 0.8 - 0.6 * math.exp(-0.3 * 1)
    shift, shift_c = mod_rows(1, 0)
    scale, scale_c = mod_rows(1, 1)
    gate, _ = mod_rows(1, 2)
    nw1 = norm_w[1].reshape(1, d)
    w_in1 = diff_w_in[0].astype(BF16)
    cos, sin = _axial_rope_tables(seq)
    q_scale = math.log2(math.e) * DIFF_HEAD_DIM ** -0.5
    tm1, tn1 = DIFF_INPROJ_TILE
    diff_extras = [diff_qn_w[0], diff_kn_w[0]]
    nrm_specs = [
        pl.BlockSpec((2, DIFF_HEAD_DIM), lambda bi, i, j: (0, 0)),
        pl.BlockSpec((2, DIFF_HEAD_DIM), lambda bi, i, j: (0, 0)),
    ]
    (qkvg1_x,) = _inproj_call(
        functools.partial(_diff_inproj_kernel, rope=True, tn=tn1, q_scale=q_scale),
        x1, shift, scale, nw1, w_in1, diff_extras + [cos, sin],
        nrm_specs + [pl.BlockSpec((tm1, DIFF_HEAD_DIM), lambda bi, i, j: (i, 0)),
                     pl.BlockSpec((tm1, DIFF_HEAD_DIM), lambda bi, i, j: (i, 0))],
        tm=tm1, tn=tn1, name="diff_inproj_x")
    (qkvg1_c,) = _inproj_call(
        functools.partial(_diff_inproj_kernel, rope=False, tn=tn1, q_scale=q_scale),
        ctx1, shift_c, scale_c, nw1, w_in1, diff_extras + [cos, sin],
        nrm_specs + [pl.BlockSpec((CTX_INPROJ_TM, DIFF_HEAD_DIM), lambda bi, i, j: (0, 0)),
                     pl.BlockSpec((CTX_INPROJ_TM, DIFF_HEAD_DIM), lambda bi, i, j: (0, 0))],
        tm=CTX_INPROJ_TM, tn=tn1, name="diff_inproj_ctx")

    score_bound = q_scale * DIFF_HEAD_DIM * jnp.max(
        jnp.max(jnp.abs(diff_qn_w[0]), axis=-1) * jnp.max(jnp.abs(diff_kn_w[0]), axis=-1))
    attn = functools.partial(_attn_call, qkvg1_x, qkvg1_c, diff_lam[0], diff_subln_w[0], lambda_init)
    o_attn = lax.cond(score_bound <= ATTN_SCORE_BOUND,
                      functools.partial(attn, bounded=True), functools.partial(attn, bounded=False))

    w_out1 = diff_w_out[0].astype(BF16)
    g_col_blk1 = 3 * DIFF_WIDTH // DIFF_WIDTH
    return _finish_call(x1, gate, [o_attn], qkvg1_x, g_col_blk1, w_out1, name="diff_finish_x")
```

```python
import functools
import math

import jax
import jax.numpy as jnp
from jax import lax
from jax.experimental import pallas as pl
from jax.experimental.pallas import tpu as pltpu

F32 = jnp.float32
BF16 = jnp.bfloat16

D_MODEL = 2048
DEPTH = 2
GRID_W = 64
EPS = 1e-6
GLA_HEADS = 4
GLA_KEY_DIM = D_MODEL // 2
GLA_VAL_DIM = D_MODEL
GLA_DK = GLA_KEY_DIM // GLA_HEADS
GLA_DV = GLA_VAL_DIM // GLA_HEADS
GLA_GATE_RANK = 16
GLA_GATE_NORM = 16.0
GLA_CHUNK = 64
GLA_IN_DIM = 2 * GLA_KEY_DIM + 2 * GLA_VAL_DIM
DIFF_HEAD_DIM = 128
DIFF_HEADS = D_MODEL // (2 * DIFF_HEAD_DIM)
DIFF_VAL_DIM = 2 * DIFF_HEAD_DIM
DIFF_WIDTH = DIFF_HEADS * DIFF_VAL_DIM
DIFF_IN_DIM = 4 * DIFF_WIDTH
ROPE_BASE = 10000.0

V7X_LANES = 128
V7X_SUBLANES = 8
V7X_MXU_DIM = 256
VMEM_LIMIT = 56 * 1024 * 1024

ADA_ROWS = V7X_SUBLANES
ADA_TN = 1024
GLA_BLOCK = 256
GLA_INPROJ_TILE = (512, 2048)
DIFF_INPROJ_TILE = (1024, 1024)
CTX_INPROJ_TM = 256
FINISH_TM = 512
ATTN_TQ = 1024
ATTN_TK = 8192
ATTN_TQ_ONLINE = 512
ATTN_TK_ONLINE = 1024
ATTN_KEY_SUB = V7X_MXU_DIM
ATTN_SCORE_BOUND = 60.0


def _dot(a, b):
    return jnp.dot(a, b, preferred_element_type=F32)


def _dot_nt(a, b):
    return lax.dot_general(a, b, (((1,), (1,)), ((), ())), preferred_element_type=F32)


def _params(sem):
    return pltpu.CompilerParams(dimension_semantics=sem, vmem_limit_bytes=VMEM_LIMIT)


def _ada_kernel(c_ref, w_ref, b_ref, o_ref):
    c = c_ref[...]
    s = c * jax.nn.sigmoid(c)
    o_ref[...] = jnp.dot(s, w_ref[...], preferred_element_type=F32,
                         precision=lax.Precision.HIGHEST) + b_ref[...]


def _ada_call(cond, ada_w, ada_b):
    tn = ADA_TN
    n = ada_w.shape[-1]
    return pl.pallas_call(
        _ada_kernel,
        grid=(DEPTH, n // tn),
        in_specs=[
            pl.BlockSpec((ADA_ROWS, D_MODEL), lambda l, j: (0, 0)),
            pl.BlockSpec((None, D_MODEL, tn), lambda l, j: (l, 0, j)),
            pl.BlockSpec((None, 1, tn), lambda l, j: (l, 0, j)),
        ],
        out_specs=pl.BlockSpec((None, ADA_ROWS, tn), lambda l, j: (l, 0, j)),
        out_shape=jax.ShapeDtypeStruct((DEPTH, ADA_ROWS, n), F32),
        compiler_params=_params(("arbitrary", "arbitrary")),
        name="ada_mod",
    )(cond, ada_w, ada_b.reshape(DEPTH, 1, n))


def _normalize_into(x_ref, shift_ref, scale_ref, nw_ref, hn_ref):
    x = x_ref[...]
    ms = jnp.mean(x * x, axis=-1, keepdims=True)
    a = nw_ref[...] * (1.0 + scale_ref[...])
    hn_ref[...] = (x * lax.rsqrt(ms + EPS) * a + shift_ref[...]).astype(BF16)


def _gla_inproj_kernel(x_ref, shift_ref, scale_ref, nw_ref, w_ref, a1_ref, a2_ref, gb_ref,
                       out_ref, la_ref, hn_ref):
    j = pl.program_id(2)

    @pl.when(j == 0)
    def _():
        _normalize_into(x_ref, shift_ref, scale_ref, nw_ref, hn_ref)
        z = _dot(hn_ref[...], a1_ref[...])
        lg = _dot(z.astype(BF16), a2_ref[...]) + gb_ref[...]
        log_sig = jnp.minimum(lg, 0.0) - jnp.log(1.0 + jnp.exp(-jnp.abs(lg)))
        la_ref[...] = (log_sig * (1.0 / GLA_GATE_NORM)).astype(la_ref.dtype)

    out_ref[...] = _dot(hn_ref[...], w_ref[...]).astype(BF16)


def _pair_layout(a):
    quarter = DIFF_HEAD_DIM // 4
    lead = a.shape[:-1]
    n_heads = a.shape[-1] // DIFF_HEAD_DIM
    return a.reshape(*lead, n_heads, 2, 2, quarter).swapaxes(-2, -3).reshape(a.shape)


def _rope_rotate(t, cos, sin_signed):
    return t * cos + pltpu.roll(t, DIFF_HEAD_DIM // 2, 1) * sin_signed


def _diff_inproj_kernel(x_ref, shift_ref, scale_ref, nw_ref, w_ref, qn_ref, kn_ref,
                        cos_ref, sin_ref, out_ref, hn_ref, *, rope, tn, q_scale):
    j = pl.program_id(2)
    tiles_per_part = DIFF_WIDTH // tn

    @pl.when(j == 0)
    def _():
        _normalize_into(x_ref, shift_ref, scale_ref, nw_ref, hn_ref)

    acc = _dot(hn_ref[...], w_ref[...])

    def head_norm(nrm_ref, scale):
        if rope:
            cos = cos_ref[...]
            sin_signed = sin_ref[...]
        for g in range(tn // DIFF_HEAD_DIM):
            sl = slice(g * DIFF_HEAD_DIM, (g + 1) * DIFF_HEAD_DIM)
            t = acc[:, sl]
            ms = jnp.mean(t * t, axis=-1, keepdims=True)
            t = t * lax.rsqrt(ms + EPS) * nrm_ref[g % 2:g % 2 + 1, :]
            if rope:
                t = _rope_rotate(t, cos, sin_signed)
            if scale != 1.0:
                t = t * scale
            out_ref[:, sl] = t.astype(BF16)

    @pl.when(j < tiles_per_part)
    def _():
        head_norm(qn_ref, q_scale)

    @pl.when((j >= tiles_per_part) & (j < 2 * tiles_per_part))
    def _():
        head_norm(kn_ref, 1.0)

    @pl.when(j >= 2 * tiles_per_part)
    def _():
        out_ref[...] = acc.astype(BF16)


def _inproj_call(kernel, x, shift, scale, nw, w, extras, extra_specs, *, tm, tn,
                 extra_out=None, name):
    b, t, d = x.shape
    n = w.shape[-1]
    assert t % tm == 0 and n % tn == 0
    in_specs = [
        pl.BlockSpec((None, tm, d), lambda bi, i, j: (bi, i, 0)),
        pl.BlockSpec((None, 1, d), lambda bi, i, j: (bi, 0, 0)),
        pl.BlockSpec((None, 1, d), lambda bi, i, j: (bi, 0, 0)),
        pl.BlockSpec((1, d), lambda bi, i, j: (0, 0)),
        pl.BlockSpec((d, tn), lambda bi, i, j: (0, j)),
    ] + extra_specs
    out_specs = [pl.BlockSpec((None, tm, tn), lambda bi, i, j: (bi, i, j))]
    out_shape = [jax.ShapeDtypeStruct((b, t, n), BF16)]
    if extra_out is not None:
        width, dtype = extra_out
        out_specs.append(pl.BlockSpec((None, tm, width), lambda bi, i, j: (bi, i, 0)))
        out_shape.append(jax.ShapeDtypeStruct((b, t, width), dtype))
    return pl.pallas_call(
        kernel,
        grid=(b, t // tm, n // tn),
        in_specs=in_specs,
        out_specs=out_specs,
        out_shape=out_shape,
        scratch_shapes=[pltpu.VMEM((tm, d), BF16)],
        compiler_params=_params(("arbitrary", "arbitrary", "arbitrary")),
        name=name,
    )(x, shift, scale, nw, w, *extras)


def _gla_tri_matrices():
    idx = jnp.arange(GLA_BLOCK)
    i, j = idx[:, None], idx[None, :]
    same = (i // GLA_CHUNK) == (j // GLA_CHUNK)
    lower, upper = j <= i, j >= i
    return jnp.stack([jnp.stack([lower, lower & same]),
                      jnp.stack([upper, upper & same])]).astype(BF16)


def _dot_tn(a, b):
    return lax.dot_general(a, b, (((0,), (0,)), ((), ())), preferred_element_type=F32)


def _gla_direction(q_ref, k_ref, v_ref, la_ref, tri_ref, st_ref, o_ref, direction):
    fwd = direction == 0
    cs = GLA_CHUNK
    n_sub = GLA_BLOCK // cs
    b = _dot(tri_ref[direction, 0], la_ref[...])
    tot = b[GLA_BLOCK - 1:GLA_BLOCK] if fwd else b[0:1]

    def per_chunk_rows(row_of_chunk):
        return jnp.concatenate(
            [jnp.broadcast_to(b[row_of_chunk(c):row_of_chunk(c) + 1], (cs, GLA_DK))
             for c in range(n_sub)], axis=0)

    if fwd:
        mid = per_chunk_rows(lambda c: c * cs + cs // 2 - 1)
        edge = per_chunk_rows(lambda c: max(c * cs - 1, 0))
    else:
        mid = per_chunk_rows(lambda c: c * cs + cs // 2)
        edge = per_chunk_rows(lambda c: min((c + 1) * cs, GLA_BLOCK - 1))

    q = q_ref[...].astype(F32) * (GLA_DK ** -0.5)
    k = k_ref[...].astype(F32)
    v = v_ref[...]
    st = st_ref[...]
    q_in = (q * jnp.exp(b)).astype(BF16)
    k_in = (k * jnp.exp(tot - b)).astype(BF16)
    q_d = (q * jnp.exp(b - mid)).astype(BF16)
    k_d = (k * jnp.exp(mid - b)).astype(BF16)
    q_o = (q * jnp.exp(b - edge)).astype(BF16)

    scores = jnp.where(tri_ref[direction, 1] > 0, _dot_nt(q_d, k_d), 0.0)
    tok = lax.broadcasted_iota(jnp.int32, (GLA_BLOCK, 1), 0)
    off = []
    for c in range(n_sub):
        lo, hi = c * cs, (c + 1) * cs
        first, last, ref_row = (0, lo, lo - 1) if fwd else (hi, GLA_BLOCK, hi)
        if last > first:
            expo = jnp.where((tok >= first) & (tok < last), b[ref_row:ref_row + 1] - b, -1e30)
            k_o = (k * jnp.exp(expo)).astype(BF16)
            off.append(_dot_nt(q_o[lo:hi], k_o))
        else:
            off.append(jnp.zeros((cs, GLA_BLOCK), F32))
    scores = scores + jnp.concatenate(off, axis=0)

    o = _dot_nt(q_in, st.astype(BF16)) + _dot(scores.astype(BF16), v)
    o_ref[...] = o.astype(BF16)
    st_ref[...] = st * jnp.exp(tot) + _dot_tn(v, k_in)


def _gla_scan_kernel(qf_ref, kf_ref, vf_ref, laf_ref, qb_ref, kb_ref, vb_ref, lab_ref,
                     qc_ref, kc_ref, vc_ref, lacf_ref, lacb_ref, tri_ref,
                     of_ref, ob_ref, ocf_ref, ocb_ref, sf_ref, sb_ref):
    t = pl.program_id(2)

    @pl.when(t == 0)
    def _():
        sf_ref[...] = jnp.zeros_like(sf_ref)
        sb_ref[...] = jnp.zeros_like(sb_ref)
        _gla_direction(qc_ref, kc_ref, vc_ref, lacf_ref, tri_ref, sf_ref, ocf_ref, 0)
        _gla_direction(qc_ref, kc_ref, vc_ref, lacb_ref, tri_ref, sb_ref, ocb_ref, 1)

    @pl.when(t > 0)
    def _():
        _gla_direction(qf_ref, kf_ref, vf_ref, laf_ref, tri_ref, sf_ref, of_ref, 0)
        _gla_direction(qb_ref, kb_ref, vb_ref, lab_ref, tri_ref, sb_ref, ob_ref, 1)


def _gla_scan_call(qkvg_x, la_x, qkvg_c, la_c):
    b, seq, _ = qkvg_x.shape
    nb = seq // GLA_BLOCK
    assert qkvg_c.shape[1] == GLA_BLOCK
    k_off = GLA_KEY_DIM // GLA_DK
    v_off = 2 * GLA_KEY_DIM // GLA_DV

    def pos_f(t):
        return jnp.maximum(t - 1, 0)

    def pos_b(t):
        return nb - jnp.maximum(t, 1)

    def pos_c(t):
        return 0

    def specs(pos, la_col_off, with_la=True):
        out = [
            pl.BlockSpec((None, GLA_BLOCK, GLA_DK), lambda bi, h, t: (bi, pos(t), h)),
            pl.BlockSpec((None, GLA_BLOCK, GLA_DK), lambda bi, h, t: (bi, pos(t), k_off + h)),
            pl.BlockSpec((None, GLA_BLOCK, GLA_DV), lambda bi, h, t: (bi, pos(t), v_off + h)),
        ]
        if with_la:
            out.append(pl.BlockSpec((None, GLA_BLOCK, GLA_DK),
                                    lambda bi, h, t: (bi, pos(t), la_col_off + h)))
        return out

    def la_spec(pos, la_col_off):
        return pl.BlockSpec((None, GLA_BLOCK, GLA_DK), lambda bi, h, t: (bi, pos(t), la_col_off + h))

    def o_spec(pos):
        return pl.BlockSpec((None, GLA_BLOCK, GLA_DV), lambda bi, h, t: (bi, pos(t), h))

    tri = _gla_tri_matrices()
    in_specs = (specs(pos_f, 0) + specs(pos_b, GLA_HEADS) + specs(pos_c, 0, with_la=False)
                + [la_spec(pos_c, 0), la_spec(pos_c, GLA_HEADS),
                   pl.BlockSpec(tri.shape, lambda bi, h, t: (0, 0, 0, 0))])
    ox_shape = jax.ShapeDtypeStruct((b, seq, GLA_VAL_DIM), BF16)
    oc_shape = jax.ShapeDtypeStruct((b, GLA_BLOCK, GLA_VAL_DIM), BF16)
    return pl.pallas_call(
        _gla_scan_kernel,
        grid=(b, GLA_HEADS, nb + 1),
        in_specs=in_specs,
        out_specs=[o_spec(pos_f), o_spec(pos_b), o_spec(pos_c), o_spec(pos_c)],
        out_shape=[ox_shape, ox_shape, oc_shape, oc_shape],
        scratch_shapes=[pltpu.VMEM((GLA_DV, GLA_DK), F32), pltpu.VMEM((GLA_DV, GLA_DK), F32)],
        compiler_params=_params(("arbitrary", "arbitrary", "arbitrary")),
        name="gla_scan",
    )(qkvg_x, qkvg_x, qkvg_x, la_x, qkvg_x, qkvg_x, qkvg_x, la_x,
      qkvg_c, qkvg_c, qkvg_c, la_c, la_c, tri)


def _finish_kernel(*refs, head_norm):
    if head_norm:
        x_ref, gate_ref, o1_ref, o2_ref, g_ref, gn_ref, w_ref, out_ref = refs
        o = o1_ref[...].astype(F32) + o2_ref[...].astype(F32)
        parts = []
        for h in range(GLA_HEADS):
            oh = o[:, h * GLA_DV:(h + 1) * GLA_DV]
            ms = jnp.mean(oh * oh, axis=-1, keepdims=True)
            parts.append(oh * lax.rsqrt(ms + EPS) * gn_ref[...])
        o = jnp.concatenate(parts, axis=-1)
    else:
        x_ref, gate_ref, o1_ref, g_ref, w_ref, out_ref = refs
        o = o1_ref[...].astype(F32)
    g = g_ref[...].astype(F32)
    y_in = (o * (g * jax.nn.sigmoid(g))).astype(BF16)
    y = _dot(y_in, w_ref[...])
    out_ref[...] = x_ref[...] + gate_ref[...] * y


def _finish_call(x, gate, o_list, g_arr, g_col_blk, w, *, gn_w=None, name):
    b, t, d = x.shape
    tm = min(FINISH_TM, t)
    assert t % tm == 0
    width = w.shape[0]
    in_specs = [
        pl.BlockSpec((None, tm, d), lambda bi, i: (bi, i, 0)),
        pl.BlockSpec((None, 1, d), lambda bi, i: (bi, 0, 0)),
    ]
    args = [x, gate]
    for o in o_list:
        in_specs.append(pl.BlockSpec((None, tm, width), lambda bi, i: (bi, i, 0)))
        args.append(o)
    in_specs.append(pl.BlockSpec((None, tm, width), lambda bi, i: (bi, i, g_col_blk)))
    args.append(g_arr)
    if gn_w is not None:
        in_specs.append(pl.BlockSpec((1, gn_w.shape[-1]), lambda bi, i: (0, 0)))
        args.append(gn_w)
    in_specs.append(pl.BlockSpec((width, d), lambda bi, i: (0, 0)))
    args.append(w)
    return pl.pallas_call(
        functools.partial(_finish_kernel, head_norm=gn_w is not None),
        grid=(b, t // tm),
        in_specs=in_specs,
        out_specs=pl.BlockSpec((None, tm, d), lambda bi, i: (bi, i, 0)),
        out_shape=jax.ShapeDtypeStruct((b, t, d), F32),
        compiler_params=_params(("arbitrary", "arbitrary")),
        name=name,
    )(*args)


def _attn_finalize(acc1, l1, acc2, l2, lam_ref, sw_ref, o_ref, lambda_init):
    lv = lam_ref[...]
    lam = (jnp.exp(jnp.sum(lv[0:1] * lv[1:2], axis=-1, keepdims=True))
           - jnp.exp(jnp.sum(lv[2:3] * lv[3:4], axis=-1, keepdims=True)) + lambda_init)
    o = acc1 / l1 - lam * (acc2 / l2)
    ms = jnp.mean(o * o, axis=-1, keepdims=True)
    o = o * lax.rsqrt(ms + EPS) * sw_ref[...] * (1.0 - lambda_init)
    o_ref[...] = o.astype(BF16)


def _attn_online_kernel(q1_ref, q2_ref, k1_ref, k2_ref, v_ref, k1c_ref, k2c_ref, vc_ref,
                        lam_ref, sw_ref, o_ref, m_ref, l_ref, acc_ref, *, tk, lambda_init):
    n_keys = v_ref.shape[0]
    m_ref[...] = jnp.full_like(m_ref, -1e30)
    l_ref[...] = jnp.zeros_like(l_ref)
    acc_ref[...] = jnp.zeros_like(acc_ref)

    def update(mp, q_ref, k_blk, v_blk):
        s = _dot_nt(q_ref[...], k_blk)
        m_old = m_ref[mp]
        m_new = jnp.maximum(m_old, jnp.max(s, axis=1, keepdims=True))
        alpha = jnp.exp2(m_old - m_new)
        p = jnp.exp2(s - m_new)
        l_ref[mp] = alpha * l_ref[mp] + jnp.sum(p, axis=1, keepdims=True)
        acc_ref[mp] = alpha * acc_ref[mp] + _dot(p.astype(BF16), v_blk)
        m_ref[mp] = m_new

    def body(c, carry):
        off = pl.multiple_of(c * tk, tk)
        v_blk = v_ref[pl.ds(off, tk), :]
        update(0, q1_ref, k1_ref[pl.ds(off, tk), :], v_blk)
        update(1, q2_ref, k2_ref[pl.ds(off, tk), :], v_blk)
        return carry

    lax.fori_loop(0, n_keys // tk, body, 0)
    update(0, q1_ref, k1c_ref[...], vc_ref[...])
    update(1, q2_ref, k2c_ref[...], vc_ref[...])
    _attn_finalize(acc_ref[0], l_ref[0], acc_ref[1], l_ref[1], lam_ref, sw_ref, o_ref, lambda_init)


def _attn_bounded_kernel(q1_ref, q2_ref, k1_ref, k2_ref, v_ref, k1c_ref, k2c_ref, vc_ref,
                         lam_ref, sw_ref, o_ref, l_ref, acc_ref, *, tk, lambda_init):
    n_keys = v_ref.shape[0]
    l_ref[...] = jnp.zeros_like(l_ref)
    acc_ref[...] = jnp.zeros_like(acc_ref)

    def sub_block(q, k_blk, v_blk):
        s = _dot_nt(q, k_blk)
        p = jnp.exp2(s)
        part = p[:, :V7X_LANES] + p[:, V7X_LANES:]
        return part, _dot(p.astype(BF16), v_blk)

    def body(c, carry):
        for mp, (q_ref, k_ref) in enumerate(((q1_ref, k1_ref), (q2_ref, k2_ref))):
            q = q_ref[...]
            acc = None
            lsum = None
            for sb in range(tk // ATTN_KEY_SUB):
                off = pl.multiple_of(c * tk + sb * ATTN_KEY_SUB, ATTN_KEY_SUB)
                part, d = sub_block(q, k_ref[pl.ds(off, ATTN_KEY_SUB), :],
                                    v_ref[pl.ds(off, ATTN_KEY_SUB), :])
                lsum = part if lsum is None else lsum + part
                acc = d if acc is None else acc + d
            l_ref[mp] += lsum
            acc_ref[mp] += acc
        return carry

    lax.fori_loop(0, n_keys // tk, body, 0)

    for mp, (q_ref, kc_ref) in enumerate(((q1_ref, k1c_ref), (q2_ref, k2c_ref))):
        part, d = sub_block(q_ref[...], kc_ref[...], vc_ref[...])
        l_ref[mp] += part
        acc_ref[mp] += d

    l1 = jnp.sum(l_ref[0], axis=-1, keepdims=True)
    l2 = jnp.sum(l_ref[1], axis=-1, keepdims=True)
    _attn_finalize(acc_ref[0], l1, acc_ref[1], l2, lam_ref, sw_ref, o_ref, lambda_init)


def _attn_call(qkvg_x, qkvg_c, lam_v, subln_w, lambda_init, *, bounded):
    b, seq, _ = qkvg_x.shape
    n_ctx = qkvg_c.shape[1]
    tq, tk = (ATTN_TQ, ATTN_TK) if bounded else (ATTN_TQ_ONLINE, ATTN_TK_ONLINE)
    assert seq % tq == 0 and seq % tk == 0 and tk % ATTN_KEY_SUB == 0 and n_ctx == ATTN_KEY_SUB
    d = DIFF_HEAD_DIM
    k_off = DIFF_WIDTH // d
    v_off = 2 * DIFF_WIDTH // DIFF_VAL_DIM
    in_specs = [
        pl.BlockSpec((None, tq, d), lambda bi, h, i: (bi, i, 2 * h)),
        pl.BlockSpec((None, tq, d), lambda bi, h, i: (bi, i, 2 * h + 1)),
        pl.BlockSpec((None, seq, d), lambda bi, h, i: (bi, 0, k_off + 2 * h)),
        pl.BlockSpec((None, seq, d), lambda bi, h, i: (bi, 0, k_off + 2 * h + 1)),
        pl.BlockSpec((None, seq, DIFF_VAL_DIM), lambda bi, h, i: (bi, 0, v_off + h)),
        pl.BlockSpec((None, n_ctx, d), lambda bi, h, i: (bi, 0, k_off + 2 * h)),
        pl.BlockSpec((None, n_ctx, d), lambda bi, h, i: (bi, 0, k_off + 2 * h + 1)),
        pl.BlockSpec((None, n_ctx, DIFF_VAL_DIM), lambda bi, h, i: (bi, 0, v_off + h)),
        pl.BlockSpec(lam_v.shape, lambda bi, h, i: (0, 0)),
        pl.BlockSpec((1, DIFF_VAL_DIM), lambda bi, h, i: (0, 0)),
    ]
    if bounded:
        body = functools.partial(_attn_bounded_kernel, tk=tk, lambda_init=lambda_init)
        scratch = [pltpu.VMEM((2, tq, V7X_LANES), F32), pltpu.VMEM((2, tq, DIFF_VAL_DIM), F32)]
    else:
        body = functools.partial(_attn_online_kernel, tk=tk, lambda_init=lambda_init)
        scratch = [pltpu.VMEM((2, tq, 1), F32), pltpu.VMEM((2, tq, 1), F32),
                   pltpu.VMEM((2, tq, DIFF_VAL_DIM), F32)]
    return pl.pallas_call(
        body,
        grid=(b, DIFF_HEADS, seq // tq),
        in_specs=in_specs,
        out_specs=pl.BlockSpec((None, tq, DIFF_VAL_DIM), lambda bi, h, i: (bi, i, h)),
        out_shape=jax.ShapeDtypeStruct((b, seq, DIFF_WIDTH), BF16),
        scratch_shapes=scratch,
        compiler_params=_params(("arbitrary", "arbitrary", "arbitrary")),
        name="diff_attn_bounded" if bounded else "diff_attn_online",
    )(qkvg_x, qkvg_x, qkvg_x, qkvg_x, qkvg_x, qkvg_c, qkvg_c, qkvg_c,
      lam_v, subln_w.reshape(1, DIFF_VAL_DIM))


def _axial_rope_tables(n_tokens):
    rows_n = n_tokens // GRID_W
    row = jnp.repeat(jnp.arange(rows_n), GRID_W).astype(F32)
    col = jnp.tile(jnp.arange(GRID_W), rows_n).astype(F32)
    half = DIFF_HEAD_DIM // 2
    inv_freq = ROPE_BASE ** (-jnp.arange(0, half, 2, dtype=F32) / half)
    ang_r = row[:, None] * inv_freq
    ang_c = col[:, None] * inv_freq
    cos = jnp.cos(jnp.concatenate([ang_r, ang_c, ang_r, ang_c], axis=-1))
    sin_r, sin_c = jnp.sin(ang_r), jnp.sin(ang_c)
    return cos, jnp.concatenate([-sin_r, -sin_c, sin_r, sin_c], axis=-1)


def kernel(x, c, ctx, c_ctx, norm_w, ada_w, ada_b, gla_w_in, gla_gate_a1, gla_gate_a2, gla_gate_b,
           gla_gn_w, gla_w_out, diff_w_in, diff_qn_w, diff_kn_w, diff_lam, diff_subln_w, diff_w_out):
    b, seq, d = x.shape
    ctx_len = ctx.shape[1]
    assert ctx_len == GLA_BLOCK and seq % GLA_BLOCK == 0 and d == D_MODEL

    cond = jnp.zeros((ADA_ROWS, d), F32).at[:b].set(c).at[b].set(c_ctx)
    mod = _ada_call(cond, ada_w, ada_b)

    def mod_rows(layer, part):
        m = mod[layer, :, part * d:(part + 1) * d]
        lat = m[:b, None, :]
        cx = jnp.broadcast_to(m[b][None, None, :], (b, 1, d))
        return lat, cx

    shift, shift_c = mod_rows(0, 0)
    scale, scale_c = mod_rows(0, 1)
    gate, gate_c = mod_rows(0, 2)
    nw0 = norm_w[0].reshape(1, d)
    w_in0 = gla_w_in[0].astype(BF16)
    a1 = jnp.zeros((d, V7X_LANES), F32)
    a1 = a1.at[:, :GLA_GATE_RANK].set(gla_gate_a1[0, 0]).at[:, GLA_GATE_RANK:2 * GLA_GATE_RANK].set(
        gla_gate_a1[0, 1]).astype(BF16)
    a2 = jnp.zeros((V7X_LANES, 2 * GLA_KEY_DIM), F32)
    a2 = a2.at[:GLA_GATE_RANK, :GLA_KEY_DIM].set(gla_gate_a2[0, 0])
    a2 = a2.at[GLA_GATE_RANK:2 * GLA_GATE_RANK, GLA_KEY_DIM:].set(gla_gate_a2[0, 1]).astype(BF16)
    gb = gla_gate_b[0].reshape(1, 2 * GLA_KEY_DIM)
    gla_extras = [a1, a2, gb]
    gla_extra_specs = [
        pl.BlockSpec(a1.shape, lambda bi, i, j: (0, 0)),
        pl.BlockSpec(a2.shape, lambda bi, i, j: (0, 0)),
        pl.BlockSpec(gb.shape, lambda bi, i, j: (0, 0)),
    ]
    la_out = (2 * GLA_KEY_DIM, BF16)
    tm0, tn0 = GLA_INPROJ_TILE
    qkvg0_x, la0_x = _inproj_call(
        _gla_inproj_kernel, x, shift, scale, nw0, w_in0, gla_extras, gla_extra_specs,
        tm=tm0, tn=tn0, extra_out=la_out, name="gla_inproj_x")
    qkvg0_c, la0_c = _inproj_call(
        _gla_inproj_kernel, ctx, shift_c, scale_c, nw0, w_in0, gla_extras, gla_extra_specs,
        tm=CTX_INPROJ_TM, tn=tn0, extra_out=la_out, name="gla_inproj_ctx")

    o_f, o_b, oc_f, oc_b = _gla_scan_call(qkvg0_x, la0_x, qkvg0_c, la0_c)

    w_out0 = gla_w_out[0].astype(BF16)
    gn_w = gla_gn_w[0].reshape(1, GLA_DV)
    g_col_blk0 = (2 * GLA_KEY_DIM + GLA_VAL_DIM) // GLA_VAL_DIM
    x1 = _finish_call(x, gate, [o_f, o_b], qkvg0_x, g_col_blk0, w_out0, gn_w=gn_w,
                      name="gla_finish_x")
    ctx1 = _finish_call(ctx, gate_c, [oc_f, oc_b], qkvg0_c, g_col_blk0, w_out0, gn_w=gn_w,
                        name="gla_finish_ctx")

    lambda_init = 0.8 - 0.6 * math.exp(-0.3 * 1)
    shift, shift_c = mod_rows(1, 0)
    scale, scale_c = mod_rows(1, 1)
    gate, _ = mod_rows(1, 2)
    nw1 = norm_w[1].reshape(1, d)
    w1 = diff_w_in[0]
    w_in1 = jnp.concatenate([_pair_layout(w1[:, :2 * DIFF_WIDTH]), w1[:, 2 * DIFF_WIDTH:]],
                            axis=1).astype(BF16)
    cos, sin = _axial_rope_tables(seq)
    q_scale = math.log2(math.e) * DIFF_HEAD_DIM ** -0.5
    tm1, tn1 = DIFF_INPROJ_TILE
    diff_extras = [_pair_layout(diff_qn_w[0]), _pair_layout(diff_kn_w[0])]
    nrm_specs = [
        pl.BlockSpec((2, DIFF_HEAD_DIM), lambda bi, i, j: (0, 0)),
        pl.BlockSpec((2, DIFF_HEAD_DIM), lambda bi, i, j: (0, 0)),
    ]
    (qkvg1_x,) = _inproj_call(
        functools.partial(_diff_inproj_kernel, rope=True, tn=tn1, q_scale=q_scale),
        x1, shift, scale, nw1, w_in1, diff_extras + [cos, sin],
        nrm_specs + [pl.BlockSpec((tm1, DIFF_HEAD_DIM), lambda bi, i, j: (i, 0)),
                     pl.BlockSpec((tm1, DIFF_HEAD_DIM), lambda bi, i, j: (i, 0))],
        tm=tm1, tn=tn1, name="diff_inproj_x")
    (qkvg1_c,) = _inproj_call(
        functools.partial(_diff_inproj_kernel, rope=False, tn=tn1, q_scale=q_scale),
        ctx1, shift_c, scale_c, nw1, w_in1, diff_extras + [cos, sin],
        nrm_specs + [pl.BlockSpec((CTX_INPROJ_TM, DIFF_HEAD_DIM), lambda bi, i, j: (0, 0)),
                     pl.BlockSpec((CTX_INPROJ_TM, DIFF_HEAD_DIM), lambda bi, i, j: (0, 0))],
        tm=CTX_INPROJ_TM, tn=tn1, name="diff_inproj_ctx")

    score_bound = q_scale * DIFF_HEAD_DIM * jnp.max(
        jnp.max(jnp.abs(diff_qn_w[0]), axis=-1) * jnp.max(jnp.abs(diff_kn_w[0]), axis=-1))
    attn = functools.partial(_attn_call, qkvg1_x, qkvg1_c, diff_lam[0], diff_subln_w[0], lambda_init)
    o_attn = lax.cond(score_bound <= ATTN_SCORE_BOUND,
                      functools.partial(attn, bounded=True), functools.partial(attn, bounded=False))

    w_out1 = diff_w_out[0].astype(BF16)
    g_col_blk1 = 3 * DIFF_WIDTH // DIFF_WIDTH
    return _finish_call(x1, gate, [o_attn], qkvg1_x, g_col_blk1, w_out1, name="diff_finish_x")
```

```python
import functools
import math

import jax
import jax.numpy as jnp
import numpy as np
from jax import lax
from jax.experimental import pallas as pl
from jax.experimental.pallas import tpu as pltpu

F32 = jnp.float32
BF16 = jnp.bfloat16

D_MODEL = 2048
DEPTH = 2
GRID_W = 64
EPS = 1e-6
GLA_HEADS = 4
GLA_KEY_DIM = D_MODEL // 2
GLA_VAL_DIM = D_MODEL
GLA_DK = GLA_KEY_DIM // GLA_HEADS
GLA_DV = GLA_VAL_DIM // GLA_HEADS
GLA_GATE_RANK = 16
GLA_GATE_NORM = 16.0
GLA_CHUNK = 64
GLA_IN_DIM = 2 * GLA_KEY_DIM + 2 * GLA_VAL_DIM
DIFF_HEAD_DIM = 128
DIFF_HEADS = D_MODEL // (2 * DIFF_HEAD_DIM)
DIFF_VAL_DIM = 2 * DIFF_HEAD_DIM
DIFF_WIDTH = DIFF_HEADS * DIFF_VAL_DIM
DIFF_IN_DIM = 4 * DIFF_WIDTH
ROPE_BASE = 10000.0

V7X_LANES = 128
V7X_SUBLANES = 8
V7X_MXU_DIM = 256
VMEM_LIMIT = 56 * 1024 * 1024

ADA_ROWS = V7X_SUBLANES
ADA_TN = 1024
GLA_BLOCK = 256
GLA_INPROJ_TILE = (512, 2048)
DIFF_INPROJ_TILE = (1024, 1024)
CTX_INPROJ_TM = 256
FINISH_TM = 512
ATTN_TQ = 1024
ATTN_TK = 8192
ATTN_TQ_ONLINE = 512
ATTN_TK_ONLINE = 1024
ATTN_KEY_SUB = V7X_MXU_DIM
ATTN_SCORE_BOUND = 60.0


def _dot(a, b):
    return jnp.dot(a, b, preferred_element_type=F32)


def _dot_nt(a, b):
    return lax.dot_general(a, b, (((1,), (1,)), ((), ())), preferred_element_type=F32)


def _params(sem):
    return pltpu.CompilerParams(dimension_semantics=sem, vmem_limit_bytes=VMEM_LIMIT)


def _ada_kernel(c_ref, w_ref, b_ref, o_ref):
    c = c_ref[...]
    s = c * jax.nn.sigmoid(c)
    o_ref[...] = jnp.dot(s, w_ref[...], preferred_element_type=F32,
                         precision=lax.Precision.HIGHEST) + b_ref[...]


def _ada_call(cond, ada_w, ada_b):
    tn = ADA_TN
    n = ada_w.shape[-1]
    return pl.pallas_call(
        _ada_kernel,
        grid=(DEPTH, n // tn),
        in_specs=[
            pl.BlockSpec((ADA_ROWS, D_MODEL), lambda l, j: (0, 0)),
            pl.BlockSpec((None, D_MODEL, tn), lambda l, j: (l, 0, j)),
            pl.BlockSpec((None, 1, tn), lambda l, j: (l, 0, j)),
        ],
        out_specs=pl.BlockSpec((None, ADA_ROWS, tn), lambda l, j: (l, 0, j)),
        out_shape=jax.ShapeDtypeStruct((DEPTH, ADA_ROWS, n), F32),
        compiler_params=_params(("arbitrary", "arbitrary")),
        name="ada_mod",
    )(cond, ada_w, ada_b.reshape(DEPTH, 1, n))


def _normalize_into(x_ref, shift_ref, scale_ref, nw_ref, hn_ref):
    x = x_ref[...]
    ms = jnp.mean(x * x, axis=-1, keepdims=True)
    a = nw_ref[...] * (1.0 + scale_ref[...])
    hn_ref[...] = (x * lax.rsqrt(ms + EPS) * a + shift_ref[...]).astype(BF16)


def _gla_inproj_kernel(x_ref, shift_ref, scale_ref, nw_ref, w_ref, a1_ref, a2_ref, gb_ref,
                       out_ref, la_ref, hn_ref):
    j = pl.program_id(2)

    @pl.when(j == 0)
    def _():
        _normalize_into(x_ref, shift_ref, scale_ref, nw_ref, hn_ref)
        z = _dot(hn_ref[...], a1_ref[...])
        lg = _dot(z.astype(BF16), a2_ref[...]) + gb_ref[...]
        log_sig = jnp.minimum(lg, 0.0) - jnp.log(1.0 + jnp.exp(-jnp.abs(lg)))
        la_ref[...] = (log_sig * (1.0 / GLA_GATE_NORM)).astype(la_ref.dtype)

    out_ref[...] = _dot(hn_ref[...], w_ref[...]).astype(BF16)


def _pair_layout(a):
    qd = DIFF_HEAD_DIM // 4
    heads = a.reshape(*a.shape[:-1], a.shape[-1] // DIFF_HEAD_DIM, DIFF_HEAD_DIM)
    parts = [heads[..., i * qd:(i + 1) * qd] for i in (0, 2, 1, 3)]
    return jnp.concatenate(parts, axis=-1).reshape(a.shape)


def _rope_rotate(t, cos, sin_signed):
    return t * cos + pltpu.roll(t, DIFF_HEAD_DIM // 2, 1) * sin_signed


def _diff_inproj_kernel(x_ref, shift_ref, scale_ref, nw_ref, w_ref, qn_ref, kn_ref,
                        cos_ref, sin_ref, ones_ref, out_ref, hn_ref, *, rope, tn, q_scale):
    j = pl.program_id(2)
    tiles_per_part = DIFF_WIDTH // tn

    @pl.when(j == 0)
    def _():
        _normalize_into(x_ref, shift_ref, scale_ref, nw_ref, hn_ref)

    acc = _dot(hn_ref[...], w_ref[...])

    def head_norm(nrm_ref, scale):
        if rope:
            cos = cos_ref[...]
            sin_signed = sin_ref[...]
        for head in range(tn // DIFF_VAL_DIM):
            t2 = acc[:, head * DIFF_VAL_DIM:(head + 1) * DIFF_VAL_DIM]
            ss = _dot((t2 * t2).astype(BF16), ones_ref[...])
            inv = lax.rsqrt(ss * (1.0 / DIFF_HEAD_DIM) + EPS)
            for mp in range(2):
                lanes = slice(mp * DIFF_HEAD_DIM, (mp + 1) * DIFF_HEAD_DIM)
                t = t2[:, lanes] * inv[:, lanes] * nrm_ref[mp:mp + 1, :]
                if rope:
                    t = _rope_rotate(t, cos, sin_signed)
                if scale != 1.0:
                    t = t * scale
                col = head * DIFF_VAL_DIM + mp * DIFF_HEAD_DIM
                out_ref[:, col:col + DIFF_HEAD_DIM] = t.astype(BF16)

    @pl.when(j < tiles_per_part)
    def _():
        head_norm(qn_ref, q_scale)

    @pl.when((j >= tiles_per_part) & (j < 2 * tiles_per_part))
    def _():
        head_norm(kn_ref, 1.0)

    @pl.when(j >= 2 * tiles_per_part)
    def _():
        out_ref[...] = acc.astype(BF16)


def _inproj_call(kernel, x, shift, scale, nw, w, extras, extra_specs, *, tm, tn,
                 extra_out=None, name):
    b, t, d = x.shape
    n = w.shape[-1]
    assert t % tm == 0 and n % tn == 0
    in_specs = [
        pl.BlockSpec((None, tm, d), lambda bi, i, j: (bi, i, 0)),
        pl.BlockSpec((None, 1, d), lambda bi, i, j: (bi, 0, 0)),
        pl.BlockSpec((None, 1, d), lambda bi, i, j: (bi, 0, 0)),
        pl.BlockSpec((1, d), lambda bi, i, j: (0, 0)),
        pl.BlockSpec((d, tn), lambda bi, i, j: (0, j)),
    ] + extra_specs
    out_specs = [pl.BlockSpec((None, tm, tn), lambda bi, i, j: (bi, i, j))]
    out_shape = [jax.ShapeDtypeStruct((b, t, n), BF16)]
    if extra_out is not None:
        width, dtype = extra_out
        out_specs.append(pl.BlockSpec((None, tm, width), lambda bi, i, j: (bi, i, 0)))
        out_shape.append(jax.ShapeDtypeStruct((b, t, width), dtype))
    return pl.pallas_call(
        kernel,
        grid=(b, t // tm, n // tn),
        in_specs=in_specs,
        out_specs=out_specs,
        out_shape=out_shape,
        scratch_shapes=[pltpu.VMEM((tm, d), BF16)],
        compiler_params=_params(("arbitrary", "arbitrary", "arbitrary")),
        name=name,
    )(x, shift, scale, nw, w, *extras)


def _gla_tri_matrices():
    idx = jnp.arange(GLA_BLOCK)
    i, j = idx[:, None], idx[None, :]
    same = (i // GLA_CHUNK) == (j // GLA_CHUNK)
    lower, upper = j <= i, j >= i
    return jnp.stack([jnp.stack([lower, lower & same]),
                      jnp.stack([upper, upper & same])]).astype(BF16)


def _dot_tn(a, b):
    return lax.dot_general(a, b, (((0,), (0,)), ((), ())), preferred_element_type=F32)


def _gla_direction(q_ref, k_ref, v_ref, la_ref, tri_ref, st_ref, o_ref, direction):
    fwd = direction == 0
    cs = GLA_CHUNK
    n_sub = GLA_BLOCK // cs
    b = _dot(tri_ref[direction, 0], la_ref[...])
    tot = b[GLA_BLOCK - 1:GLA_BLOCK] if fwd else b[0:1]

    def per_chunk_rows(row_of_chunk):
        return jnp.concatenate(
            [jnp.broadcast_to(b[row_of_chunk(c):row_of_chunk(c) + 1], (cs, GLA_DK))
             for c in range(n_sub)], axis=0)

    if fwd:
        mid = per_chunk_rows(lambda c: c * cs + cs // 2 - 1)
        edge = per_chunk_rows(lambda c: max(c * cs - 1, 0))
    else:
        mid = per_chunk_rows(lambda c: c * cs + cs // 2)
        edge = per_chunk_rows(lambda c: min((c + 1) * cs, GLA_BLOCK - 1))

    q = q_ref[...].astype(F32) * (GLA_DK ** -0.5)
    k = k_ref[...].astype(F32)
    v = v_ref[...]
    st = st_ref[...]
    q_in = (q * jnp.exp(b)).astype(BF16)
    k_in = (k * jnp.exp(tot - b)).astype(BF16)
    q_d = (q * jnp.exp(b - mid)).astype(BF16)
    k_d = (k * jnp.exp(mid - b)).astype(BF16)
    q_o = (q * jnp.exp(b - edge)).astype(BF16)

    scores = jnp.where(tri_ref[direction, 1] > 0, _dot_nt(q_d, k_d), 0.0)
    tok = lax.broadcasted_iota(jnp.int32, (GLA_BLOCK, 1), 0)
    off = []
    for c in range(n_sub):
        lo, hi = c * cs, (c + 1) * cs
        first, last, ref_row = (0, lo, lo - 1) if fwd else (hi, GLA_BLOCK, hi)
        if last > first:
            expo = jnp.where((tok >= first) & (tok < last), b[ref_row:ref_row + 1] - b, -1e30)
            k_o = (k * jnp.exp(expo)).astype(BF16)
            off.append(_dot_nt(q_o[lo:hi], k_o))
        else:
            off.append(jnp.zeros((cs, GLA_BLOCK), F32))
    scores = scores + jnp.concatenate(off, axis=0)

    o = _dot_nt(q_in, st.astype(BF16)) + _dot(scores.astype(BF16), v)
    o_ref[...] = o.astype(BF16)
    st_ref[...] = st * jnp.exp(tot) + _dot_tn(v, k_in)


def _gla_scan_kernel(qf_ref, kf_ref, vf_ref, laf_ref, qb_ref, kb_ref, vb_ref, lab_ref,
                     qc_ref, kc_ref, vc_ref, lacf_ref, lacb_ref, tri_ref,
                     of_ref, ob_ref, ocf_ref, ocb_ref, sf_ref, sb_ref):
    t = pl.program_id(2)

    @pl.when(t == 0)
    def _():
        sf_ref[...] = jnp.zeros_like(sf_ref)
        sb_ref[...] = jnp.zeros_like(sb_ref)
        _gla_direction(qc_ref, kc_ref, vc_ref, lacf_ref, tri_ref, sf_ref, ocf_ref, 0)
        _gla_direction(qc_ref, kc_ref, vc_ref, lacb_ref, tri_ref, sb_ref, ocb_ref, 1)

    @pl.when(t > 0)
    def _():
        _gla_direction(qf_ref, kf_ref, vf_ref, laf_ref, tri_ref, sf_ref, of_ref, 0)
        _gla_direction(qb_ref, kb_ref, vb_ref, lab_ref, tri_ref, sb_ref, ob_ref, 1)


def _gla_scan_call(qkvg_x, la_x, qkvg_c, la_c):
    b, seq, _ = qkvg_x.shape
    nb = seq // GLA_BLOCK
    assert qkvg_c.shape[1] == GLA_BLOCK
    k_off = GLA_KEY_DIM // GLA_DK
    v_off = 2 * GLA_KEY_DIM // GLA_DV

    def pos_f(t):
        return jnp.maximum(t - 1, 0)

    def pos_b(t):
        return nb - jnp.maximum(t, 1)

    def pos_c(t):
        return 0

    def specs(pos, la_col_off, with_la=True):
        out = [
            pl.BlockSpec((None, GLA_BLOCK, GLA_DK), lambda bi, h, t: (bi, pos(t), h)),
            pl.BlockSpec((None, GLA_BLOCK, GLA_DK), lambda bi, h, t: (bi, pos(t), k_off + h)),
            pl.BlockSpec((None, GLA_BLOCK, GLA_DV), lambda bi, h, t: (bi, pos(t), v_off + h)),
        ]
        if with_la:
            out.append(pl.BlockSpec((None, GLA_BLOCK, GLA_DK),
                                    lambda bi, h, t: (bi, pos(t), la_col_off + h)))
        return out

    def la_spec(pos, la_col_off):
        return pl.BlockSpec((None, GLA_BLOCK, GLA_DK), lambda bi, h, t: (bi, pos(t), la_col_off + h))

    def o_spec(pos):
        return pl.BlockSpec((None, GLA_BLOCK, GLA_DV), lambda bi, h, t: (bi, pos(t), h))

    tri = _gla_tri_matrices()
    in_specs = (specs(pos_f, 0) + specs(pos_b, GLA_HEADS) + specs(pos_c, 0, with_la=False)
                + [la_spec(pos_c, 0), la_spec(pos_c, GLA_HEADS),
                   pl.BlockSpec(tri.shape, lambda bi, h, t: (0, 0, 0, 0))])
    ox_shape = jax.ShapeDtypeStruct((b, seq, GLA_VAL_DIM), BF16)
    oc_shape = jax.ShapeDtypeStruct((b, GLA_BLOCK, GLA_VAL_DIM), BF16)
    return pl.pallas_call(
        _gla_scan_kernel,
        grid=(b, GLA_HEADS, nb + 1),
        in_specs=in_specs,
        out_specs=[o_spec(pos_f), o_spec(pos_b), o_spec(pos_c), o_spec(pos_c)],
        out_shape=[ox_shape, ox_shape, oc_shape, oc_shape],
        scratch_shapes=[pltpu.VMEM((GLA_DV, GLA_DK), F32), pltpu.VMEM((GLA_DV, GLA_DK), F32)],
        compiler_params=_params(("arbitrary", "arbitrary", "arbitrary")),
        name="gla_scan",
    )(qkvg_x, qkvg_x, qkvg_x, la_x, qkvg_x, qkvg_x, qkvg_x, la_x,
      qkvg_c, qkvg_c, qkvg_c, la_c, la_c, tri)


def _finish_kernel(*refs, head_norm):
    if head_norm:
        x_ref, gate_ref, o1_ref, o2_ref, g_ref, gn_ref, w_ref, out_ref = refs
        o = o1_ref[...].astype(F32) + o2_ref[...].astype(F32)
        parts = []
        for h in range(GLA_HEADS):
            oh = o[:, h * GLA_DV:(h + 1) * GLA_DV]
            ms = jnp.mean(oh * oh, axis=-1, keepdims=True)
            parts.append(oh * lax.rsqrt(ms + EPS) * gn_ref[...])
        o = jnp.concatenate(parts, axis=-1)
    else:
        x_ref, gate_ref, o1_ref, g_ref, w_ref, out_ref = refs
        o = o1_ref[...].astype(F32)
    g = g_ref[...].astype(F32)
    y_in = (o * (g * jax.nn.sigmoid(g))).astype(BF16)
    y = _dot(y_in, w_ref[...])
    out_ref[...] = x_ref[...] + gate_ref[...] * y


def _finish_call(x, gate, o_list, g_arr, g_col_blk, w, *, gn_w=None, name):
    b, t, d = x.shape
    tm = min(FINISH_TM, t)
    assert t % tm == 0
    width = w.shape[0]
    in_specs = [
        pl.BlockSpec((None, tm, d), lambda bi, i: (bi, i, 0)),
        pl.BlockSpec((None, 1, d), lambda bi, i: (bi, 0, 0)),
    ]
    args = [x, gate]
    for o in o_list:
        in_specs.append(pl.BlockSpec((None, tm, width), lambda bi, i: (bi, i, 0)))
        args.append(o)
    in_specs.append(pl.BlockSpec((None, tm, width), lambda bi, i: (bi, i, g_col_blk)))
    args.append(g_arr)
    if gn_w is not None:
        in_specs.append(pl.BlockSpec((1, gn_w.shape[-1]), lambda bi, i: (0, 0)))
        args.append(gn_w)
    in_specs.append(pl.BlockSpec((width, d), lambda bi, i: (0, 0)))
    args.append(w)
    return pl.pallas_call(
        functools.partial(_finish_kernel, head_norm=gn_w is not None),
        grid=(b, t // tm),
        in_specs=in_specs,
        out_specs=pl.BlockSpec((None, tm, d), lambda bi, i: (bi, i, 0)),
        out_shape=jax.ShapeDtypeStruct((b, t, d), F32),
        compiler_params=_params(("arbitrary", "arbitrary")),
        name=name,
    )(*args)


def _attn_finalize(acc1, l1, acc2, l2, lam_ref, sw_ref, o_ref, lambda_init):
    lv = lam_ref[...]
    lam = (jnp.exp(jnp.sum(lv[0:1] * lv[1:2], axis=-1, keepdims=True))
           - jnp.exp(jnp.sum(lv[2:3] * lv[3:4], axis=-1, keepdims=True)) + lambda_init)
    o = acc1 / l1 - lam * (acc2 / l2)
    ms = jnp.mean(o * o, axis=-1, keepdims=True)
    o = o * lax.rsqrt(ms + EPS) * sw_ref[...] * (1.0 - lambda_init)
    o_ref[...] = o.astype(BF16)


def _attn_online_kernel(q1_ref, q2_ref, k1_ref, k2_ref, v_ref, k1c_ref, k2c_ref, vc_ref,
                        lam_ref, sw_ref, o_ref, m_ref, l_ref, acc_ref, *, tk, lambda_init):
    n_keys = v_ref.shape[0]
    m_ref[...] = jnp.full_like(m_ref, -1e30)
    l_ref[...] = jnp.zeros_like(l_ref)
    acc_ref[...] = jnp.zeros_like(acc_ref)

    def update(mp, q_ref, k_blk, v_blk):
        s = _dot_nt(q_ref[...], k_blk)
        m_old = m_ref[mp]
        m_new = jnp.maximum(m_old, jnp.max(s, axis=1, keepdims=True))
        alpha = jnp.exp2(m_old - m_new)
        p = jnp.exp2(s - m_new)
        l_ref[mp] = alpha * l_ref[mp] + jnp.sum(p, axis=1, keepdims=True)
        acc_ref[mp] = alpha * acc_ref[mp] + _dot(p.astype(BF16), v_blk)
        m_ref[mp] = m_new

    def body(c, carry):
        off = pl.multiple_of(c * tk, tk)
        v_blk = v_ref[pl.ds(off, tk), :]
        update(0, q1_ref, k1_ref[pl.ds(off, tk), :], v_blk)
        update(1, q2_ref, k2_ref[pl.ds(off, tk), :], v_blk)
        return carry

    lax.fori_loop(0, n_keys // tk, body, 0)
    update(0, q1_ref, k1c_ref[...], vc_ref[...])
    update(1, q2_ref, k2c_ref[...], vc_ref[...])
    _attn_finalize(acc_ref[0], l_ref[0], acc_ref[1], l_ref[1], lam_ref, sw_ref, o_ref, lambda_init)


def _attn_bounded_kernel(q1_ref, q2_ref, k1_ref, k2_ref, v_ref, k1c_ref, k2c_ref, vc_ref,
                         lam_ref, sw_ref, o_ref, l_ref, acc_ref, *, tk, lambda_init):
    n_keys = v_ref.shape[0]
    l_ref[...] = jnp.zeros_like(l_ref)
    acc_ref[...] = jnp.zeros_like(acc_ref)

    def sub_block(q, k_blk, v_blk):
        s = _dot_nt(q, k_blk)
        p = jnp.exp2(s)
        part = p[:, :V7X_LANES] + p[:, V7X_LANES:]
        return part, _dot(p.astype(BF16), v_blk)

    def body(c, carry):
        for mp, (q_ref, k_ref) in enumerate(((q1_ref, k1_ref), (q2_ref, k2_ref))):
            q = q_ref[...]
            acc = None
            lsum = None
            for sb in range(tk // ATTN_KEY_SUB):
                off = pl.multiple_of(c * tk + sb * ATTN_KEY_SUB, ATTN_KEY_SUB)
                part, d = sub_block(q, k_ref[pl.ds(off, ATTN_KEY_SUB), :],
                                    v_ref[pl.ds(off, ATTN_KEY_SUB), :])
                lsum = part if lsum is None else lsum + part
                acc = d if acc is None else acc + d
            l_ref[mp] += lsum
            acc_ref[mp] += acc
        return carry

    lax.fori_loop(0, n_keys // tk, body, 0)

    for mp, (q_ref, kc_ref) in enumerate(((q1_ref, k1c_ref), (q2_ref, k2c_ref))):
        part, d = sub_block(q_ref[...], kc_ref[...], vc_ref[...])
        l_ref[mp] += part
        acc_ref[mp] += d

    l1 = jnp.sum(l_ref[0], axis=-1, keepdims=True)
    l2 = jnp.sum(l_ref[1], axis=-1, keepdims=True)
    _attn_finalize(acc_ref[0], l1, acc_ref[1], l2, lam_ref, sw_ref, o_ref, lambda_init)


def _attn_call(qkvg_x, qkvg_c, lam_v, subln_w, lambda_init, *, bounded):
    b, seq, _ = qkvg_x.shape
    n_ctx = qkvg_c.shape[1]
    tq, tk = (ATTN_TQ, ATTN_TK) if bounded else (ATTN_TQ_ONLINE, ATTN_TK_ONLINE)
    assert seq % tq == 0 and seq % tk == 0 and tk % ATTN_KEY_SUB == 0 and n_ctx == ATTN_KEY_SUB
    d = DIFF_HEAD_DIM
    k_off = DIFF_WIDTH // d
    v_off = 2 * DIFF_WIDTH // DIFF_VAL_DIM
    in_specs = [
        pl.BlockSpec((None, tq, d), lambda bi, h, i: (bi, i, 2 * h)),
        pl.BlockSpec((None, tq, d), lambda bi, h, i: (bi, i, 2 * h + 1)),
        pl.BlockSpec((None, seq, d), lambda bi, h, i: (bi, 0, k_off + 2 * h)),
        pl.BlockSpec((None, seq, d), lambda bi, h, i: (bi, 0, k_off + 2 * h + 1)),
        pl.BlockSpec((None, seq, DIFF_VAL_DIM), lambda bi, h, i: (bi, 0, v_off + h)),
        pl.BlockSpec((None, n_ctx, d), lambda bi, h, i: (bi, 0, k_off + 2 * h)),
        pl.BlockSpec((None, n_ctx, d), lambda bi, h, i: (bi, 0, k_off + 2 * h + 1)),
        pl.BlockSpec((None, n_ctx, DIFF_VAL_DIM), lambda bi, h, i: (bi, 0, v_off + h)),
        pl.BlockSpec(lam_v.shape, lambda bi, h, i: (0, 0)),
        pl.BlockSpec((1, DIFF_VAL_DIM), lambda bi, h, i: (0, 0)),
    ]
    if bounded:
        body = functools.partial(_attn_bounded_kernel, tk=tk, lambda_init=lambda_init)
        scratch = [pltpu.VMEM((2, tq, V7X_LANES), F32), pltpu.VMEM((2, tq, DIFF_VAL_DIM), F32)]
    else:
        body = functools.partial(_attn_online_kernel, tk=tk, lambda_init=lambda_init)
        scratch = [pltpu.VMEM((2, tq, 1), F32), pltpu.VMEM((2, tq, 1), F32),
                   pltpu.VMEM((2, tq, DIFF_VAL_DIM), F32)]
    return pl.pallas_call(
        body,
        grid=(b, DIFF_HEADS, seq // tq),
        in_specs=in_specs,
        out_specs=pl.BlockSpec((None, tq, DIFF_VAL_DIM), lambda bi, h, i: (bi, i, h)),
        out_shape=jax.ShapeDtypeStruct((b, seq, DIFF_WIDTH), BF16),
        scratch_shapes=scratch,
        compiler_params=_params(("arbitrary", "arbitrary", "arbitrary")),
        name="diff_attn_bounded" if bounded else "diff_attn_online",
    )(qkvg_x, qkvg_x, qkvg_x, qkvg_x, qkvg_x, qkvg_c, qkvg_c, qkvg_c,
      lam_v, subln_w.reshape(1, DIFF_VAL_DIM))


def _axial_rope_tables(n_tokens):
    rows_n = n_tokens // GRID_W
    half = DIFF_HEAD_DIM // 2
    inv_freq = (ROPE_BASE ** (-np.arange(0, half, 2, dtype=np.float32) / half)).astype(np.float32)
    ang_r = np.arange(rows_n, dtype=np.float32)[:, None] * inv_freq
    ang_c = np.arange(GRID_W, dtype=np.float32)[:, None] * inv_freq

    def expand(tab_r, tab_c):
        r = jnp.repeat(jnp.asarray(tab_r, F32), GRID_W, axis=0)
        c = jnp.tile(jnp.asarray(tab_c, F32), (rows_n, 1))
        return r, c

    cos_r, cos_c = expand(np.cos(ang_r), np.cos(ang_c))
    sin_r, sin_c = expand(np.sin(ang_r), np.sin(ang_c))
    return (jnp.concatenate([cos_r, cos_c, cos_r, cos_c], axis=-1),
            jnp.concatenate([-sin_r, -sin_c, sin_r, sin_c], axis=-1))


def kernel(x, c, ctx, c_ctx, norm_w, ada_w, ada_b, gla_w_in, gla_gate_a1, gla_gate_a2, gla_gate_b,
           gla_gn_w, gla_w_out, diff_w_in, diff_qn_w, diff_kn_w, diff_lam, diff_subln_w, diff_w_out):
    b, seq, d = x.shape
    ctx_len = ctx.shape[1]
    assert ctx_len == GLA_BLOCK and seq % GLA_BLOCK == 0 and d == D_MODEL

    cond = jnp.zeros((ADA_ROWS, d), F32).at[:b].set(c).at[b].set(c_ctx)
    mod = _ada_call(cond, ada_w, ada_b)

    def mod_rows(layer, part):
        m = mod[layer, :, part * d:(part + 1) * d]
        lat = m[:b, None, :]
        cx = jnp.broadcast_to(m[b][None, None, :], (b, 1, d))
        return lat, cx

    shift, shift_c = mod_rows(0, 0)
    scale, scale_c = mod_rows(0, 1)
    gate, gate_c = mod_rows(0, 2)
    nw0 = norm_w[0].reshape(1, d)
    w_in0 = gla_w_in[0].astype(BF16)
    a1 = jnp.zeros((d, V7X_LANES), F32)
    a1 = a1.at[:, :GLA_GATE_RANK].set(gla_gate_a1[0, 0]).at[:, GLA_GATE_RANK:2 * GLA_GATE_RANK].set(
        gla_gate_a1[0, 1]).astype(BF16)
    a2 = jnp.zeros((V7X_LANES, 2 * GLA_KEY_DIM), F32)
    a2 = a2.at[:GLA_GATE_RANK, :GLA_KEY_DIM].set(gla_gate_a2[0, 0])
    a2 = a2.at[GLA_GATE_RANK:2 * GLA_GATE_RANK, GLA_KEY_DIM:].set(gla_gate_a2[0, 1]).astype(BF16)
    gb = gla_gate_b[0].reshape(1, 2 * GLA_KEY_DIM)
    gla_extras = [a1, a2, gb]
    gla_extra_specs = [
        pl.BlockSpec(a1.shape, lambda bi, i, j: (0, 0)),
        pl.BlockSpec(a2.shape, lambda bi, i, j: (0, 0)),
        pl.BlockSpec(gb.shape, lambda bi, i, j: (0, 0)),
    ]
    la_out = (2 * GLA_KEY_DIM, BF16)
    tm0, tn0 = GLA_INPROJ_TILE
    qkvg0_x, la0_x = _inproj_call(
        _gla_inproj_kernel, x, shift, scale, nw0, w_in0, gla_extras, gla_extra_specs,
        tm=tm0, tn=tn0, extra_out=la_out, name="gla_inproj_x")
    qkvg0_c, la0_c = _inproj_call(
        _gla_inproj_kernel, ctx, shift_c, scale_c, nw0, w_in0, gla_extras, gla_extra_specs,
        tm=CTX_INPROJ_TM, tn=tn0, extra_out=la_out, name="gla_inproj_ctx")

    o_f, o_b, oc_f, oc_b = _gla_scan_call(qkvg0_x, la0_x, qkvg0_c, la0_c)

    w_out0 = gla_w_out[0].astype(BF16)
    gn_w = gla_gn_w[0].reshape(1, GLA_DV)
    g_col_blk0 = (2 * GLA_KEY_DIM + GLA_VAL_DIM) // GLA_VAL_DIM
    x1 = _finish_call(x, gate, [o_f, o_b], qkvg0_x, g_col_blk0, w_out0, gn_w=gn_w,
                      name="gla_finish_x")
    ctx1 = _finish_call(ctx, gate_c, [oc_f, oc_b], qkvg0_c, g_col_blk0, w_out0, gn_w=gn_w,
                        name="gla_finish_ctx")

    lambda_init = 0.8 - 0.6 * math.exp(-0.3 * 1)
    shift, shift_c = mod_rows(1, 0)
    scale, scale_c = mod_rows(1, 1)
    gate, _ = mod_rows(1, 2)
    nw1 = norm_w[1].reshape(1, d)
    w1 = diff_w_in[0]
    w_in1 = jnp.concatenate([_pair_layout(w1[:, :2 * DIFF_WIDTH]), w1[:, 2 * DIFF_WIDTH:]],
                            axis=1).astype(BF16)
    cos, sin = _axial_rope_tables(seq)
    q_scale = math.log2(math.e) * DIFF_HEAD_DIM ** -0.5
    tm1, tn1 = DIFF_INPROJ_TILE
    map_ones = jnp.kron(jnp.eye(2, dtype=F32), jnp.ones((DIFF_HEAD_DIM, DIFF_HEAD_DIM), F32)).astype(BF16)
    nrm_args = [_pair_layout(diff_qn_w[0]), _pair_layout(diff_kn_w[0])]
    nrm_specs = [
        pl.BlockSpec((2, DIFF_HEAD_DIM), lambda bi, i, j: (0, 0)),
        pl.BlockSpec((2, DIFF_HEAD_DIM), lambda bi, i, j: (0, 0)),
    ]
    ones_spec = pl.BlockSpec(map_ones.shape, lambda bi, i, j: (0, 0))
    (qkvg1_x,) = _inproj_call(
        functools.partial(_diff_inproj_kernel, rope=True, tn=tn1, q_scale=q_scale),
        x1, shift, scale, nw1, w_in1, nrm_args + [cos, sin, map_ones],
        nrm_specs + [pl.BlockSpec((tm1, DIFF_HEAD_DIM), lambda bi, i, j: (i, 0)),
                     pl.BlockSpec((tm1, DIFF_HEAD_DIM), lambda bi, i, j: (i, 0)), ones_spec],
        tm=tm1, tn=tn1, name="diff_inproj_x")
    (qkvg1_c,) = _inproj_call(
        functools.partial(_diff_inproj_kernel, rope=False, tn=tn1, q_scale=q_scale),
        ctx1, shift_c, scale_c, nw1, w_in1, nrm_args + [cos, sin, map_ones],
        nrm_specs + [pl.BlockSpec((CTX_INPROJ_TM, DIFF_HEAD_DIM), lambda bi, i, j: (0, 0)),
                     pl.BlockSpec((CTX_INPROJ_TM, DIFF_HEAD_DIM), lambda bi, i, j: (0, 0)), ones_spec],
        tm=CTX_INPROJ_TM, tn=tn1, name="diff_inproj_ctx")

    score_bound = q_scale * DIFF_HEAD_DIM * jnp.max(
        jnp.max(jnp.abs(diff_qn_w[0]), axis=-1) * jnp.max(jnp.abs(diff_kn_w[0]), axis=-1))
    attn = functools.partial(_attn_call, qkvg1_x, qkvg1_c, diff_lam[0], diff_subln_w[0], lambda_init)
    o_attn = lax.cond(score_bound <= ATTN_SCORE_BOUND,
                      functools.partial(attn, bounded=True), functools.partial(attn, bounded=False))

    w_out1 = diff_w_out[0].astype(BF16)
    g_col_blk1 = 3 * DIFF_WIDTH // DIFF_WIDTH
    return _finish_call(x1, gate, [o_attn], qkvg1_x, g_col_blk1, w_out1, name="diff_finish_x")
```

```python
import functools
import math

import jax
import jax.numpy as jnp
import numpy as np
from jax import lax
from jax.experimental import pallas as pl
from jax.experimental.pallas import tpu as pltpu

F32 = jnp.float32
BF16 = jnp.bfloat16

D_MODEL = 2048
DEPTH = 2
GRID_W = 64
EPS = 1e-6
GLA_HEADS = 4
GLA_KEY_DIM = D_MODEL // 2
GLA_VAL_DIM = D_MODEL
GLA_DK = GLA_KEY_DIM // GLA_HEADS
GLA_DV = GLA_VAL_DIM // GLA_HEADS
GLA_GATE_RANK = 16
GLA_GATE_NORM = 16.0
GLA_CHUNK = 64
GLA_IN_DIM = 2 * GLA_KEY_DIM + 2 * GLA_VAL_DIM
DIFF_HEAD_DIM = 128
DIFF_HEADS = D_MODEL // (2 * DIFF_HEAD_DIM)
DIFF_VAL_DIM = 2 * DIFF_HEAD_DIM
DIFF_WIDTH = DIFF_HEADS * DIFF_VAL_DIM
DIFF_IN_DIM = 4 * DIFF_WIDTH
ROPE_BASE = 10000.0

V7X_LANES = 128
V7X_SUBLANES = 8
V7X_MXU_DIM = 256
VMEM_LIMIT = 56 * 1024 * 1024

ADA_ROWS = V7X_SUBLANES
ADA_TN = 1024
GLA_BLOCK = 256
GLA_INPROJ_TILE = (512, 2048)
DIFF_INPROJ_TILE = (1024, 1024)
CTX_INPROJ_TM = 256
FINISH_TM = 512
ATTN_TQ = 1024
ATTN_TK = 8192
ATTN_TQ_ONLINE = 512
ATTN_TK_ONLINE = 1024
ATTN_KEY_SUB = V7X_MXU_DIM
ATTN_SCORE_BOUND = 60.0


def _dot(a, b):
    return jnp.dot(a, b, preferred_element_type=F32)


def _dot_nt(a, b):
    return lax.dot_general(a, b, (((1,), (1,)), ((), ())), preferred_element_type=F32)


def _params(sem):
    return pltpu.CompilerParams(dimension_semantics=sem, vmem_limit_bytes=VMEM_LIMIT)


def _ada_kernel(c_ref, w_ref, b_ref, o_ref):
    c = c_ref[...]
    s = c * jax.nn.sigmoid(c)
    o_ref[...] = jnp.dot(s, w_ref[...], preferred_element_type=F32,
                         precision=lax.Precision.HIGHEST) + b_ref[...]


def _ada_call(cond, ada_w, ada_b):
    tn = ADA_TN
    n = ada_w.shape[-1]
    return pl.pallas_call(
        _ada_kernel,
        grid=(DEPTH, n // tn),
        in_specs=[
            pl.BlockSpec((ADA_ROWS, D_MODEL), lambda l, j: (0, 0)),
            pl.BlockSpec((None, D_MODEL, tn), lambda l, j: (l, 0, j)),
            pl.BlockSpec((None, 1, tn), lambda l, j: (l, 0, j)),
        ],
        out_specs=pl.BlockSpec((None, ADA_ROWS, tn), lambda l, j: (l, 0, j)),
        out_shape=jax.ShapeDtypeStruct((DEPTH, ADA_ROWS, n), F32),
        compiler_params=_params(("arbitrary", "arbitrary")),
        name="ada_mod",
    )(cond, ada_w, ada_b.reshape(DEPTH, 1, n))


def _normalize_into(x_ref, shift_ref, scale_ref, nw_ref, hn_ref):
    x = x_ref[...]
    ms = jnp.mean(x * x, axis=-1, keepdims=True)
    a = nw_ref[...] * (1.0 + scale_ref[...])
    hn_ref[...] = (x * lax.rsqrt(ms + EPS) * a + shift_ref[...]).astype(BF16)


def _gla_inproj_kernel(x_ref, shift_ref, scale_ref, nw_ref, w_ref, a1_ref, a2_ref, gb_ref,
                       out_ref, la_ref, hn_ref):
    j = pl.program_id(2)

    @pl.when(j == 0)
    def _():
        _normalize_into(x_ref, shift_ref, scale_ref, nw_ref, hn_ref)
        z = _dot(hn_ref[...], a1_ref[...])
        lg = _dot(z.astype(BF16), a2_ref[...]) + gb_ref[...]
        log_sig = jnp.minimum(lg, 0.0) - jnp.log(1.0 + jnp.exp(-jnp.abs(lg)))
        la_ref[...] = (log_sig * (1.0 / GLA_GATE_NORM)).astype(la_ref.dtype)

    out_ref[...] = _dot(hn_ref[...], w_ref[...]).astype(BF16)


def _head_matrices():
    qd = DIFF_HEAD_DIM // 4
    e = np.arange(qd)
    rot = np.zeros((DIFF_HEAD_DIM, DIFF_HEAD_DIM), np.float32)
    rot[qd + e, e] = -1.0
    rot[e, qd + e] = 1.0
    rot[3 * qd + e, 2 * qd + e] = -1.0
    rot[2 * qd + e, 3 * qd + e] = 1.0
    both_maps = np.eye(2, dtype=np.float32)
    mean = np.kron(both_maps, np.full((DIFF_HEAD_DIM, DIFF_HEAD_DIM), 1.0 / DIFF_HEAD_DIM, np.float32))
    return jnp.asarray(np.stack([mean, np.kron(both_maps, rot)]), BF16)


def _diff_inproj_kernel(x_ref, shift_ref, scale_ref, nw_ref, w_ref, qn_ref, kn_ref,
                        cos_ref, sin_ref, hm_ref, out_ref, hn_ref, *, rope, tn, q_scale):
    j = pl.program_id(2)
    tiles_per_part = DIFF_WIDTH // tn

    @pl.when(j == 0)
    def _():
        _normalize_into(x_ref, shift_ref, scale_ref, nw_ref, hn_ref)

    acc = _dot(hn_ref[...], w_ref[...])

    def head_norm(nrm_ref, scale):
        w2 = nrm_ref[...] * scale
        if rope:
            cos2 = jnp.concatenate([cos_ref[...], cos_ref[...]], axis=1)
            sin2 = jnp.concatenate([sin_ref[...], sin_ref[...]], axis=1)
        for head in range(tn // DIFF_VAL_DIM):
            cols = slice(head * DIFF_VAL_DIM, (head + 1) * DIFF_VAL_DIM)
            t2 = acc[:, cols]
            ms = _dot((t2 * t2).astype(BF16), hm_ref[0])
            u = t2 * w2
            if rope:
                u = u * cos2 + _dot(u.astype(BF16), hm_ref[1]) * sin2
            out_ref[:, cols] = (u * lax.rsqrt(ms + EPS)).astype(BF16)

    @pl.when(j < tiles_per_part)
    def _():
        head_norm(qn_ref, q_scale)

    @pl.when((j >= tiles_per_part) & (j < 2 * tiles_per_part))
    def _():
        head_norm(kn_ref, 1.0)

    @pl.when(j >= 2 * tiles_per_part)
    def _():
        out_ref[...] = acc.astype(BF16)


def _inproj_call(kernel, x, shift, scale, nw, w, extras, extra_specs, *, tm, tn,
                 extra_out=None, name):
    b, t, d = x.shape
    n = w.shape[-1]
    assert t % tm == 0 and n % tn == 0
    in_specs = [
        pl.BlockSpec((None, tm, d), lambda bi, i, j: (bi, i, 0)),
        pl.BlockSpec((None, 1, d), lambda bi, i, j: (bi, 0, 0)),
        pl.BlockSpec((None, 1, d), lambda bi, i, j: (bi, 0, 0)),
        pl.BlockSpec((1, d), lambda bi, i, j: (0, 0)),
        pl.BlockSpec((d, tn), lambda bi, i, j: (0, j)),
    ] + extra_specs
    out_specs = [pl.BlockSpec((None, tm, tn), lambda bi, i, j: (bi, i, j))]
    out_shape = [jax.ShapeDtypeStruct((b, t, n), BF16)]
    if extra_out is not None:
        width, dtype = extra_out
        out_specs.append(pl.BlockSpec((None, tm, width), lambda bi, i, j: (bi, i, 0)))
        out_shape.append(jax.ShapeDtypeStruct((b, t, width), dtype))
    return pl.pallas_call(
        kernel,
        grid=(b, t // tm, n // tn),
        in_specs=in_specs,
        out_specs=out_specs,
        out_shape=out_shape,
        scratch_shapes=[pltpu.VMEM((tm, d), BF16)],
        compiler_params=_params(("arbitrary", "arbitrary", "arbitrary")),
        name=name,
    )(x, shift, scale, nw, w, *extras)


def _gla_tri_matrices():
    idx = jnp.arange(GLA_BLOCK)
    i, j = idx[:, None], idx[None, :]
    same = (i // GLA_CHUNK) == (j // GLA_CHUNK)
    lower, upper = j <= i, j >= i
    return jnp.stack([jnp.stack([lower, lower & same]),
                      jnp.stack([upper, upper & same])]).astype(BF16)


def _dot_tn(a, b):
    return lax.dot_general(a, b, (((0,), (0,)), ((), ())), preferred_element_type=F32)


def _gla_direction(q_ref, k_ref, v_ref, la_ref, tri_ref, st_ref, o_ref, direction):
    fwd = direction == 0
    cs = GLA_CHUNK
    n_sub = GLA_BLOCK // cs
    b = _dot(tri_ref[direction, 0], la_ref[...])
    tot = b[GLA_BLOCK - 1:GLA_BLOCK] if fwd else b[0:1]

    def per_chunk_rows(row_of_chunk):
        return jnp.concatenate(
            [jnp.broadcast_to(b[row_of_chunk(c):row_of_chunk(c) + 1], (cs, GLA_DK))
             for c in range(n_sub)], axis=0)

    if fwd:
        mid = per_chunk_rows(lambda c: c * cs + cs // 2 - 1)
        edge = per_chunk_rows(lambda c: max(c * cs - 1, 0))
    else:
        mid = per_chunk_rows(lambda c: c * cs + cs // 2)
        edge = per_chunk_rows(lambda c: min((c + 1) * cs, GLA_BLOCK - 1))

    q = q_ref[...].astype(F32) * (GLA_DK ** -0.5)
    k = k_ref[...].astype(F32)
    v = v_ref[...]
    st = st_ref[...]
    q_in = (q * jnp.exp(b)).astype(BF16)
    k_in = (k * jnp.exp(tot - b)).astype(BF16)
    q_d = (q * jnp.exp(b - mid)).astype(BF16)
    k_d = (k * jnp.exp(mid - b)).astype(BF16)
    q_o = (q * jnp.exp(b - edge)).astype(BF16)

    scores = jnp.where(tri_ref[direction, 1] > 0, _dot_nt(q_d, k_d), 0.0)
    tok = lax.broadcasted_iota(jnp.int32, (GLA_BLOCK, 1), 0)
    off = []
    for c in range(n_sub):
        lo, hi = c * cs, (c + 1) * cs
        first, last, ref_row = (0, lo, lo - 1) if fwd else (hi, GLA_BLOCK, hi)
        if last > first:
            expo = jnp.where((tok >= first) & (tok < last), b[ref_row:ref_row + 1] - b, -1e30)
            k_o = (k * jnp.exp(expo)).astype(BF16)
            off.append(_dot_nt(q_o[lo:hi], k_o))
        else:
            off.append(jnp.zeros((cs, GLA_BLOCK), F32))
    scores = scores + jnp.concatenate(off, axis=0)

    o = _dot_nt(q_in, st.astype(BF16)) + _dot(scores.astype(BF16), v)
    o_ref[...] = o.astype(BF16)
    st_ref[...] = st * jnp.exp(tot) + _dot_tn(v, k_in)


def _gla_scan_kernel(qf_ref, kf_ref, vf_ref, laf_ref, qb_ref, kb_ref, vb_ref, lab_ref,
                     qc_ref, kc_ref, vc_ref, lacf_ref, lacb_ref, tri_ref,
                     of_ref, ob_ref, ocf_ref, ocb_ref, sf_ref, sb_ref):
    t = pl.program_id(2)

    @pl.when(t == 0)
    def _():
        sf_ref[...] = jnp.zeros_like(sf_ref)
        sb_ref[...] = jnp.zeros_like(sb_ref)
        _gla_direction(qc_ref, kc_ref, vc_ref, lacf_ref, tri_ref, sf_ref, ocf_ref, 0)
        _gla_direction(qc_ref, kc_ref, vc_ref, lacb_ref, tri_ref, sb_ref, ocb_ref, 1)

    @pl.when(t > 0)
    def _():
        _gla_direction(qf_ref, kf_ref, vf_ref, laf_ref, tri_ref, sf_ref, of_ref, 0)
        _gla_direction(qb_ref, kb_ref, vb_ref, lab_ref, tri_ref, sb_ref, ob_ref, 1)


def _gla_scan_call(qkvg_x, la_x, qkvg_c, la_c):
    b, seq, _ = qkvg_x.shape
    nb = seq // GLA_BLOCK
    assert qkvg_c.shape[1] == GLA_BLOCK
    k_off = GLA_KEY_DIM // GLA_DK
    v_off = 2 * GLA_KEY_DIM // GLA_DV

    def pos_f(t):
        return jnp.maximum(t - 1, 0)

    def pos_b(t):
        return nb - jnp.maximum(t, 1)

    def pos_c(t):
        return 0

    def specs(pos, la_col_off, with_la=True):
        out = [
            pl.BlockSpec((None, GLA_BLOCK, GLA_DK), lambda bi, h, t: (bi, pos(t), h)),
            pl.BlockSpec((None, GLA_BLOCK, GLA_DK), lambda bi, h, t: (bi, pos(t), k_off + h)),
            pl.BlockSpec((None, GLA_BLOCK, GLA_DV), lambda bi, h, t: (bi, pos(t), v_off + h)),
        ]
        if with_la:
            out.append(pl.BlockSpec((None, GLA_BLOCK, GLA_DK),
                                    lambda bi, h, t: (bi, pos(t), la_col_off + h)))
        return out

    def la_spec(pos, la_col_off):
        return pl.BlockSpec((None, GLA_BLOCK, GLA_DK), lambda bi, h, t: (bi, pos(t), la_col_off + h))

    def o_spec(pos):
        return pl.BlockSpec((None, GLA_BLOCK, GLA_DV), lambda bi, h, t: (bi, pos(t), h))

    tri = _gla_tri_matrices()
    in_specs = (specs(pos_f, 0) + specs(pos_b, GLA_HEADS) + specs(pos_c, 0, with_la=False)
                + [la_spec(pos_c, 0), la_spec(pos_c, GLA_HEADS),
                   pl.BlockSpec(tri.shape, lambda bi, h, t: (0, 0, 0, 0))])
    ox_shape = jax.ShapeDtypeStruct((b, seq, GLA_VAL_DIM), BF16)
    oc_shape = jax.ShapeDtypeStruct((b, GLA_BLOCK, GLA_VAL_DIM), BF16)
    return pl.pallas_call(
        _gla_scan_kernel,
        grid=(b, GLA_HEADS, nb + 1),
        in_specs=in_specs,
        out_specs=[o_spec(pos_f), o_spec(pos_b), o_spec(pos_c), o_spec(pos_c)],
        out_shape=[ox_shape, ox_shape, oc_shape, oc_shape],
        scratch_shapes=[pltpu.VMEM((GLA_DV, GLA_DK), F32), pltpu.VMEM((GLA_DV, GLA_DK), F32)],
        compiler_params=_params(("arbitrary", "arbitrary", "arbitrary")),
        name="gla_scan",
    )(qkvg_x, qkvg_x, qkvg_x, la_x, qkvg_x, qkvg_x, qkvg_x, la_x,
      qkvg_c, qkvg_c, qkvg_c, la_c, la_c, tri)


def _finish_kernel(*refs, head_norm):
    if head_norm:
        x_ref, gate_ref, o1_ref, o2_ref, g_ref, gn_ref, w_ref, out_ref = refs
        o = o1_ref[...].astype(F32) + o2_ref[...].astype(F32)
        parts = []
        for h in range(GLA_HEADS):
            oh = o[:, h * GLA_DV:(h + 1) * GLA_DV]
            ms = jnp.mean(oh * oh, axis=-1, keepdims=True)
            parts.append(oh * lax.rsqrt(ms + EPS) * gn_ref[...])
        o = jnp.concatenate(parts, axis=-1)
    else:
        x_ref, gate_ref, o1_ref, g_ref, w_ref, out_ref = refs
        o = o1_ref[...].astype(F32)
    g = g_ref[...].astype(F32)
    y_in = (o * (g * jax.nn.sigmoid(g))).astype(BF16)
    y = _dot(y_in, w_ref[...])
    out_ref[...] = x_ref[...] + gate_ref[...] * y


def _finish_call(x, gate, o_list, g_arr, g_col_blk, w, *, gn_w=None, name):
    b, t, d = x.shape
    tm = min(FINISH_TM, t)
    assert t % tm == 0
    width = w.shape[0]
    in_specs = [
        pl.BlockSpec((None, tm, d), lambda bi, i: (bi, i, 0)),
        pl.BlockSpec((None, 1, d), lambda bi, i: (bi, 0, 0)),
    ]
    args = [x, gate]
    for o in o_list:
        in_specs.append(pl.BlockSpec((None, tm, width), lambda bi, i: (bi, i, 0)))
        args.append(o)
    in_specs.append(pl.BlockSpec((None, tm, width), lambda bi, i: (bi, i, g_col_blk)))
    args.append(g_arr)
    if gn_w is not None:
        in_specs.append(pl.BlockSpec((1, gn_w.shape[-1]), lambda bi, i: (0, 0)))
        args.append(gn_w)
    in_specs.append(pl.BlockSpec((width, d), lambda bi, i: (0, 0)))
    args.append(w)
    return pl.pallas_call(
        functools.partial(_finish_kernel, head_norm=gn_w is not None),
        grid=(b, t // tm),
        in_specs=in_specs,
        out_specs=pl.BlockSpec((None, tm, d), lambda bi, i: (bi, i, 0)),
        out_shape=jax.ShapeDtypeStruct((b, t, d), F32),
        compiler_params=_params(("arbitrary", "arbitrary")),
        name=name,
    )(*args)


def _attn_finalize(acc1, l1, acc2, l2, lam_ref, sw_ref, o_ref, lambda_init):
    lv = lam_ref[...]
    lam = (jnp.exp(jnp.sum(lv[0:1] * lv[1:2], axis=-1, keepdims=True))
           - jnp.exp(jnp.sum(lv[2:3] * lv[3:4], axis=-1, keepdims=True)) + lambda_init)
    o = acc1 / l1 - lam * (acc2 / l2)
    ms = jnp.mean(o * o, axis=-1, keepdims=True)
    o = o * lax.rsqrt(ms + EPS) * sw_ref[...] * (1.0 - lambda_init)
    o_ref[...] = o.astype(BF16)


def _attn_online_kernel(q1_ref, q2_ref, k1_ref, k2_ref, v_ref, k1c_ref, k2c_ref, vc_ref,
                        lam_ref, sw_ref, o_ref, m_ref, l_ref, acc_ref, *, tk, lambda_init):
    n_keys = v_ref.shape[0]
    m_ref[...] = jnp.full_like(m_ref, -1e30)
    l_ref[...] = jnp.zeros_like(l_ref)
    acc_ref[...] = jnp.zeros_like(acc_ref)

    def update(mp, q_ref, k_blk, v_blk):
        s = _dot_nt(q_ref[...], k_blk)
        m_old = m_ref[mp]
        m_new = jnp.maximum(m_old, jnp.max(s, axis=1, keepdims=True))
        alpha = jnp.exp2(m_old - m_new)
        p = jnp.exp2(s - m_new)
        l_ref[mp] = alpha * l_ref[mp] + jnp.sum(p, axis=1, keepdims=True)
        acc_ref[mp] = alpha * acc_ref[mp] + _dot(p.astype(BF16), v_blk)
        m_ref[mp] = m_new

    def body(c, carry):
        off = pl.multiple_of(c * tk, tk)
        v_blk = v_ref[pl.ds(off, tk), :]
        update(0, q1_ref, k1_ref[pl.ds(off, tk), :], v_blk)
        update(1, q2_ref, k2_ref[pl.ds(off, tk), :], v_blk)
        return carry

    lax.fori_loop(0, n_keys // tk, body, 0)
    update(0, q1_ref, k1c_ref[...], vc_ref[...])
    update(1, q2_ref, k2c_ref[...], vc_ref[...])
    _attn_finalize(acc_ref[0], l_ref[0], acc_ref[1], l_ref[1], lam_ref, sw_ref, o_ref, lambda_init)


def _attn_bounded_kernel(q1_ref, q2_ref, k1_ref, k2_ref, v_ref, k1c_ref, k2c_ref, vc_ref,
                         lam_ref, sw_ref, o_ref, l_ref, acc_ref, *, tk, lambda_init):
    n_keys = v_ref.shape[0]
    l_ref[...] = jnp.zeros_like(l_ref)
    acc_ref[...] = jnp.zeros_like(acc_ref)

    def sub_block(q, k_blk, v_blk):
        s = _dot_nt(q, k_blk)
        p = jnp.exp2(s)
        part = p[:, :V7X_LANES] + p[:, V7X_LANES:]
        return part, _dot(p.astype(BF16), v_blk)

    def body(c, carry):
        for mp, (q_ref, k_ref) in enumerate(((q1_ref, k1_ref), (q2_ref, k2_ref))):
            q = q_ref[...]
            acc = None
            lsum = None
            for sb in range(tk // ATTN_KEY_SUB):
                off = pl.multiple_of(c * tk + sb * ATTN_KEY_SUB, ATTN_KEY_SUB)
                part, d = sub_block(q, k_ref[pl.ds(off, ATTN_KEY_SUB), :],
                                    v_ref[pl.ds(off, ATTN_KEY_SUB), :])
                lsum = part if lsum is None else lsum + part
                acc = d if acc is None else acc + d
            l_ref[mp] += lsum
            acc_ref[mp] += acc
        return carry

    lax.fori_loop(0, n_keys // tk, body, 0)

    for mp, (q_ref, kc_ref) in enumerate(((q1_ref, k1c_ref), (q2_ref, k2c_ref))):
        part, d = sub_block(q_ref[...], kc_ref[...], vc_ref[...])
        l_ref[mp] += part
        acc_ref[mp] += d

    l1 = jnp.sum(l_ref[0], axis=-1, keepdims=True)
    l2 = jnp.sum(l_ref[1], axis=-1, keepdims=True)
    _attn_finalize(acc_ref[0], l1, acc_ref[1], l2, lam_ref, sw_ref, o_ref, lambda_init)


def _attn_call(qkvg_x, qkvg_c, lam_v, subln_w, lambda_init, *, bounded):
    b, seq, _ = qkvg_x.shape
    n_ctx = qkvg_c.shape[1]
    tq, tk = (ATTN_TQ, ATTN_TK) if bounded else (ATTN_TQ_ONLINE, ATTN_TK_ONLINE)
    assert seq % tq == 0 and seq % tk == 0 and tk % ATTN_KEY_SUB == 0 and n_ctx == ATTN_KEY_SUB
    d = DIFF_HEAD_DIM
    k_off = DIFF_WIDTH // d
    v_off = 2 * DIFF_WIDTH // DIFF_VAL_DIM
    in_specs = [
        pl.BlockSpec((None, tq, d), lambda bi, h, i: (bi, i, 2 * h)),
        pl.BlockSpec((None, tq, d), lambda bi, h, i: (bi, i, 2 * h + 1)),
        pl.BlockSpec((None, seq, d), lambda bi, h, i: (bi, 0, k_off + 2 * h)),
        pl.BlockSpec((None, seq, d), lambda bi, h, i: (bi, 0, k_off + 2 * h + 1)),
        pl.BlockSpec((None, seq, DIFF_VAL_DIM), lambda bi, h, i: (bi, 0, v_off + h)),
        pl.BlockSpec((None, n_ctx, d), lambda bi, h, i: (bi, 0, k_off + 2 * h)),
        pl.BlockSpec((None, n_ctx, d), lambda bi, h, i: (bi, 0, k_off + 2 * h + 1)),
        pl.BlockSpec((None, n_ctx, DIFF_VAL_DIM), lambda bi, h, i: (bi, 0, v_off + h)),
        pl.BlockSpec(lam_v.shape, lambda bi, h, i: (0, 0)),
        pl.BlockSpec((1, DIFF_VAL_DIM), lambda bi, h, i: (0, 0)),
    ]
    if bounded:
        body = functools.partial(_attn_bounded_kernel, tk=tk, lambda_init=lambda_init)
        scratch = [pltpu.VMEM((2, tq, V7X_LANES), F32), pltpu.VMEM((2, tq, DIFF_VAL_DIM), F32)]
    else:
        body = functools.partial(_attn_online_kernel, tk=tk, lambda_init=lambda_init)
        scratch = [pltpu.VMEM((2, tq, 1), F32), pltpu.VMEM((2, tq, 1), F32),
                   pltpu.VMEM((2, tq, DIFF_VAL_DIM), F32)]
    return pl.pallas_call(
        body,
        grid=(b, DIFF_HEADS, seq // tq),
        in_specs=in_specs,
        out_specs=pl.BlockSpec((None, tq, DIFF_VAL_DIM), lambda bi, h, i: (bi, i, h)),
        out_shape=jax.ShapeDtypeStruct((b, seq, DIFF_WIDTH), BF16),
        scratch_shapes=scratch,
        compiler_params=_params(("arbitrary", "arbitrary", "arbitrary")),
        name="diff_attn_bounded" if bounded else "diff_attn_online",
    )(qkvg_x, qkvg_x, qkvg_x, qkvg_x, qkvg_x, qkvg_c, qkvg_c, qkvg_c,
      lam_v, subln_w.reshape(1, DIFF_VAL_DIM))


def _axial_rope_tables(n_tokens):
    rows_n = n_tokens // GRID_W
    half = DIFF_HEAD_DIM // 2
    inv_freq = (ROPE_BASE ** (-np.arange(0, half, 2, dtype=np.float32) / half)).astype(np.float32)
    ang_r = np.arange(rows_n, dtype=np.float32)[:, None] * inv_freq
    ang_c = np.arange(GRID_W, dtype=np.float32)[:, None] * inv_freq

    def expand(tab_r, tab_c):
        r = jnp.repeat(jnp.asarray(tab_r, F32), GRID_W, axis=0)
        c = jnp.tile(jnp.asarray(tab_c, F32), (rows_n, 1))
        return r, c

    cos_r, cos_c = expand(np.cos(ang_r), np.cos(ang_c))
    sin_r, sin_c = expand(np.sin(ang_r), np.sin(ang_c))
    return (jnp.concatenate([cos_r, cos_r, cos_c, cos_c], axis=-1),
            jnp.concatenate([sin_r, sin_r, sin_c, sin_c], axis=-1))


def kernel(x, c, ctx, c_ctx, norm_w, ada_w, ada_b, gla_w_in, gla_gate_a1, gla_gate_a2, gla_gate_b,
           gla_gn_w, gla_w_out, diff_w_in, diff_qn_w, diff_kn_w, diff_lam, diff_subln_w, diff_w_out):
    b, seq, d = x.shape
    ctx_len = ctx.shape[1]
    assert ctx_len == GLA_BLOCK and seq % GLA_BLOCK == 0 and d == D_MODEL

    cond = jnp.zeros((ADA_ROWS, d), F32).at[:b].set(c).at[b].set(c_ctx)
    mod = _ada_call(cond, ada_w, ada_b)

    def mod_rows(layer, part):
        m = mod[layer, :, part * d:(part + 1) * d]
        lat = m[:b, None, :]
        cx = jnp.broadcast_to(m[b][None, None, :], (b, 1, d))
        return lat, cx

    shift, shift_c = mod_rows(0, 0)
    scale, scale_c = mod_rows(0, 1)
    gate, gate_c = mod_rows(0, 2)
    nw0 = norm_w[0].reshape(1, d)
    w_in0 = gla_w_in[0].astype(BF16)
    a1 = jnp.zeros((d, V7X_LANES), F32)
    a1 = a1.at[:, :GLA_GATE_RANK].set(gla_gate_a1[0, 0]).at[:, GLA_GATE_RANK:2 * GLA_GATE_RANK].set(
        gla_gate_a1[0, 1]).astype(BF16)
    a2 = jnp.zeros((V7X_LANES, 2 * GLA_KEY_DIM), F32)
    a2 = a2.at[:GLA_GATE_RANK, :GLA_KEY_DIM].set(gla_gate_a2[0, 0])
    a2 = a2.at[GLA_GATE_RANK:2 * GLA_GATE_RANK, GLA_KEY_DIM:].set(gla_gate_a2[0, 1]).astype(BF16)
    gb = gla_gate_b[0].reshape(1, 2 * GLA_KEY_DIM)
    gla_extras = [a1, a2, gb]
    gla_extra_specs = [
        pl.BlockSpec(a1.shape, lambda bi, i, j: (0, 0)),
        pl.BlockSpec(a2.shape, lambda bi, i, j: (0, 0)),
        pl.BlockSpec(gb.shape, lambda bi, i, j: (0, 0)),
    ]
    la_out = (2 * GLA_KEY_DIM, BF16)
    tm0, tn0 = GLA_INPROJ_TILE
    qkvg0_x, la0_x = _inproj_call(
        _gla_inproj_kernel, x, shift, scale, nw0, w_in0, gla_extras, gla_extra_specs,
        tm=tm0, tn=tn0, extra_out=la_out, name="gla_inproj_x")
    qkvg0_c, la0_c = _inproj_call(
        _gla_inproj_kernel, ctx, shift_c, scale_c, nw0, w_in0, gla_extras, gla_extra_specs,
        tm=CTX_INPROJ_TM, tn=tn0, extra_out=la_out, name="gla_inproj_ctx")

    o_f, o_b, oc_f, oc_b = _gla_scan_call(qkvg0_x, la0_x, qkvg0_c, la0_c)

    w_out0 = gla_w_out[0].astype(BF16)
    gn_w = gla_gn_w[0].reshape(1, GLA_DV)
    g_col_blk0 = (2 * GLA_KEY_DIM + GLA_VAL_DIM) // GLA_VAL_DIM
    x1 = _finish_call(x, gate, [o_f, o_b], qkvg0_x, g_col_blk0, w_out0, gn_w=gn_w,
                      name="gla_finish_x")
    ctx1 = _finish_call(ctx, gate_c, [oc_f, oc_b], qkvg0_c, g_col_blk0, w_out0, gn_w=gn_w,
                        name="gla_finish_ctx")

    lambda_init = 0.8 - 0.6 * math.exp(-0.3 * 1)
    shift, shift_c = mod_rows(1, 0)
    scale, scale_c = mod_rows(1, 1)
    gate, _ = mod_rows(1, 2)
    nw1 = norm_w[1].reshape(1, d)
    w_in1 = diff_w_in[0].astype(BF16)
    cos, sin = _axial_rope_tables(seq)
    q_scale = math.log2(math.e) * DIFF_HEAD_DIM ** -0.5
    tm1, tn1 = DIFF_INPROJ_TILE
    head_mats = _head_matrices()
    nrm_args = [diff_qn_w[0].reshape(1, DIFF_VAL_DIM), diff_kn_w[0].reshape(1, DIFF_VAL_DIM)]
    nrm_specs = [
        pl.BlockSpec((1, DIFF_VAL_DIM), lambda bi, i, j: (0, 0)),
        pl.BlockSpec((1, DIFF_VAL_DIM), lambda bi, i, j: (0, 0)),
    ]
    hm_spec = pl.BlockSpec(head_mats.shape, lambda bi, i, j: (0, 0, 0))
    (qkvg1_x,) = _inproj_call(
        functools.partial(_diff_inproj_kernel, rope=True, tn=tn1, q_scale=q_scale),
        x1, shift, scale, nw1, w_in1, nrm_args + [cos, sin, head_mats],
        nrm_specs + [pl.BlockSpec((tm1, DIFF_HEAD_DIM), lambda bi, i, j: (i, 0)),
                     pl.BlockSpec((tm1, DIFF_HEAD_DIM), lambda bi, i, j: (i, 0)), hm_spec],
        tm=tm1, tn=tn1, name="diff_inproj_x")
    (qkvg1_c,) = _inproj_call(
        functools.partial(_diff_inproj_kernel, rope=False, tn=tn1, q_scale=q_scale),
        ctx1, shift_c, scale_c, nw1, w_in1, nrm_args + [cos, sin, head_mats],
        nrm_specs + [pl.BlockSpec((CTX_INPROJ_TM, DIFF_HEAD_DIM), lambda bi, i, j: (0, 0)),
                     pl.BlockSpec((CTX_INPROJ_TM, DIFF_HEAD_DIM), lambda bi, i, j: (0, 0)), hm_spec],
        tm=CTX_INPROJ_TM, tn=tn1, name="diff_inproj_ctx")

    score_bound = q_scale * DIFF_HEAD_DIM * jnp.max(
        jnp.max(jnp.abs(diff_qn_w[0]), axis=-1) * jnp.max(jnp.abs(diff_kn_w[0]), axis=-1))
    attn = functools.partial(_attn_call, qkvg1_x, qkvg1_c, diff_lam[0], diff_subln_w[0], lambda_init)
    o_attn = lax.cond(score_bound <= ATTN_SCORE_BOUND,
                      functools.partial(attn, bounded=True), functools.partial(attn, bounded=False))

    w_out1 = diff_w_out[0].astype(BF16)
    g_col_blk1 = 3 * DIFF_WIDTH // DIFF_WIDTH
    return _finish_call(x1, gate, [o_attn], qkvg1_x, g_col_blk1, w_out1, name="diff_finish_x")
```

```python
import functools
import math

import jax
import jax.numpy as jnp
import numpy as np
from jax import lax
from jax.experimental import pallas as pl
from jax.experimental.pallas import tpu as pltpu

F32 = jnp.float32
BF16 = jnp.bfloat16

D_MODEL = 2048
DEPTH = 2
GRID_W = 64
EPS = 1e-6
GLA_HEADS = 4
GLA_KEY_DIM = D_MODEL // 2
GLA_VAL_DIM = D_MODEL
GLA_DK = GLA_KEY_DIM // GLA_HEADS
GLA_DV = GLA_VAL_DIM // GLA_HEADS
GLA_GATE_RANK = 16
GLA_GATE_NORM = 16.0
GLA_CHUNK = 64
GLA_IN_DIM = 2 * GLA_KEY_DIM + 2 * GLA_VAL_DIM
DIFF_HEAD_DIM = 128
DIFF_HEADS = D_MODEL // (2 * DIFF_HEAD_DIM)
DIFF_VAL_DIM = 2 * DIFF_HEAD_DIM
DIFF_WIDTH = DIFF_HEADS * DIFF_VAL_DIM
DIFF_IN_DIM = 4 * DIFF_WIDTH
ROPE_BASE = 10000.0

V7X_LANES = 128
V7X_SUBLANES = 8
V7X_MXU_DIM = 256
VMEM_LIMIT = 56 * 1024 * 1024

ADA_ROWS = V7X_SUBLANES
ADA_TN = 1024
GLA_BLOCK = 256
GLA_STEP_BLOCKS = 2
GLA_INPROJ_TILE = (512, 2048)
DIFF_INPROJ_TILE = (1024, 1024)
CTX_INPROJ_TM = 256
FINISH_TM = 512
ATTN_TQ = 1024
ATTN_TK = 8192
ATTN_TQ_ONLINE = 512
ATTN_TK_ONLINE = 1024
ATTN_KEY_SUB = V7X_MXU_DIM
ATTN_SCORE_BOUND = 60.0


def _dot(a, b):
    return jnp.dot(a, b, preferred_element_type=F32)


def _dot_nt(a, b):
    return lax.dot_general(a, b, (((1,), (1,)), ((), ())), preferred_element_type=F32)


def _params(sem):
    return pltpu.CompilerParams(dimension_semantics=sem, vmem_limit_bytes=VMEM_LIMIT)


def _ada_kernel(c_ref, w_ref, b_ref, o_ref):
    c = c_ref[...]
    s = c * jax.nn.sigmoid(c)
    o_ref[...] = jnp.dot(s, w_ref[...], preferred_element_type=F32,
                         precision=lax.Precision.HIGHEST) + b_ref[...]


def _ada_call(cond, ada_w, ada_b):
    tn = ADA_TN
    n = ada_w.shape[-1]
    return pl.pallas_call(
        _ada_kernel,
        grid=(DEPTH, n // tn),
        in_specs=[
            pl.BlockSpec((ADA_ROWS, D_MODEL), lambda l, j: (0, 0)),
            pl.BlockSpec((None, D_MODEL, tn), lambda l, j: (l, 0, j)),
            pl.BlockSpec((None, 1, tn), lambda l, j: (l, 0, j)),
        ],
        out_specs=pl.BlockSpec((None, ADA_ROWS, tn), lambda l, j: (l, 0, j)),
        out_shape=jax.ShapeDtypeStruct((DEPTH, ADA_ROWS, n), F32),
        compiler_params=_params(("arbitrary", "arbitrary")),
        name="ada_mod",
    )(cond, ada_w, ada_b.reshape(DEPTH, 1, n))


def _normalize_into(x_ref, shift_ref, scale_ref, nw_ref, hn_ref):
    x = x_ref[...]
    ms = jnp.mean(x * x, axis=-1, keepdims=True)
    a = nw_ref[...] * (1.0 + scale_ref[...])
    hn_ref[...] = (x * lax.rsqrt(ms + EPS) * a + shift_ref[...]).astype(BF16)


def _gla_inproj_kernel(x_ref, shift_ref, scale_ref, nw_ref, w_ref, a1_ref, a2_ref, gb_ref,
                       out_ref, la_ref, hn_ref):
    j = pl.program_id(2)

    @pl.when(j == 0)
    def _():
        _normalize_into(x_ref, shift_ref, scale_ref, nw_ref, hn_ref)
        z = _dot(hn_ref[...], a1_ref[...])
        lg = _dot(z.astype(BF16), a2_ref[...]) + gb_ref[...]
        log_sig = jnp.minimum(lg, 0.0) - jnp.log(1.0 + jnp.exp(-jnp.abs(lg)))
        la_ref[...] = (log_sig * (1.0 / GLA_GATE_NORM)).astype(la_ref.dtype)

    out_ref[...] = _dot(hn_ref[...], w_ref[...]).astype(BF16)


def _head_matrices():
    qd = DIFF_HEAD_DIM // 4
    e = np.arange(qd)
    rot = np.zeros((DIFF_HEAD_DIM, DIFF_HEAD_DIM), np.float32)
    rot[qd + e, e] = -1.0
    rot[e, qd + e] = 1.0
    rot[3 * qd + e, 2 * qd + e] = -1.0
    rot[2 * qd + e, 3 * qd + e] = 1.0
    both_maps = np.eye(2, dtype=np.float32)
    mean = np.kron(both_maps, np.full((DIFF_HEAD_DIM, DIFF_HEAD_DIM), 1.0 / DIFF_HEAD_DIM, np.float32))
    return jnp.asarray(np.stack([mean, np.kron(both_maps, rot)]), BF16)


def _diff_inproj_kernel(x_ref, shift_ref, scale_ref, nw_ref, w_ref, qn_ref, kn_ref,
                        cos_ref, sin_ref, hm_ref, out_ref, hn_ref, *, rope, tn, q_scale):
    j = pl.program_id(2)
    tiles_per_part = DIFF_WIDTH // tn

    @pl.when(j == 0)
    def _():
        _normalize_into(x_ref, shift_ref, scale_ref, nw_ref, hn_ref)

    acc = _dot(hn_ref[...], w_ref[...])

    def head_norm(nrm_ref, scale):
        w2 = nrm_ref[...] * scale
        if rope:
            cos2 = jnp.concatenate([cos_ref[...], cos_ref[...]], axis=1)
            sin2 = jnp.concatenate([sin_ref[...], sin_ref[...]], axis=1)
        for head in range(tn // DIFF_VAL_DIM):
            cols = slice(head * DIFF_VAL_DIM, (head + 1) * DIFF_VAL_DIM)
            t2 = acc[:, cols]
            ms = _dot((t2 * t2).astype(BF16), hm_ref[0])
            u = t2 * w2
            if rope:
                u = u * cos2 + _dot(u.astype(BF16), hm_ref[1]) * sin2
            out_ref[:, cols] = (u * lax.rsqrt(ms + EPS)).astype(BF16)

    @pl.when(j < tiles_per_part)
    def _():
        head_norm(qn_ref, q_scale)

    @pl.when((j >= tiles_per_part) & (j < 2 * tiles_per_part))
    def _():
        head_norm(kn_ref, 1.0)

    @pl.when(j >= 2 * tiles_per_part)
    def _():
        out_ref[...] = acc.astype(BF16)


def _inproj_call(kernel, x, shift, scale, nw, w, extras, extra_specs, *, tm, tn,
                 extra_out=None, name):
    b, t, d = x.shape
    n = w.shape[-1]
    assert t % tm == 0 and n % tn == 0
    in_specs = [
        pl.BlockSpec((None, tm, d), lambda bi, i, j: (bi, i, 0)),
        pl.BlockSpec((None, 1, d), lambda bi, i, j: (bi, 0, 0)),
        pl.BlockSpec((None, 1, d), lambda bi, i, j: (bi, 0, 0)),
        pl.BlockSpec((1, d), lambda bi, i, j: (0, 0)),
        pl.BlockSpec((d, tn), lambda bi, i, j: (0, j)),
    ] + extra_specs
    out_specs = [pl.BlockSpec((None, tm, tn), lambda bi, i, j: (bi, i, j))]
    out_shape = [jax.ShapeDtypeStruct((b, t, n), BF16)]
    if extra_out is not None:
        width, dtype = extra_out
        out_specs.append(pl.BlockSpec((None, tm, width), lambda bi, i, j: (bi, i, 0)))
        out_shape.append(jax.ShapeDtypeStruct((b, t, width), dtype))
    return pl.pallas_call(
        kernel,
        grid=(b, t // tm, n // tn),
        in_specs=in_specs,
        out_specs=out_specs,
        out_shape=out_shape,
        scratch_shapes=[pltpu.VMEM((tm, d), BF16)],
        compiler_params=_params(("arbitrary", "arbitrary", "arbitrary")),
        name=name,
    )(x, shift, scale, nw, w, *extras)


def _gla_tri_matrices():
    idx = jnp.arange(GLA_BLOCK)
    i, j = idx[:, None], idx[None, :]
    same = (i // GLA_CHUNK) == (j // GLA_CHUNK)
    lower, upper = j <= i, j >= i
    return jnp.stack([jnp.stack([lower, lower & same]),
                      jnp.stack([upper, upper & same])]).astype(BF16)


def _dot_tn(a, b):
    return lax.dot_general(a, b, (((0,), (0,)), ((), ())), preferred_element_type=F32)


def _gla_direction(q_ref, k_ref, v_ref, la_ref, tri_ref, st_ref, o_ref, direction, rows):
    fwd = direction == 0
    cs = GLA_CHUNK
    n_sub = GLA_BLOCK // cs
    b = _dot(tri_ref[direction, 0], la_ref[rows, :])
    tot = b[GLA_BLOCK - 1:GLA_BLOCK] if fwd else b[0:1]

    def per_chunk_rows(row_of_chunk):
        return jnp.concatenate(
            [jnp.broadcast_to(b[row_of_chunk(c):row_of_chunk(c) + 1], (cs, GLA_DK))
             for c in range(n_sub)], axis=0)

    if fwd:
        mid = per_chunk_rows(lambda c: c * cs + cs // 2 - 1)
        edge = per_chunk_rows(lambda c: max(c * cs - 1, 0))
    else:
        mid = per_chunk_rows(lambda c: c * cs + cs // 2)
        edge = per_chunk_rows(lambda c: min((c + 1) * cs, GLA_BLOCK - 1))

    q = q_ref[rows, :].astype(F32) * (GLA_DK ** -0.5)
    k = k_ref[rows, :].astype(F32)
    v = v_ref[rows, :]
    st = st_ref[...]
    q_in = (q * jnp.exp(b)).astype(BF16)
    k_in = (k * jnp.exp(tot - b)).astype(BF16)
    q_d = (q * jnp.exp(b - mid)).astype(BF16)
    k_d = (k * jnp.exp(mid - b)).astype(BF16)
    q_o = (q * jnp.exp(b - edge)).astype(BF16)

    scores = jnp.where(tri_ref[direction, 1] > 0, _dot_nt(q_d, k_d), 0.0)
    tok = lax.broadcasted_iota(jnp.int32, (GLA_BLOCK, 1), 0)
    off = []
    for c in range(n_sub):
        lo, hi = c * cs, (c + 1) * cs
        first, last, ref_row = (0, lo, lo - 1) if fwd else (hi, GLA_BLOCK, hi)
        if last > first:
            expo = jnp.where((tok >= first) & (tok < last), b[ref_row:ref_row + 1] - b, -1e30)
            k_o = (k * jnp.exp(expo)).astype(BF16)
            off.append(_dot_nt(q_o[lo:hi], k_o))
        else:
            off.append(jnp.zeros((cs, GLA_BLOCK), F32))
    scores = scores + jnp.concatenate(off, axis=0)

    o = _dot_nt(q_in, st.astype(BF16)) + _dot(scores.astype(BF16), v)
    o_ref[rows, :] = o.astype(BF16)
    st_ref[...] = st * jnp.exp(tot) + _dot_tn(v, k_in)


def _gla_scan_kernel(qf_ref, kf_ref, vf_ref, laf_ref, qb_ref, kb_ref, vb_ref, lab_ref,
                     qc_ref, kc_ref, vc_ref, lacf_ref, lacb_ref, tri_ref,
                     of_ref, ob_ref, ocf_ref, ocb_ref, sf_ref, sb_ref):
    t = pl.program_id(2)

    @pl.when(t == 0)
    def _():
        sf_ref[...] = jnp.zeros_like(sf_ref)
        sb_ref[...] = jnp.zeros_like(sb_ref)
        whole = slice(0, GLA_BLOCK)
        _gla_direction(qc_ref, kc_ref, vc_ref, lacf_ref, tri_ref, sf_ref, ocf_ref, 0, whole)
        _gla_direction(qc_ref, kc_ref, vc_ref, lacb_ref, tri_ref, sb_ref, ocb_ref, 1, whole)

    @pl.when(t > 0)
    def _():
        blocks = [slice(s * GLA_BLOCK, (s + 1) * GLA_BLOCK) for s in range(GLA_STEP_BLOCKS)]
        for rows_f, rows_b in zip(blocks, reversed(blocks)):
            _gla_direction(qf_ref, kf_ref, vf_ref, laf_ref, tri_ref, sf_ref, of_ref, 0, rows_f)
            _gla_direction(qb_ref, kb_ref, vb_ref, lab_ref, tri_ref, sb_ref, ob_ref, 1, rows_b)


def _gla_scan_call(qkvg_x, la_x, qkvg_c, la_c):
    b, seq, _ = qkvg_x.shape
    step_rows = GLA_BLOCK * GLA_STEP_BLOCKS
    nb = seq // step_rows
    assert qkvg_c.shape[1] == GLA_BLOCK and seq % step_rows == 0
    k_off = GLA_KEY_DIM // GLA_DK
    v_off = 2 * GLA_KEY_DIM // GLA_DV

    def pos_f(t):
        return jnp.maximum(t - 1, 0)

    def pos_b(t):
        return nb - jnp.maximum(t, 1)

    def pos_c(t):
        return 0

    def specs(pos, rows, la_col_off, with_la=True):
        out = [
            pl.BlockSpec((None, rows, GLA_DK), lambda bi, h, t: (bi, pos(t), h)),
            pl.BlockSpec((None, rows, GLA_DK), lambda bi, h, t: (bi, pos(t), k_off + h)),
            pl.BlockSpec((None, rows, GLA_DV), lambda bi, h, t: (bi, pos(t), v_off + h)),
        ]
        if with_la:
            out.append(la_spec(pos, rows, la_col_off))
        return out

    def la_spec(pos, rows, la_col_off):
        return pl.BlockSpec((None, rows, GLA_DK), lambda bi, h, t: (bi, pos(t), la_col_off + h))

    def o_spec(pos, rows):
        return pl.BlockSpec((None, rows, GLA_DV), lambda bi, h, t: (bi, pos(t), h))

    tri = _gla_tri_matrices()
    in_specs = (specs(pos_f, step_rows, 0) + specs(pos_b, step_rows, GLA_HEADS)
                + specs(pos_c, GLA_BLOCK, 0, with_la=False)
                + [la_spec(pos_c, GLA_BLOCK, 0), la_spec(pos_c, GLA_BLOCK, GLA_HEADS),
                   pl.BlockSpec(tri.shape, lambda bi, h, t: (0, 0, 0, 0))])
    ox_shape = jax.ShapeDtypeStruct((b, seq, GLA_VAL_DIM), BF16)
    oc_shape = jax.ShapeDtypeStruct((b, GLA_BLOCK, GLA_VAL_DIM), BF16)
    return pl.pallas_call(
        _gla_scan_kernel,
        grid=(b, GLA_HEADS, nb + 1),
        in_specs=in_specs,
        out_specs=[o_spec(pos_f, step_rows), o_spec(pos_b, step_rows),
                   o_spec(pos_c, GLA_BLOCK), o_spec(pos_c, GLA_BLOCK)],
        out_shape=[ox_shape, ox_shape, oc_shape, oc_shape],
        scratch_shapes=[pltpu.VMEM((GLA_DV, GLA_DK), F32), pltpu.VMEM((GLA_DV, GLA_DK), F32)],
        compiler_params=_params(("arbitrary", "arbitrary", "arbitrary")),
        name="gla_scan",
    )(qkvg_x, qkvg_x, qkvg_x, la_x, qkvg_x, qkvg_x, qkvg_x, la_x,
      qkvg_c, qkvg_c, qkvg_c, la_c, la_c, tri)


def _finish_kernel(*refs, head_norm):
    if head_norm:
        x_ref, gate_ref, o1_ref, o2_ref, g_ref, gn_ref, w_ref, out_ref = refs
        o = o1_ref[...].astype(F32) + o2_ref[...].astype(F32)
        parts = []
        for h in range(GLA_HEADS):
            oh = o[:, h * GLA_DV:(h + 1) * GLA_DV]
            ms = jnp.mean(oh * oh, axis=-1, keepdims=True)
            parts.append(oh * lax.rsqrt(ms + EPS) * gn_ref[...])
        o = jnp.concatenate(parts, axis=-1)
    else:
        x_ref, gate_ref, o1_ref, g_ref, w_ref, out_ref = refs
        o = o1_ref[...].astype(F32)
    half_g = 0.5 * g_ref[...].astype(F32)
    silu_g = half_g * (1.0 + jnp.tanh(half_g))
    y_in = (o * silu_g).astype(BF16)
    y = _dot(y_in, w_ref[...])
    out_ref[...] = x_ref[...] + gate_ref[...] * y


def _finish_call(x, gate, o_list, g_arr, g_col_blk, w, *, gn_w=None, name):
    b, t, d = x.shape
    tm = min(FINISH_TM, t)
    assert t % tm == 0
    width = w.shape[0]
    in_specs = [
        pl.BlockSpec((None, tm, d), lambda bi, i: (bi, i, 0)),
        pl.BlockSpec((None, 1, d), lambda bi, i: (bi, 0, 0)),
    ]
    args = [x, gate]
    for o in o_list:
        in_specs.append(pl.BlockSpec((None, tm, width), lambda bi, i: (bi, i, 0)))
        args.append(o)
    in_specs.append(pl.BlockSpec((None, tm, width), lambda bi, i: (bi, i, g_col_blk)))
    args.append(g_arr)
    if gn_w is not None:
        in_specs.append(pl.BlockSpec((1, gn_w.shape[-1]), lambda bi, i: (0, 0)))
        args.append(gn_w)
    in_specs.append(pl.BlockSpec((width, d), lambda bi, i: (0, 0)))
    args.append(w)
    return pl.pallas_call(
        functools.partial(_finish_kernel, head_norm=gn_w is not None),
        grid=(b, t // tm),
        in_specs=in_specs,
        out_specs=pl.BlockSpec((None, tm, d), lambda bi, i: (bi, i, 0)),
        out_shape=jax.ShapeDtypeStruct((b, t, d), F32),
        compiler_params=_params(("arbitrary", "arbitrary")),
        name=name,
    )(*args)


def _attn_finalize(acc1, l1, acc2, l2, lam_ref, sw_ref, o_ref, lambda_init):
    lv = lam_ref[...]
    lam = (jnp.exp(jnp.sum(lv[0:1] * lv[1:2], axis=-1, keepdims=True))
           - jnp.exp(jnp.sum(lv[2:3] * lv[3:4], axis=-1, keepdims=True)) + lambda_init)
    o = acc1 / l1 - lam * (acc2 / l2)
    ms = jnp.mean(o * o, axis=-1, keepdims=True)
    o = o * lax.rsqrt(ms + EPS) * sw_ref[...] * (1.0 - lambda_init)
    o_ref[...] = o.astype(BF16)


def _attn_online_kernel(q1_ref, q2_ref, k1_ref, k2_ref, v_ref, k1c_ref, k2c_ref, vc_ref,
                        lam_ref, sw_ref, o_ref, m_ref, l_ref, acc_ref, *, tk, lambda_init):
    n_keys = v_ref.shape[0]
    m_ref[...] = jnp.full_like(m_ref, -1e30)
    l_ref[...] = jnp.zeros_like(l_ref)
    acc_ref[...] = jnp.zeros_like(acc_ref)

    def update(mp, q_ref, k_blk, v_blk):
        s = _dot_nt(q_ref[...], k_blk)
        m_old = m_ref[mp]
        m_new = jnp.maximum(m_old, jnp.max(s, axis=1, keepdims=True))
        alpha = jnp.exp2(m_old - m_new)
        p = jnp.exp2(s - m_new)
        l_ref[mp] = alpha * l_ref[mp] + jnp.sum(p, axis=1, keepdims=True)
        acc_ref[mp] = alpha * acc_ref[mp] + _dot(p.astype(BF16), v_blk)
        m_ref[mp] = m_new

    def body(c, carry):
        off = pl.multiple_of(c * tk, tk)
        v_blk = v_ref[pl.ds(off, tk), :]
        update(0, q1_ref, k1_ref[pl.ds(off, tk), :], v_blk)
        update(1, q2_ref, k2_ref[pl.ds(off, tk), :], v_blk)
        return carry

    lax.fori_loop(0, n_keys // tk, body, 0)
    update(0, q1_ref, k1c_ref[...], vc_ref[...])
    update(1, q2_ref, k2c_ref[...], vc_ref[...])
    _attn_finalize(acc_ref[0], l_ref[0], acc_ref[1], l_ref[1], lam_ref, sw_ref, o_ref, lambda_init)


def _attn_bounded_kernel(q1_ref, q2_ref, k1_ref, k2_ref, v_ref, k1c_ref, k2c_ref, vc_ref,
                         lam_ref, sw_ref, o_ref, l_ref, acc_ref, *, tk, lambda_init):
    n_keys = v_ref.shape[0]
    l_ref[...] = jnp.zeros_like(l_ref)
    acc_ref[...] = jnp.zeros_like(acc_ref)

    def sub_block(q, k_blk, v_blk):
        s = _dot_nt(q, k_blk)
        p = jnp.exp2(s)
        part = p[:, :V7X_LANES] + p[:, V7X_LANES:]
        return part, _dot(p.astype(BF16), v_blk)

    def body(c, carry):
        for mp, (q_ref, k_ref) in enumerate(((q1_ref, k1_ref), (q2_ref, k2_ref))):
            q = q_ref[...]
            acc = None
            lsum = None
            for sb in range(tk // ATTN_KEY_SUB):
                off = pl.multiple_of(c * tk + sb * ATTN_KEY_SUB, ATTN_KEY_SUB)
                part, d = sub_block(q, k_ref[pl.ds(off, ATTN_KEY_SUB), :],
                                    v_ref[pl.ds(off, ATTN_KEY_SUB), :])
                lsum = part if lsum is None else lsum + part
                acc = d if acc is None else acc + d
            l_ref[mp] += lsum
            acc_ref[mp] += acc
        return carry

    lax.fori_loop(0, n_keys // tk, body, 0)

    for mp, (q_ref, kc_ref) in enumerate(((q1_ref, k1c_ref), (q2_ref, k2c_ref))):
        part, d = sub_block(q_ref[...], kc_ref[...], vc_ref[...])
        l_ref[mp] += part
        acc_ref[mp] += d

    l1 = jnp.sum(l_ref[0], axis=-1, keepdims=True)
    l2 = jnp.sum(l_ref[1], axis=-1, keepdims=True)
    _attn_finalize(acc_ref[0], l1, acc_ref[1], l2, lam_ref, sw_ref, o_ref, lambda_init)


def _attn_call(qkvg_x, qkvg_c, lam_v, subln_w, lambda_init, *, bounded):
    b, seq, _ = qkvg_x.shape
    n_ctx = qkvg_c.shape[1]
    tq, tk = (ATTN_TQ, ATTN_TK) if bounded else (ATTN_TQ_ONLINE, ATTN_TK_ONLINE)
    assert seq % tq == 0 and seq % tk == 0 and tk % ATTN_KEY_SUB == 0 and n_ctx == ATTN_KEY_SUB
    d = DIFF_HEAD_DIM
    k_off = DIFF_WIDTH // d
    v_off = 2 * DIFF_WIDTH // DIFF_VAL_DIM
    in_specs = [
        pl.BlockSpec((None, tq, d), lambda bi, h, i: (bi, i, 2 * h)),
        pl.BlockSpec((None, tq, d), lambda bi, h, i: (bi, i, 2 * h + 1)),
        pl.BlockSpec((None, seq, d), lambda bi, h, i: (bi, 0, k_off + 2 * h)),
        pl.BlockSpec((None, seq, d), lambda bi, h, i: (bi, 0, k_off + 2 * h + 1)),
        pl.BlockSpec((None, seq, DIFF_VAL_DIM), lambda bi, h, i: (bi, 0, v_off + h)),
        pl.BlockSpec((None, n_ctx, d), lambda bi, h, i: (bi, 0, k_off + 2 * h)),
        pl.BlockSpec((None, n_ctx, d), lambda bi, h, i: (bi, 0, k_off + 2 * h + 1)),
        pl.BlockSpec((None, n_ctx, DIFF_VAL_DIM), lambda bi, h, i: (bi, 0, v_off + h)),
        pl.BlockSpec(lam_v.shape, lambda bi, h, i: (0, 0)),
        pl.BlockSpec((1, DIFF_VAL_DIM), lambda bi, h, i: (0, 0)),
    ]
    if bounded:
        body = functools.partial(_attn_bounded_kernel, tk=tk, lambda_init=lambda_init)
        scratch = [pltpu.VMEM((2, tq, V7X_LANES), F32), pltpu.VMEM((2, tq, DIFF_VAL_DIM), F32)]
    else:
        body = functools.partial(_attn_online_kernel, tk=tk, lambda_init=lambda_init)
        scratch = [pltpu.VMEM((2, tq, 1), F32), pltpu.VMEM((2, tq, 1), F32),
                   pltpu.VMEM((2, tq, DIFF_VAL_DIM), F32)]
    return pl.pallas_call(
        body,
        grid=(b, DIFF_HEADS, seq // tq),
        in_specs=in_specs,
        out_specs=pl.BlockSpec((None, tq, DIFF_VAL_DIM), lambda bi, h, i: (bi, i, h)),
        out_shape=jax.ShapeDtypeStruct((b, seq, DIFF_WIDTH), BF16),
        scratch_shapes=scratch,
        compiler_params=_params(("arbitrary", "arbitrary", "arbitrary")),
        name="diff_attn_bounded" if bounded else "diff_attn_online",
    )(qkvg_x, qkvg_x, qkvg_x, qkvg_x, qkvg_x, qkvg_c, qkvg_c, qkvg_c,
      lam_v, subln_w.reshape(1, DIFF_VAL_DIM))


def _axial_rope_tables(n_tokens):
    rows_n = n_tokens // GRID_W
    half = DIFF_HEAD_DIM // 2
    inv_freq = (ROPE_BASE ** (-np.arange(0, half, 2, dtype=np.float32) / half)).astype(np.float32)
    ang_r = np.arange(rows_n, dtype=np.float32)[:, None] * inv_freq
    ang_c = np.arange(GRID_W, dtype=np.float32)[:, None] * inv_freq

    def expand(fn):
        tr, tc = fn(ang_r), fn(ang_c)
        by_row = np.concatenate([tr, tr, np.zeros_like(tr), np.zeros_like(tr)], axis=-1)
        by_col = np.concatenate([np.zeros_like(tc), np.zeros_like(tc), tc, tc], axis=-1)
        full = jnp.asarray(by_row)[:, None, :] + jnp.asarray(by_col)[None, :, :]
        return full.reshape(n_tokens, DIFF_HEAD_DIM)

    return expand(np.cos), expand(np.sin)


def kernel(x, c, ctx, c_ctx, norm_w, ada_w, ada_b, gla_w_in, gla_gate_a1, gla_gate_a2, gla_gate_b,
           gla_gn_w, gla_w_out, diff_w_in, diff_qn_w, diff_kn_w, diff_lam, diff_subln_w, diff_w_out):
    b, seq, d = x.shape
    ctx_len = ctx.shape[1]
    assert ctx_len == GLA_BLOCK and seq % GLA_BLOCK == 0 and d == D_MODEL

    cond = jnp.zeros((ADA_ROWS, d), F32).at[:b].set(c).at[b].set(c_ctx)
    mod = _ada_call(cond, ada_w, ada_b)

    def mod_rows(layer, part):
        m = mod[layer, :, part * d:(part + 1) * d]
        lat = m[:b, None, :]
        cx = jnp.broadcast_to(m[b][None, None, :], (b, 1, d))
        return lat, cx

    shift, shift_c = mod_rows(0, 0)
    scale, scale_c = mod_rows(0, 1)
    gate, gate_c = mod_rows(0, 2)
    nw0 = norm_w[0].reshape(1, d)
    w_in0 = gla_w_in[0].astype(BF16)
    a1 = jnp.zeros((d, V7X_LANES), F32)
    a1 = a1.at[:, :GLA_GATE_RANK].set(gla_gate_a1[0, 0]).at[:, GLA_GATE_RANK:2 * GLA_GATE_RANK].set(
        gla_gate_a1[0, 1]).astype(BF16)
    a2 = jnp.zeros((V7X_LANES, 2 * GLA_KEY_DIM), F32)
    a2 = a2.at[:GLA_GATE_RANK, :GLA_KEY_DIM].set(gla_gate_a2[0, 0])
    a2 = a2.at[GLA_GATE_RANK:2 * GLA_GATE_RANK, GLA_KEY_DIM:].set(gla_gate_a2[0, 1]).astype(BF16)
    gb = gla_gate_b[0].reshape(1, 2 * GLA_KEY_DIM)
    gla_extras = [a1, a2, gb]
    gla_extra_specs = [
        pl.BlockSpec(a1.shape, lambda bi, i, j: (0, 0)),
        pl.BlockSpec(a2.shape, lambda bi, i, j: (0, 0)),
        pl.BlockSpec(gb.shape, lambda bi, i, j: (0, 0)),
    ]
    la_out = (2 * GLA_KEY_DIM, BF16)
    tm0, tn0 = GLA_INPROJ_TILE
    qkvg0_x, la0_x = _inproj_call(
        _gla_inproj_kernel, x, shift, scale, nw0, w_in0, gla_extras, gla_extra_specs,
        tm=tm0, tn=tn0, extra_out=la_out, name="gla_inproj_x")
    qkvg0_c, la0_c = _inproj_call(
        _gla_inproj_kernel, ctx, shift_c, scale_c, nw0, w_in0, gla_extras, gla_extra_specs,
        tm=CTX_INPROJ_TM, tn=tn0, extra_out=la_out, name="gla_inproj_ctx")

    o_f, o_b, oc_f, oc_b = _gla_scan_call(qkvg0_x, la0_x, qkvg0_c, la0_c)

    w_out0 = gla_w_out[0].astype(BF16)
    gn_w = gla_gn_w[0].reshape(1, GLA_DV)
    g_col_blk0 = (2 * GLA_KEY_DIM + GLA_VAL_DIM) // GLA_VAL_DIM
    x1 = _finish_call(x, gate, [o_f, o_b], qkvg0_x, g_col_blk0, w_out0, gn_w=gn_w,
                      name="gla_finish_x")
    ctx1 = _finish_call(ctx, gate_c, [oc_f, oc_b], qkvg0_c, g_col_blk0, w_out0, gn_w=gn_w,
                        name="gla_finish_ctx")

    lambda_init = 0.8 - 0.6 * math.exp(-0.3 * 1)
    shift, shift_c = mod_rows(1, 0)
    scale, scale_c = mod_rows(1, 1)
    gate, _ = mod_rows(1, 2)
    nw1 = norm_w[1].reshape(1, d)
    w_in1 = diff_w_in[0].astype(BF16)
    cos, sin = _axial_rope_tables(seq)
    q_scale = math.log2(math.e) * DIFF_HEAD_DIM ** -0.5
    tm1, tn1 = DIFF_INPROJ_TILE
    head_mats = _head_matrices()
    nrm_args = [diff_qn_w[0].reshape(1, DIFF_VAL_DIM), diff_kn_w[0].reshape(1, DIFF_VAL_DIM)]
    nrm_specs = [
        pl.BlockSpec((1, DIFF_VAL_DIM), lambda bi, i, j: (0, 0)),
        pl.BlockSpec((1, DIFF_VAL_DIM), lambda bi, i, j: (0, 0)),
    ]
    hm_spec = pl.BlockSpec(head_mats.shape, lambda bi, i, j: (0, 0, 0))
    (qkvg1_x,) = _inproj_call(
        functools.partial(_diff_inproj_kernel, rope=True, tn=tn1, q_scale=q_scale),
        x1, shift, scale, nw1, w_in1, nrm_args + [cos, sin, head_mats],
        nrm_specs + [pl.BlockSpec((tm1, DIFF_HEAD_DIM), lambda bi, i, j: (i, 0)),
                     pl.BlockSpec((tm1, DIFF_HEAD_DIM), lambda bi, i, j: (i, 0)), hm_spec],
        tm=tm1, tn=tn1, name="diff_inproj_x")
    (qkvg1_c,) = _inproj_call(
        functools.partial(_diff_inproj_kernel, rope=False, tn=tn1, q_scale=q_scale),
        ctx1, shift_c, scale_c, nw1, w_in1, nrm_args + [cos, sin, head_mats],
        nrm_specs + [pl.BlockSpec((CTX_INPROJ_TM, DIFF_HEAD_DIM), lambda bi, i, j: (0, 0)),
                     pl.BlockSpec((CTX_INPROJ_TM, DIFF_HEAD_DIM), lambda bi, i, j: (0, 0)), hm_spec],
        tm=CTX_INPROJ_TM, tn=tn1, name="diff_inproj_ctx")

    score_bound = q_scale * DIFF_HEAD_DIM * jnp.max(
        jnp.max(jnp.abs(diff_qn_w[0]), axis=-1) * jnp.max(jnp.abs(diff_kn_w[0]), axis=-1))
    attn = functools.partial(_attn_call, qkvg1_x, qkvg1_c, diff_lam[0], diff_subln_w[0], lambda_init)
    o_attn = lax.cond(score_bound <= ATTN_SCORE_BOUND,
                      functools.partial(attn, bounded=True), functools.partial(attn, bounded=False))

    w_out1 = diff_w_out[0].astype(BF16)
    g_col_blk1 = 3 * DIFF_WIDTH // DIFF_WIDTH
    return _finish_call(x1, gate, [o_attn], qkvg1_x, g_col_blk1, w_out1, name="diff_finish_x")
```

```python
import functools
import math

import jax
import jax.numpy as jnp
import numpy as np
from jax import lax
from jax.experimental import pallas as pl
from jax.experimental.pallas import tpu as pltpu

F32 = jnp.float32
BF16 = jnp.bfloat16

D_MODEL = 2048
DEPTH = 2
GRID_W = 64
EPS = 1e-6
GLA_HEADS = 4
GLA_KEY_DIM = D_MODEL // 2
GLA_VAL_DIM = D_MODEL
GLA_DK = GLA_KEY_DIM // GLA_HEADS
GLA_DV = GLA_VAL_DIM // GLA_HEADS
GLA_GATE_RANK = 16
GLA_GATE_NORM = 16.0
GLA_CHUNK = 64
GLA_IN_DIM = 2 * GLA_KEY_DIM + 2 * GLA_VAL_DIM
DIFF_HEAD_DIM = 128
DIFF_HEADS = D_MODEL // (2 * DIFF_HEAD_DIM)
DIFF_VAL_DIM = 2 * DIFF_HEAD_DIM
DIFF_WIDTH = DIFF_HEADS * DIFF_VAL_DIM
DIFF_IN_DIM = 4 * DIFF_WIDTH
ROPE_BASE = 10000.0

V7X_LANES = 128
V7X_SUBLANES = 8
V7X_MXU_DIM = 256
VMEM_LIMIT = 56 * 1024 * 1024

ADA_ROWS = V7X_SUBLANES
ADA_TN = 1024
GLA_BLOCK = 256
GLA_STEP_BLOCKS = 4
GLA_INPROJ_TILE = (1024, 1024)
DIFF_INPROJ_TILE = (1024, 1024)
CTX_INPROJ_TM = 256
FINISH_TM = 512
ATTN_TQ = 1024
ATTN_TK = 8192
ATTN_TQ_ONLINE = 512
ATTN_TK_ONLINE = 1024
ATTN_KEY_SUB = V7X_MXU_DIM
ATTN_SCORE_BOUND = 60.0


def _dot(a, b):
    return jnp.dot(a, b, preferred_element_type=F32)


def _dot_nt(a, b):
    return lax.dot_general(a, b, (((1,), (1,)), ((), ())), preferred_element_type=F32)


def _params(sem):
    return pltpu.CompilerParams(dimension_semantics=sem, vmem_limit_bytes=VMEM_LIMIT)


def _ada_kernel(c_ref, w_ref, b_ref, o_ref):
    c = c_ref[...]
    s = c * jax.nn.sigmoid(c)
    o_ref[...] = jnp.dot(s, w_ref[...], preferred_element_type=F32,
                         precision=lax.Precision.HIGHEST) + b_ref[...]


def _ada_call(cond, ada_w, ada_b):
    tn = ADA_TN
    n = ada_w.shape[-1]
    return pl.pallas_call(
        _ada_kernel,
        grid=(DEPTH, n // tn),
        in_specs=[
            pl.BlockSpec((ADA_ROWS, D_MODEL), lambda l, j: (0, 0)),
            pl.BlockSpec((None, D_MODEL, tn), lambda l, j: (l, 0, j)),
            pl.BlockSpec((None, 1, tn), lambda l, j: (l, 0, j)),
        ],
        out_specs=pl.BlockSpec((None, ADA_ROWS, tn), lambda l, j: (l, 0, j)),
        out_shape=jax.ShapeDtypeStruct((DEPTH, ADA_ROWS, n), F32),
        compiler_params=_params(("arbitrary", "arbitrary")),
        name="ada_mod",
    )(cond, ada_w, ada_b.reshape(DEPTH, 1, n))


def _normalize_into(x_ref, shift_ref, scale_ref, nw_ref, hn_ref):
    x = x_ref[...]
    ms = jnp.mean(x * x, axis=-1, keepdims=True)
    a = nw_ref[...] * (1.0 + scale_ref[...])
    hn_ref[...] = (x * lax.rsqrt(ms + EPS) * a + shift_ref[...]).astype(BF16)


def _gla_inproj_kernel(x_ref, shift_ref, scale_ref, nw_ref, w_ref, a1_ref, a2_ref, gb_ref,
                       out_ref, la_ref, hn_ref):
    j = pl.program_id(2)

    @pl.when(j == 0)
    def _():
        _normalize_into(x_ref, shift_ref, scale_ref, nw_ref, hn_ref)
        z = _dot(hn_ref[...], a1_ref[...])
        lg = _dot(z.astype(BF16), a2_ref[...]) + gb_ref[...]
        log_sig = jnp.minimum(lg, 0.0) - jnp.log(1.0 + jnp.exp(-jnp.abs(lg)))
        la_ref[...] = (log_sig * (1.0 / GLA_GATE_NORM)).astype(la_ref.dtype)

    out_ref[...] = _dot(hn_ref[...], w_ref[...]).astype(BF16)


def _head_matrices():
    qd = DIFF_HEAD_DIM // 4
    e = np.arange(qd)
    rot = np.zeros((DIFF_HEAD_DIM, DIFF_HEAD_DIM), np.float32)
    rot[qd + e, e] = -1.0
    rot[e, qd + e] = 1.0
    rot[3 * qd + e, 2 * qd + e] = -1.0
    rot[2 * qd + e, 3 * qd + e] = 1.0
    both_maps = np.eye(2, dtype=np.float32)
    mean = np.kron(both_maps, np.full((DIFF_HEAD_DIM, DIFF_HEAD_DIM), 1.0 / DIFF_HEAD_DIM, np.float32))
    return jnp.asarray(np.stack([mean, np.kron(both_maps, rot)]), BF16)


def _diff_inproj_kernel(x_ref, shift_ref, scale_ref, nw_ref, w_ref, qn_ref, kn_ref,
                        cos_ref, sin_ref, hm_ref, out_ref, hn_ref, *, rope, tn, q_scale):
    j = pl.program_id(2)
    tiles_per_part = DIFF_WIDTH // tn

    @pl.when(j == 0)
    def _():
        _normalize_into(x_ref, shift_ref, scale_ref, nw_ref, hn_ref)

    acc = _dot(hn_ref[...], w_ref[...])

    def head_norm(nrm_ref, scale):
        w2 = nrm_ref[...] * scale
        if rope:
            cos2 = jnp.concatenate([cos_ref[...], cos_ref[...]], axis=1)
            sin2 = jnp.concatenate([sin_ref[...], sin_ref[...]], axis=1)
        for head in range(tn // DIFF_VAL_DIM):
            cols = slice(head * DIFF_VAL_DIM, (head + 1) * DIFF_VAL_DIM)
            t2 = acc[:, cols]
            ms = _dot((t2 * t2).astype(BF16), hm_ref[0])
            u = t2 * w2
            if rope:
                u = u * cos2 + _dot(u.astype(BF16), hm_ref[1]) * sin2
            out_ref[:, cols] = (u * lax.rsqrt(ms + EPS)).astype(BF16)

    @pl.when(j < tiles_per_part)
    def _():
        head_norm(qn_ref, q_scale)

    @pl.when((j >= tiles_per_part) & (j < 2 * tiles_per_part))
    def _():
        head_norm(kn_ref, 1.0)

    @pl.when(j >= 2 * tiles_per_part)
    def _():
        out_ref[...] = acc.astype(BF16)


def _inproj_call(kernel, x, shift, scale, nw, w, extras, extra_specs, *, tm, tn,
                 extra_out=None, name):
    b, t, d = x.shape
    n = w.shape[-1]
    assert t % tm == 0 and n % tn == 0
    in_specs = [
        pl.BlockSpec((None, tm, d), lambda bi, i, j: (bi, i, 0)),
        pl.BlockSpec((None, 1, d), lambda bi, i, j: (bi, 0, 0)),
        pl.BlockSpec((None, 1, d), lambda bi, i, j: (bi, 0, 0)),
        pl.BlockSpec((1, d), lambda bi, i, j: (0, 0)),
        pl.BlockSpec((d, tn), lambda bi, i, j: (0, j)),
    ] + extra_specs
    out_specs = [pl.BlockSpec((None, tm, tn), lambda bi, i, j: (bi, i, j))]
    out_shape = [jax.ShapeDtypeStruct((b, t, n), BF16)]
    if extra_out is not None:
        width, dtype = extra_out
        out_specs.append(pl.BlockSpec((None, tm, width), lambda bi, i, j: (bi, i, 0)))
        out_shape.append(jax.ShapeDtypeStruct((b, t, width), dtype))
    return pl.pallas_call(
        kernel,
        grid=(b, t // tm, n // tn),
        in_specs=in_specs,
        out_specs=out_specs,
        out_shape=out_shape,
        scratch_shapes=[pltpu.VMEM((tm, d), BF16)],
        compiler_params=_params(("arbitrary", "arbitrary", "arbitrary")),
        name=name,
    )(x, shift, scale, nw, w, *extras)


def _gla_tri_matrices():
    idx = jnp.arange(GLA_BLOCK)
    i, j = idx[:, None], idx[None, :]
    same = (i // GLA_CHUNK) == (j // GLA_CHUNK)
    lower, upper = j <= i, j >= i
    return jnp.stack([jnp.stack([lower, lower & same]),
                      jnp.stack([upper, upper & same])]).astype(BF16)


def _dot_tn(a, b):
    return lax.dot_general(a, b, (((0,), (0,)), ((), ())), preferred_element_type=F32)


def _gla_direction(q_ref, k_ref, v_ref, la_ref, tri_ref, st_ref, o_ref, direction, rows):
    fwd = direction == 0
    cs = GLA_CHUNK
    n_sub = GLA_BLOCK // cs
    b = _dot(tri_ref[direction, 0], la_ref[rows, :])
    tot = b[GLA_BLOCK - 1:GLA_BLOCK] if fwd else b[0:1]

    def per_chunk_rows(row_of_chunk):
        return jnp.concatenate(
            [jnp.broadcast_to(b[row_of_chunk(c):row_of_chunk(c) + 1], (cs, GLA_DK))
             for c in range(n_sub)], axis=0)

    if fwd:
        mid = per_chunk_rows(lambda c: c * cs + cs // 2 - 1)
        edge = per_chunk_rows(lambda c: max(c * cs - 1, 0))
    else:
        mid = per_chunk_rows(lambda c: c * cs + cs // 2)
        edge = per_chunk_rows(lambda c: min((c + 1) * cs, GLA_BLOCK - 1))

    q = q_ref[rows, :].astype(F32) * (GLA_DK ** -0.5)
    k = k_ref[rows, :].astype(F32)
    v = v_ref[rows, :]
    st = st_ref[...]
    q_in = (q * jnp.exp(b)).astype(BF16)
    k_in = (k * jnp.exp(tot - b)).astype(BF16)
    q_d = (q * jnp.exp(b - mid)).astype(BF16)
    k_d = (k * jnp.exp(mid - b)).astype(BF16)
    q_o = (q * jnp.exp(b - edge)).astype(BF16)

    scores = jnp.where(tri_ref[direction, 1] > 0, _dot_nt(q_d, k_d), 0.0)
    tok = lax.broadcasted_iota(jnp.int32, (GLA_BLOCK, 1), 0)
    off = []
    for c in range(n_sub):
        lo, hi = c * cs, (c + 1) * cs
        first, last, ref_row = (0, lo, lo - 1) if fwd else (hi, GLA_BLOCK, hi)
        if last > first:
            expo = jnp.where((tok >= first) & (tok < last), b[ref_row:ref_row + 1] - b, -1e30)
            k_o = (k * jnp.exp(expo)).astype(BF16)
            off.append(_dot_nt(q_o[lo:hi], k_o))
        else:
            off.append(jnp.zeros((cs, GLA_BLOCK), F32))
    scores = scores + jnp.concatenate(off, axis=0)

    o = _dot_nt(q_in, st.astype(BF16)) + _dot(scores.astype(BF16), v)
    o_ref[rows, :] = o.astype(BF16)
    st_ref[...] = st * jnp.exp(tot) + _dot_tn(v, k_in)


def _gla_scan_kernel(qf_ref, kf_ref, vf_ref, laf_ref, qb_ref, kb_ref, vb_ref, lab_ref,
                     qc_ref, kc_ref, vc_ref, lacf_ref, lacb_ref, tri_ref,
                     of_ref, ob_ref, ocf_ref, ocb_ref, sf_ref, sb_ref):
    t = pl.program_id(2)

    @pl.when(t == 0)
    def _():
        sf_ref[...] = jnp.zeros_like(sf_ref)
        sb_ref[...] = jnp.zeros_like(sb_ref)
        whole = slice(0, GLA_BLOCK)
        _gla_direction(qc_ref, kc_ref, vc_ref, lacf_ref, tri_ref, sf_ref, ocf_ref, 0, whole)
        _gla_direction(qc_ref, kc_ref, vc_ref, lacb_ref, tri_ref, sb_ref, ocb_ref, 1, whole)

    @pl.when(t > 0)
    def _():
        blocks = [slice(s * GLA_BLOCK, (s + 1) * GLA_BLOCK) for s in range(GLA_STEP_BLOCKS)]
        for rows_f, rows_b in zip(blocks, reversed(blocks)):
            _gla_direction(qf_ref, kf_ref, vf_ref, laf_ref, tri_ref, sf_ref, of_ref, 0, rows_f)
            _gla_direction(qb_ref, kb_ref, vb_ref, lab_ref, tri_ref, sb_ref, ob_ref, 1, rows_b)


def _gla_scan_call(qkvg_x, la_x, qkvg_c, la_c):
    b, seq, _ = qkvg_x.shape
    step_rows = GLA_BLOCK * GLA_STEP_BLOCKS
    nb = seq // step_rows
    assert qkvg_c.shape[1] == GLA_BLOCK and seq % step_rows == 0
    k_off = GLA_KEY_DIM // GLA_DK
    v_off = 2 * GLA_KEY_DIM // GLA_DV

    def pos_f(t):
        return jnp.maximum(t - 1, 0)

    def pos_b(t):
        return nb - jnp.maximum(t, 1)

    def pos_c(t):
        return 0

    def specs(pos, rows, la_col_off, with_la=True):
        out = [
            pl.BlockSpec((None, rows, GLA_DK), lambda bi, h, t: (bi, pos(t), h)),
            pl.BlockSpec((None, rows, GLA_DK), lambda bi, h, t: (bi, pos(t), k_off + h)),
            pl.BlockSpec((None, rows, GLA_DV), lambda bi, h, t: (bi, pos(t), v_off + h)),
        ]
        if with_la:
            out.append(la_spec(pos, rows, la_col_off))
        return out

    def la_spec(pos, rows, la_col_off):
        return pl.BlockSpec((None, rows, GLA_DK), lambda bi, h, t: (bi, pos(t), la_col_off + h))

    def o_spec(pos, rows):
        return pl.BlockSpec((None, rows, GLA_DV), lambda bi, h, t: (bi, pos(t), h))

    tri = _gla_tri_matrices()
    in_specs = (specs(pos_f, step_rows, 0) + specs(pos_b, step_rows, GLA_HEADS)
                + specs(pos_c, GLA_BLOCK, 0, with_la=False)
                + [la_spec(pos_c, GLA_BLOCK, 0), la_spec(pos_c, GLA_BLOCK, GLA_HEADS),
                   pl.BlockSpec(tri.shape, lambda bi, h, t: (0, 0, 0, 0))])
    ox_shape = jax.ShapeDtypeStruct((b, seq, GLA_VAL_DIM), BF16)
    oc_shape = jax.ShapeDtypeStruct((b, GLA_BLOCK, GLA_VAL_DIM), BF16)
    return pl.pallas_call(
        _gla_scan_kernel,
        grid=(b, GLA_HEADS, nb + 1),
        in_specs=in_specs,
        out_specs=[o_spec(pos_f, step_rows), o_spec(pos_b, step_rows),
                   o_spec(pos_c, GLA_BLOCK), o_spec(pos_c, GLA_BLOCK)],
        out_shape=[ox_shape, ox_shape, oc_shape, oc_shape],
        scratch_shapes=[pltpu.VMEM((GLA_DV, GLA_DK), F32), pltpu.VMEM((GLA_DV, GLA_DK), F32)],
        compiler_params=_params(("arbitrary", "arbitrary", "arbitrary")),
        name="gla_scan",
    )(qkvg_x, qkvg_x, qkvg_x, la_x, qkvg_x, qkvg_x, qkvg_x, la_x,
      qkvg_c, qkvg_c, qkvg_c, la_c, la_c, tri)


def _finish_kernel(*refs, head_norm):
    if head_norm:
        x_ref, gate_ref, o1_ref, o2_ref, g_ref, gn_ref, w_ref, out_ref = refs
        o = o1_ref[...].astype(F32) + o2_ref[...].astype(F32)
        parts = []
        for h in range(GLA_HEADS):
            oh = o[:, h * GLA_DV:(h + 1) * GLA_DV]
            ms = jnp.mean(oh * oh, axis=-1, keepdims=True)
            parts.append(oh * lax.rsqrt(ms + EPS) * gn_ref[...])
        o = jnp.concatenate(parts, axis=-1)
    else:
        x_ref, gate_ref, o1_ref, g_ref, w_ref, out_ref = refs
        o = o1_ref[...].astype(F32)
    half_g = 0.5 * g_ref[...].astype(F32)
    silu_g = half_g * (1.0 + jnp.tanh(half_g))
    y_in = (o * silu_g).astype(BF16)
    y = _dot(y_in, w_ref[...])
    out_ref[...] = x_ref[...] + gate_ref[...] * y


def _finish_call(x, gate, o_list, g_arr, g_col_blk, w, *, gn_w=None, name):
    b, t, d = x.shape
    tm = min(FINISH_TM, t)
    assert t % tm == 0
    width = w.shape[0]
    in_specs = [
        pl.BlockSpec((None, tm, d), lambda bi, i: (bi, i, 0)),
        pl.BlockSpec((None, 1, d), lambda bi, i: (bi, 0, 0)),
    ]
    args = [x, gate]
    for o in o_list:
        in_specs.append(pl.BlockSpec((None, tm, width), lambda bi, i: (bi, i, 0)))
        args.append(o)
    in_specs.append(pl.BlockSpec((None, tm, width), lambda bi, i: (bi, i, g_col_blk)))
    args.append(g_arr)
    if gn_w is not None:
        in_specs.append(pl.BlockSpec((1, gn_w.shape[-1]), lambda bi, i: (0, 0)))
        args.append(gn_w)
    in_specs.append(pl.BlockSpec((width, d), lambda bi, i: (0, 0)))
    args.append(w)
    return pl.pallas_call(
        functools.partial(_finish_kernel, head_norm=gn_w is not None),
        grid=(b, t // tm),
        in_specs=in_specs,
        out_specs=pl.BlockSpec((None, tm, d), lambda bi, i: (bi, i, 0)),
        out_shape=jax.ShapeDtypeStruct((b, t, d), F32),
        compiler_params=_params(("arbitrary", "arbitrary")),
        name=name,
    )(*args)


def _attn_finalize(acc1, l1, acc2, l2, lam_ref, sw_ref, o_ref, lambda_init):
    lv = lam_ref[...]
    lam = (jnp.exp(jnp.sum(lv[0:1] * lv[1:2], axis=-1, keepdims=True))
           - jnp.exp(jnp.sum(lv[2:3] * lv[3:4], axis=-1, keepdims=True)) + lambda_init)
    o = acc1 / l1 - lam * (acc2 / l2)
    ms = jnp.mean(o * o, axis=-1, keepdims=True)
    o = o * lax.rsqrt(ms + EPS) * sw_ref[...] * (1.0 - lambda_init)
    o_ref[...] = o.astype(BF16)


def _attn_online_kernel(q1_ref, q2_ref, k1_ref, k2_ref, v_ref, k1c_ref, k2c_ref, vc_ref,
                        lam_ref, sw_ref, o_ref, m_ref, l_ref, acc_ref, *, tk, lambda_init):
    n_keys = v_ref.shape[0]
    m_ref[...] = jnp.full_like(m_ref, -1e30)
    l_ref[...] = jnp.zeros_like(l_ref)
    acc_ref[...] = jnp.zeros_like(acc_ref)

    def update(mp, q_ref, k_blk, v_blk):
        s = _dot_nt(q_ref[...], k_blk)
        m_old = m_ref[mp]
        m_new = jnp.maximum(m_old, jnp.max(s, axis=1, keepdims=True))
        alpha = jnp.exp2(m_old - m_new)
        p = jnp.exp2(s - m_new)
        l_ref[mp] = alpha * l_ref[mp] + jnp.sum(p, axis=1, keepdims=True)
        acc_ref[mp] = alpha * acc_ref[mp] + _dot(p.astype(BF16), v_blk)
        m_ref[mp] = m_new

    def body(c, carry):
        off = pl.multiple_of(c * tk, tk)
        v_blk = v_ref[pl.ds(off, tk), :]
        update(0, q1_ref, k1_ref[pl.ds(off, tk), :], v_blk)
        update(1, q2_ref, k2_ref[pl.ds(off, tk), :], v_blk)
        return carry

    lax.fori_loop(0, n_keys // tk, body, 0)
    update(0, q1_ref, k1c_ref[...], vc_ref[...])
    update(1, q2_ref, k2c_ref[...], vc_ref[...])
    _attn_finalize(acc_ref[0], l_ref[0], acc_ref[1], l_ref[1], lam_ref, sw_ref, o_ref, lambda_init)


def _attn_bounded_kernel(q1_ref, q2_ref, k1_ref, k2_ref, v_ref, k1c_ref, k2c_ref, vc_ref,
                         lam_ref, sw_ref, o_ref, l_ref, acc_ref, *, tk, lambda_init):
    n_keys = v_ref.shape[0]
    l_ref[...] = jnp.zeros_like(l_ref)
    acc_ref[...] = jnp.zeros_like(acc_ref)

    def sub_block(q, k_blk, v_blk):
        s = _dot_nt(q, k_blk)
        p = jnp.exp2(s)
        part = p[:, :V7X_LANES] + p[:, V7X_LANES:]
        return part, _dot(p.astype(BF16), v_blk)

    def body(c, carry):
        for mp, (q_ref, k_ref) in enumerate(((q1_ref, k1_ref), (q2_ref, k2_ref))):
            q = q_ref[...]
            acc = None
            lsum = None
            for sb in range(tk // ATTN_KEY_SUB):
                off = pl.multiple_of(c * tk + sb * ATTN_KEY_SUB, ATTN_KEY_SUB)
                part, d = sub_block(q, k_ref[pl.ds(off, ATTN_KEY_SUB), :],
                                    v_ref[pl.ds(off, ATTN_KEY_SUB), :])
                lsum = part if lsum is None else lsum + part
                acc = d if acc is None else acc + d
            l_ref[mp] += lsum
            acc_ref[mp] += acc
        return carry

    lax.fori_loop(0, n_keys // tk, body, 0)

    for mp, (q_ref, kc_ref) in enumerate(((q1_ref, k1c_ref), (q2_ref, k2c_ref))):
        part, d = sub_block(q_ref[...], kc_ref[...], vc_ref[...])
        l_ref[mp] += part
        acc_ref[mp] += d

    l1 = jnp.sum(l_ref[0], axis=-1, keepdims=True)
    l2 = jnp.sum(l_ref[1], axis=-1, keepdims=True)
    _attn_finalize(acc_ref[0], l1, acc_ref[1], l2, lam_ref, sw_ref, o_ref, lambda_init)


def _attn_call(qkvg_x, qkvg_c, lam_v, subln_w, lambda_init, *, bounded):
    b, seq, _ = qkvg_x.shape
    n_ctx = qkvg_c.shape[1]
    tq, tk = (ATTN_TQ, ATTN_TK) if bounded else (ATTN_TQ_ONLINE, ATTN_TK_ONLINE)
    assert seq % tq == 0 and seq % tk == 0 and tk % ATTN_KEY_SUB == 0 and n_ctx == ATTN_KEY_SUB
    d = DIFF_HEAD_DIM
    k_off = DIFF_WIDTH // d
    v_off = 2 * DIFF_WIDTH // DIFF_VAL_DIM
    in_specs = [
        pl.BlockSpec((None, tq, d), lambda bi, h, i: (bi, i, 2 * h)),
        pl.BlockSpec((None, tq, d), lambda bi, h, i: (bi, i, 2 * h + 1)),
        pl.BlockSpec((None, seq, d), lambda bi, h, i: (bi, 0, k_off + 2 * h)),
        pl.BlockSpec((None, seq, d), lambda bi, h, i: (bi, 0, k_off + 2 * h + 1)),
        pl.BlockSpec((None, seq, DIFF_VAL_DIM), lambda bi, h, i: (bi, 0, v_off + h)),
        pl.BlockSpec((None, n_ctx, d), lambda bi, h, i: (bi, 0, k_off + 2 * h)),
        pl.BlockSpec((None, n_ctx, d), lambda bi, h, i: (bi, 0, k_off + 2 * h + 1)),
        pl.BlockSpec((None, n_ctx, DIFF_VAL_DIM), lambda bi, h, i: (bi, 0, v_off + h)),
        pl.BlockSpec(lam_v.shape, lambda bi, h, i: (0, 0)),
        pl.BlockSpec((1, DIFF_VAL_DIM), lambda bi, h, i: (0, 0)),
    ]
    if bounded:
        body = functools.partial(_attn_bounded_kernel, tk=tk, lambda_init=lambda_init)
        scratch = [pltpu.VMEM((2, tq, V7X_LANES), F32), pltpu.VMEM((2, tq, DIFF_VAL_DIM), F32)]
    else:
        body = functools.partial(_attn_online_kernel, tk=tk, lambda_init=lambda_init)
        scratch = [pltpu.VMEM((2, tq, 1), F32), pltpu.VMEM((2, tq, 1), F32),
                   pltpu.VMEM((2, tq, DIFF_VAL_DIM), F32)]
    return pl.pallas_call(
        body,
        grid=(b, DIFF_HEADS, seq // tq),
        in_specs=in_specs,
        out_specs=pl.BlockSpec((None, tq, DIFF_VAL_DIM), lambda bi, h, i: (bi, i, h)),
        out_shape=jax.ShapeDtypeStruct((b, seq, DIFF_WIDTH), BF16),
        scratch_shapes=scratch,
        compiler_params=_params(("arbitrary", "arbitrary", "arbitrary")),
        name="diff_attn_bounded" if bounded else "diff_attn_online",
    )(qkvg_x, qkvg_x, qkvg_x, qkvg_x, qkvg_x, qkvg_c, qkvg_c, qkvg_c,
      lam_v, subln_w.reshape(1, DIFF_VAL_DIM))


def _axial_rope_tables(n_tokens):
    rows_n = n_tokens // GRID_W
    half = DIFF_HEAD_DIM // 2
    inv_freq = (ROPE_BASE ** (-np.arange(0, half, 2, dtype=np.float32) / half)).astype(np.float32)
    ang_r = np.arange(rows_n, dtype=np.float32)[:, None] * inv_freq
    ang_c = np.arange(GRID_W, dtype=np.float32)[:, None] * inv_freq

    def expand(fn):
        tr, tc = fn(ang_r), fn(ang_c)
        by_row = np.concatenate([tr, tr, np.zeros_like(tr), np.zeros_like(tr)], axis=-1)
        by_col = np.concatenate([np.zeros_like(tc), np.zeros_like(tc), tc, tc], axis=-1)
        full = jnp.asarray(by_row)[:, None, :] + jnp.asarray(by_col)[None, :, :]
        return full.reshape(n_tokens, DIFF_HEAD_DIM)

    return expand(np.cos), expand(np.sin)


def kernel(x, c, ctx, c_ctx, norm_w, ada_w, ada_b, gla_w_in, gla_gate_a1, gla_gate_a2, gla_gate_b,
           gla_gn_w, gla_w_out, diff_w_in, diff_qn_w, diff_kn_w, diff_lam, diff_subln_w, diff_w_out):
    b, seq, d = x.shape
    ctx_len = ctx.shape[1]
    assert ctx_len == GLA_BLOCK and seq % GLA_BLOCK == 0 and d == D_MODEL

    cond = jnp.zeros((ADA_ROWS, d), F32).at[:b].set(c).at[b].set(c_ctx)
    mod = _ada_call(cond, ada_w, ada_b)

    def mod_rows(layer, part):
        m = mod[layer, :, part * d:(part + 1) * d]
        lat = m[:b, None, :]
        cx = jnp.broadcast_to(m[b][None, None, :], (b, 1, d))
        return lat, cx

    shift, shift_c = mod_rows(0, 0)
    scale, scale_c = mod_rows(0, 1)
    gate, gate_c = mod_rows(0, 2)
    nw0 = norm_w[0].reshape(1, d)
    w_in0 = gla_w_in[0].astype(BF16)
    a1 = jnp.zeros((d, V7X_LANES), F32)
    a1 = a1.at[:, :GLA_GATE_RANK].set(gla_gate_a1[0, 0]).at[:, GLA_GATE_RANK:2 * GLA_GATE_RANK].set(
        gla_gate_a1[0, 1]).astype(BF16)
    a2 = jnp.zeros((V7X_LANES, 2 * GLA_KEY_DIM), F32)
    a2 = a2.at[:GLA_GATE_RANK, :GLA_KEY_DIM].set(gla_gate_a2[0, 0])
    a2 = a2.at[GLA_GATE_RANK:2 * GLA_GATE_RANK, GLA_KEY_DIM:].set(gla_gate_a2[0, 1]).astype(BF16)
    gb = gla_gate_b[0].reshape(1, 2 * GLA_KEY_DIM)
    gla_extras = [a1, a2, gb]
    gla_extra_specs = [
        pl.BlockSpec(a1.shape, lambda bi, i, j: (0, 0)),
        pl.BlockSpec(a2.shape, lambda bi, i, j: (0, 0)),
        pl.BlockSpec(gb.shape, lambda bi, i, j: (0, 0)),
    ]
    la_out = (2 * GLA_KEY_DIM, BF16)
    tm0, tn0 = GLA_INPROJ_TILE
    qkvg0_x, la0_x = _inproj_call(
        _gla_inproj_kernel, x, shift, scale, nw0, w_in0, gla_extras, gla_extra_specs,
        tm=tm0, tn=tn0, extra_out=la_out, name="gla_inproj_x")
    qkvg0_c, la0_c = _inproj_call(
        _gla_inproj_kernel, ctx, shift_c, scale_c, nw0, w_in0, gla_extras, gla_extra_specs,
        tm=CTX_INPROJ_TM, tn=tn0, extra_out=la_out, name="gla_inproj_ctx")

    o_f, o_b, oc_f, oc_b = _gla_scan_call(qkvg0_x, la0_x, qkvg0_c, la0_c)

    w_out0 = gla_w_out[0].astype(BF16)
    gn_w = gla_gn_w[0].reshape(1, GLA_DV)
    g_col_blk0 = (2 * GLA_KEY_DIM + GLA_VAL_DIM) // GLA_VAL_DIM
    x1 = _finish_call(x, gate, [o_f, o_b], qkvg0_x, g_col_blk0, w_out0, gn_w=gn_w,
                      name="gla_finish_x")
    ctx1 = _finish_call(ctx, gate_c, [oc_f, oc_b], qkvg0_c, g_col_blk0, w_out0, gn_w=gn_w,
                        name="gla_finish_ctx")

    lambda_init = 0.8 - 0.6 * math.exp(-0.3 * 1)
    shift, shift_c = mod_rows(1, 0)
    scale, scale_c = mod_rows(1, 1)
    gate, _ = mod_rows(1, 2)
    nw1 = norm_w[1].reshape(1, d)
    w_in1 = diff_w_in[0].astype(BF16)
    cos, sin = _axial_rope_tables(seq)
    q_scale = math.log2(math.e) * DIFF_HEAD_DIM ** -0.5
    tm1, tn1 = DIFF_INPROJ_TILE
    head_mats = _head_matrices()
    nrm_args = [diff_qn_w[0].reshape(1, DIFF_VAL_DIM), diff_kn_w[0].reshape(1, DIFF_VAL_DIM)]
    nrm_specs = [
        pl.BlockSpec((1, DIFF_VAL_DIM), lambda bi, i, j: (0, 0)),
        pl.BlockSpec((1, DIFF_VAL_DIM), lambda bi, i, j: (0, 0)),
    ]
    hm_spec = pl.BlockSpec(head_mats.shape, lambda bi, i, j: (0, 0, 0))
    (qkvg1_x,) = _inproj_call(
        functools.partial(_diff_inproj_kernel, rope=True, tn=tn1, q_scale=q_scale),
        x1, shift, scale, nw1, w_in1, nrm_args + [cos, sin, head_mats],
        nrm_specs + [pl.BlockSpec((tm1, DIFF_HEAD_DIM), lambda bi, i, j: (i, 0)),
                     pl.BlockSpec((tm1, DIFF_HEAD_DIM), lambda bi, i, j: (i, 0)), hm_spec],
        tm=tm1, tn=tn1, name="diff_inproj_x")
    (qkvg1_c,) = _inproj_call(
        functools.partial(_diff_inproj_kernel, rope=False, tn=tn1, q_scale=q_scale),
        ctx1, shift_c, scale_c, nw1, w_in1, nrm_args + [cos, sin, head_mats],
        nrm_specs + [pl.BlockSpec((CTX_INPROJ_TM, DIFF_HEAD_DIM), lambda bi, i, j: (0, 0)),
                     pl.BlockSpec((CTX_INPROJ_TM, DIFF_HEAD_DIM), lambda bi, i, j: (0, 0)), hm_spec],
        tm=CTX_INPROJ_TM, tn=tn1, name="diff_inproj_ctx")

    score_bound = q_scale * DIFF_HEAD_DIM * jnp.max(
        jnp.max(jnp.abs(diff_qn_w[0]), axis=-1) * jnp.max(jnp.abs(diff_kn_w[0]), axis=-1))
    attn = functools.partial(_attn_call, qkvg1_x, qkvg1_c, diff_lam[0], diff_subln_w[0], lambda_init)
    o_attn = lax.cond(score_bound <= ATTN_SCORE_BOUND,
                      functools.partial(attn, bounded=True), functools.partial(attn, bounded=False))

    w_out1 = diff_w_out[0].astype(BF16)
    g_col_blk1 = 3 * DIFF_WIDTH // DIFF_WIDTH
    return _finish_call(x1, gate, [o_attn], qkvg1_x, g_col_blk1, w_out1, name="diff_finish_x")
```

```python
import functools
import math

import jax
import jax.numpy as jnp
import numpy as np
from jax import lax
from jax.experimental import pallas as pl
from jax.experimental.pallas import tpu as pltpu

F32 = jnp.float32
BF16 = jnp.bfloat16

D_MODEL = 2048
DEPTH = 2
GRID_W = 64
EPS = 1e-6
GLA_HEADS = 4
GLA_KEY_DIM = D_MODEL // 2
GLA_VAL_DIM = D_MODEL
GLA_DK = GLA_KEY_DIM // GLA_HEADS
GLA_DV = GLA_VAL_DIM // GLA_HEADS
GLA_GATE_RANK = 16
GLA_GATE_NORM = 16.0
GLA_CHUNK = 64
GLA_IN_DIM = 2 * GLA_KEY_DIM + 2 * GLA_VAL_DIM
DIFF_HEAD_DIM = 128
DIFF_HEADS = D_MODEL // (2 * DIFF_HEAD_DIM)
DIFF_VAL_DIM = 2 * DIFF_HEAD_DIM
DIFF_WIDTH = DIFF_HEADS * DIFF_VAL_DIM
DIFF_IN_DIM = 4 * DIFF_WIDTH
ROPE_BASE = 10000.0

V7X_LANES = 128
V7X_SUBLANES = 8
V7X_MXU_DIM = 256
VMEM_LIMIT = 56 * 1024 * 1024

ADA_ROWS = V7X_SUBLANES
ADA_TN = 1024
GLA_BLOCK = 256
GLA_STEP_BLOCKS = 4
GLA_INPROJ_TILE = (1024, 1024)
DIFF_INPROJ_TILE = (1024, 1024)
FINISH_TM = 512
ATTN_TQ = 1024
ATTN_TK = 16384
ATTN_TQ_ONLINE = 512
ATTN_TK_ONLINE = 1024
ATTN_KEY_SUB = V7X_MXU_DIM
ATTN_SCORE_BOUND = 60.0


def _dot(a, b):
    return jnp.dot(a, b, preferred_element_type=F32)


def _dot_nt(a, b):
    return lax.dot_general(a, b, (((1,), (1,)), ((), ())), preferred_element_type=F32)


def _params(sem):
    return pltpu.CompilerParams(dimension_semantics=sem, vmem_limit_bytes=VMEM_LIMIT)


def _ada_kernel(c_ref, w_ref, b_ref, o_ref):
    c = c_ref[...]
    s = c * jax.nn.sigmoid(c)
    o_ref[...] = jnp.dot(s, w_ref[...], preferred_element_type=F32,
                         precision=lax.Precision.HIGHEST) + b_ref[...]


def _ada_call(cond, ada_w, ada_b):
    tn = ADA_TN
    n = ada_w.shape[-1]
    return pl.pallas_call(
        _ada_kernel,
        grid=(DEPTH, n // tn),
        in_specs=[
            pl.BlockSpec((ADA_ROWS, D_MODEL), lambda l, j: (0, 0)),
            pl.BlockSpec((None, D_MODEL, tn), lambda l, j: (l, 0, j)),
            pl.BlockSpec((None, 1, tn), lambda l, j: (l, 0, j)),
        ],
        out_specs=pl.BlockSpec((None, ADA_ROWS, tn), lambda l, j: (l, 0, j)),
        out_shape=jax.ShapeDtypeStruct((DEPTH, ADA_ROWS, n), F32),
        compiler_params=_params(("arbitrary", "arbitrary")),
        name="ada_mod",
    )(cond, ada_w, ada_b.reshape(DEPTH, 1, n))


def _normalize_into(x_ref, shift_ref, scale_ref, nw_ref, hn_ref):
    x = x_ref[...]
    ms = jnp.mean(x * x, axis=-1, keepdims=True)
    a = nw_ref[...] * (1.0 + scale_ref[...])
    hn_ref[...] = (x * lax.rsqrt(ms + EPS) * a + shift_ref[...]).astype(BF16)


def _gla_inproj_kernel(x_ref, shift_ref, scale_ref, nw_ref, w_ref, a1_ref, a2_ref, gb_ref,
                       out_ref, la_ref, hn_ref):
    j = pl.program_id(2)

    @pl.when(j == 0)
    def _():
        _normalize_into(x_ref, shift_ref, scale_ref, nw_ref, hn_ref)
        z = _dot(hn_ref[...], a1_ref[...])
        lg = _dot(z.astype(BF16), a2_ref[...]) + gb_ref[...]
        log_sig = jnp.minimum(lg, 0.0) - jnp.log(1.0 + jnp.exp(-jnp.abs(lg)))
        la_ref[...] = (log_sig * (1.0 / GLA_GATE_NORM)).astype(la_ref.dtype)

    out_ref[...] = _dot(hn_ref[...], w_ref[...]).astype(BF16)


def _head_matrices():
    qd = DIFF_HEAD_DIM // 4
    e = np.arange(qd)
    rot = np.zeros((DIFF_HEAD_DIM, DIFF_HEAD_DIM), np.float32)
    rot[qd + e, e] = -1.0
    rot[e, qd + e] = 1.0
    rot[3 * qd + e, 2 * qd + e] = -1.0
    rot[2 * qd + e, 3 * qd + e] = 1.0
    both_maps = np.eye(2, dtype=np.float32)
    mean = np.kron(both_maps, np.full((DIFF_HEAD_DIM, DIFF_HEAD_DIM), 1.0 / DIFF_HEAD_DIM, np.float32))
    return jnp.asarray(np.stack([mean, np.kron(both_maps, rot)]), BF16)


def _diff_inproj_kernel(x_ref, shift_ref, scale_ref, nw_ref, w_ref, qn_ref, kn_ref,
                        cos_ref, sin_ref, hm_ref, out_ref, hn_ref, *, rope, tn, q_scale):
    j = pl.program_id(2)
    tiles_per_part = DIFF_WIDTH // tn

    @pl.when(j == 0)
    def _():
        _normalize_into(x_ref, shift_ref, scale_ref, nw_ref, hn_ref)

    acc = _dot(hn_ref[...], w_ref[...])

    def head_norm(nrm_ref, scale):
        w2 = nrm_ref[...] * scale
        if rope:
            cos2 = jnp.concatenate([cos_ref[...], cos_ref[...]], axis=1)
            sin2 = jnp.concatenate([sin_ref[...], sin_ref[...]], axis=1)
        for head in range(tn // DIFF_VAL_DIM):
            cols = slice(head * DIFF_VAL_DIM, (head + 1) * DIFF_VAL_DIM)
            t2 = acc[:, cols]
            ms = _dot((t2 * t2).astype(BF16), hm_ref[0])
            u = t2 * w2
            if rope:
                u = u * cos2 + _dot(u.astype(BF16), hm_ref[1]) * sin2
            out_ref[:, cols] = (u * lax.rsqrt(ms + EPS)).astype(BF16)

    @pl.when(j < tiles_per_part)
    def _():
        head_norm(qn_ref, q_scale)

    @pl.when((j >= tiles_per_part) & (j < 2 * tiles_per_part))
    def _():
        head_norm(kn_ref, 1.0)

    @pl.when(j >= 2 * tiles_per_part)
    def _():
        out_ref[...] = acc.astype(BF16)


def _inproj_call(kernel, x, shift, scale, nw, w, extras, extra_specs, *, tm, tn,
                 extra_out=None, name):
    b, t, d = x.shape
    n = w.shape[-1]
    assert t % tm == 0 and n % tn == 0
    in_specs = [
        pl.BlockSpec((None, tm, d), lambda bi, i, j: (bi, i, 0)),
        pl.BlockSpec((None, 1, d), lambda bi, i, j: (bi, 0, 0)),
        pl.BlockSpec((None, 1, d), lambda bi, i, j: (bi, 0, 0)),
        pl.BlockSpec((1, d), lambda bi, i, j: (0, 0)),
        pl.BlockSpec((d, tn), lambda bi, i, j: (0, j)),
    ] + extra_specs
    out_specs = [pl.BlockSpec((None, tm, tn), lambda bi, i, j: (bi, i, j))]
    out_shape = [jax.ShapeDtypeStruct((b, t, n), BF16)]
    if extra_out is not None:
        width, dtype = extra_out
        out_specs.append(pl.BlockSpec((None, tm, width), lambda bi, i, j: (bi, i, 0)))
        out_shape.append(jax.ShapeDtypeStruct((b, t, width), dtype))
    return pl.pallas_call(
        kernel,
        grid=(b, t // tm, n // tn),
        in_specs=in_specs,
        out_specs=out_specs,
        out_shape=out_shape,
        scratch_shapes=[pltpu.VMEM((tm, d), BF16)],
        compiler_params=_params(("arbitrary", "arbitrary", "arbitrary")),
        name=name,
    )(x, shift, scale, nw, w, *extras)


def _gla_tri_matrices():
    idx = jnp.arange(GLA_BLOCK)
    i, j = idx[:, None], idx[None, :]
    same = (i // GLA_CHUNK) == (j // GLA_CHUNK)
    lower, upper = j <= i, j >= i
    return jnp.stack([jnp.stack([lower, lower & same]),
                      jnp.stack([upper, upper & same])]).astype(BF16)


def _dot_tn(a, b):
    return lax.dot_general(a, b, (((0,), (0,)), ((), ())), preferred_element_type=F32)


def _gla_direction(q_ref, k_ref, v_ref, la_ref, tri_ref, st_ref, o_ref, direction, rows):
    fwd = direction == 0
    cs = GLA_CHUNK
    n_sub = GLA_BLOCK // cs
    b = _dot(tri_ref[direction, 0], la_ref[rows, :])
    tot = b[GLA_BLOCK - 1:GLA_BLOCK] if fwd else b[0:1]

    def per_chunk_rows(row_of_chunk):
        return jnp.concatenate(
            [jnp.broadcast_to(b[row_of_chunk(c):row_of_chunk(c) + 1], (cs, GLA_DK))
             for c in range(n_sub)], axis=0)

    if fwd:
        mid = per_chunk_rows(lambda c: c * cs + cs // 2 - 1)
        edge = per_chunk_rows(lambda c: max(c * cs - 1, 0))
    else:
        mid = per_chunk_rows(lambda c: c * cs + cs // 2)
        edge = per_chunk_rows(lambda c: min((c + 1) * cs, GLA_BLOCK - 1))

    q = q_ref[rows, :].astype(F32) * (GLA_DK ** -0.5)
    k = k_ref[rows, :].astype(F32)
    v = v_ref[rows, :]
    st = st_ref[...]
    q_in = (q * jnp.exp(b)).astype(BF16)
    k_in = (k * jnp.exp(tot - b)).astype(BF16)
    q_d = (q * jnp.exp(b - mid)).astype(BF16)
    k_d = (k * jnp.exp(mid - b)).astype(BF16)
    q_o = (q * jnp.exp(b - edge)).astype(BF16)

    scores = jnp.where(tri_ref[direction, 1] > 0, _dot_nt(q_d, k_d), 0.0)
    tok = lax.broadcasted_iota(jnp.int32, (GLA_BLOCK, 1), 0)
    off = []
    for c in range(n_sub):
        lo, hi = c * cs, (c + 1) * cs
        first, last, ref_row = (0, lo, lo - 1) if fwd else (hi, GLA_BLOCK, hi)
        if last > first:
            expo = jnp.where((tok >= first) & (tok < last), b[ref_row:ref_row + 1] - b, -1e30)
            k_o = (k * jnp.exp(expo)).astype(BF16)
            off.append(_dot_nt(q_o[lo:hi], k_o))
        else:
            off.append(jnp.zeros((cs, GLA_BLOCK), F32))
    scores = scores + jnp.concatenate(off, axis=0)

    o = _dot_nt(q_in, st.astype(BF16)) + _dot(scores.astype(BF16), v)
    o_ref[rows, :] = o.astype(BF16)
    st_ref[...] = st * jnp.exp(tot) + _dot_tn(v, k_in)


def _gla_scan_kernel(qf_ref, kf_ref, vf_ref, laf_ref, qb_ref, kb_ref, vb_ref, lab_ref,
                     qc_ref, kc_ref, vc_ref, lacf_ref, lacb_ref, tri_ref,
                     of_ref, ob_ref, ocf_ref, ocb_ref, sf_ref, sb_ref):
    t = pl.program_id(2)

    @pl.when(t == 0)
    def _():
        sf_ref[...] = jnp.zeros_like(sf_ref)
        sb_ref[...] = jnp.zeros_like(sb_ref)
        whole = slice(0, GLA_BLOCK)
        _gla_direction(qc_ref, kc_ref, vc_ref, lacf_ref, tri_ref, sf_ref, ocf_ref, 0, whole)
        _gla_direction(qc_ref, kc_ref, vc_ref, lacb_ref, tri_ref, sb_ref, ocb_ref, 1, whole)

    @pl.when(t > 0)
    def _():
        blocks = [slice(s * GLA_BLOCK, (s + 1) * GLA_BLOCK) for s in range(GLA_STEP_BLOCKS)]
        for rows_f, rows_b in zip(blocks, reversed(blocks)):
            _gla_direction(qf_ref, kf_ref, vf_ref, laf_ref, tri_ref, sf_ref, of_ref, 0, rows_f)
            _gla_direction(qb_ref, kb_ref, vb_ref, lab_ref, tri_ref, sb_ref, ob_ref, 1, rows_b)


def _gla_scan_call(qkvg_x, la_x, qkvg_c, la_c):
    b, seq, _ = qkvg_x.shape
    step_rows = GLA_BLOCK * GLA_STEP_BLOCKS
    nb = seq // step_rows
    assert qkvg_c.shape[1] == GLA_BLOCK and seq % step_rows == 0
    k_off = GLA_KEY_DIM // GLA_DK
    v_off = 2 * GLA_KEY_DIM // GLA_DV

    def pos_f(t):
        return jnp.maximum(t - 1, 0)

    def pos_b(t):
        return nb - jnp.maximum(t, 1)

    def pos_c(t):
        return 0

    def specs(pos, rows, la_col_off, with_la=True):
        out = [
            pl.BlockSpec((None, rows, GLA_DK), lambda bi, h, t: (bi, pos(t), h)),
            pl.BlockSpec((None, rows, GLA_DK), lambda bi, h, t: (bi, pos(t), k_off + h)),
            pl.BlockSpec((None, rows, GLA_DV), lambda bi, h, t: (bi, pos(t), v_off + h)),
        ]
        if with_la:
            out.append(la_spec(pos, rows, la_col_off))
        return out

    def la_spec(pos, rows, la_col_off):
        return pl.BlockSpec((None, rows, GLA_DK), lambda bi, h, t: (bi, pos(t), la_col_off + h))

    def o_spec(pos, rows):
        return pl.BlockSpec((None, rows, GLA_DV), lambda bi, h, t: (bi, pos(t), h))

    tri = _gla_tri_matrices()
    in_specs = (specs(pos_f, step_rows, 0) + specs(pos_b, step_rows, GLA_HEADS)
                + specs(pos_c, GLA_BLOCK, 0, with_la=False)
                + [la_spec(pos_c, GLA_BLOCK, 0), la_spec(pos_c, GLA_BLOCK, GLA_HEADS),
                   pl.BlockSpec(tri.shape, lambda bi, h, t: (0, 0, 0, 0))])
    ox_shape = jax.ShapeDtypeStruct((b, seq, GLA_VAL_DIM), BF16)
    oc_shape = jax.ShapeDtypeStruct((b, GLA_BLOCK, GLA_VAL_DIM), BF16)
    return pl.pallas_call(
        _gla_scan_kernel,
        grid=(b, GLA_HEADS, nb + 1),
        in_specs=in_specs,
        out_specs=[o_spec(pos_f, step_rows), o_spec(pos_b, step_rows),
                   o_spec(pos_c, GLA_BLOCK), o_spec(pos_c, GLA_BLOCK)],
        out_shape=[ox_shape, ox_shape, oc_shape, oc_shape],
        scratch_shapes=[pltpu.VMEM((GLA_DV, GLA_DK), F32), pltpu.VMEM((GLA_DV, GLA_DK), F32)],
        compiler_params=_params(("arbitrary", "arbitrary", "arbitrary")),
        name="gla_scan",
    )(qkvg_x, qkvg_x, qkvg_x, la_x, qkvg_x, qkvg_x, qkvg_x, la_x,
      qkvg_c, qkvg_c, qkvg_c, la_c, la_c, tri)


def _finish_kernel(*refs, head_norm):
    if head_norm:
        x_ref, gate_ref, o1_ref, o2_ref, g_ref, gn_ref, w_ref, out_ref = refs
        o = o1_ref[...].astype(F32) + o2_ref[...].astype(F32)
        parts = []
        for h in range(GLA_HEADS):
            oh = o[:, h * GLA_DV:(h + 1) * GLA_DV]
            ms = jnp.mean(oh * oh, axis=-1, keepdims=True)
            parts.append(oh * lax.rsqrt(ms + EPS) * gn_ref[...])
        o = jnp.concatenate(parts, axis=-1)
    else:
        x_ref, gate_ref, o1_ref, g_ref, w_ref, out_ref = refs
        o = o1_ref[...].astype(F32)
    half_g = 0.5 * g_ref[...].astype(F32)
    silu_g = half_g * (1.0 + jnp.tanh(half_g))
    y_in = (o * silu_g).astype(BF16)
    y = _dot(y_in, w_ref[...])
    out_ref[...] = x_ref[...] + gate_ref[...] * y


def _finish_call(x, gate, o_list, g_arr, g_col_blk, w, *, gn_w=None, name):
    b, t, d = x.shape
    tm = min(FINISH_TM, t)
    assert t % tm == 0
    width = w.shape[0]
    in_specs = [
        pl.BlockSpec((None, tm, d), lambda bi, i: (bi, i, 0)),
        pl.BlockSpec((None, 1, d), lambda bi, i: (bi, 0, 0)),
    ]
    args = [x, gate]
    for o in o_list:
        in_specs.append(pl.BlockSpec((None, tm, width), lambda bi, i: (bi, i, 0)))
        args.append(o)
    in_specs.append(pl.BlockSpec((None, tm, width), lambda bi, i: (bi, i, g_col_blk)))
    args.append(g_arr)
    if gn_w is not None:
        in_specs.append(pl.BlockSpec((1, gn_w.shape[-1]), lambda bi, i: (0, 0)))
        args.append(gn_w)
    in_specs.append(pl.BlockSpec((width, d), lambda bi, i: (0, 0)))
    args.append(w)
    return pl.pallas_call(
        functools.partial(_finish_kernel, head_norm=gn_w is not None),
        grid=(b, t // tm),
        in_specs=in_specs,
        out_specs=pl.BlockSpec((None, tm, d), lambda bi, i: (bi, i, 0)),
        out_shape=jax.ShapeDtypeStruct((b, t, d), F32),
        compiler_params=_params(("arbitrary", "arbitrary")),
        name=name,
    )(*args)


def _attn_finalize(acc1, l1, acc2, l2, lam_ref, sw_ref, o_ref, lambda_init):
    lv = lam_ref[...]
    lam = (jnp.exp(jnp.sum(lv[0:1] * lv[1:2], axis=-1, keepdims=True))
           - jnp.exp(jnp.sum(lv[2:3] * lv[3:4], axis=-1, keepdims=True)) + lambda_init)
    o = acc1 / l1 - lam * (acc2 / l2)
    ms = jnp.mean(o * o, axis=-1, keepdims=True)
    o = o * lax.rsqrt(ms + EPS) * sw_ref[...] * (1.0 - lambda_init)
    o_ref[...] = o.astype(BF16)


def _attn_online_kernel(q1_ref, q2_ref, k1_ref, k2_ref, v_ref, k1c_ref, k2c_ref, vc_ref,
                        lam_ref, sw_ref, o_ref, m_ref, l_ref, acc_ref, *, tk, lambda_init):
    n_keys = v_ref.shape[0]
    m_ref[...] = jnp.full_like(m_ref, -1e30)
    l_ref[...] = jnp.zeros_like(l_ref)
    acc_ref[...] = jnp.zeros_like(acc_ref)

    def update(mp, q_ref, k_blk, v_blk):
        s = _dot_nt(q_ref[...], k_blk)
        m_old = m_ref[mp]
        m_new = jnp.maximum(m_old, jnp.max(s, axis=1, keepdims=True))
        alpha = jnp.exp2(m_old - m_new)
        p = jnp.exp2(s - m_new)
        l_ref[mp] = alpha * l_ref[mp] + jnp.sum(p, axis=1, keepdims=True)
        acc_ref[mp] = alpha * acc_ref[mp] + _dot(p.astype(BF16), v_blk)
        m_ref[mp] = m_new

    def body(c, carry):
        off = pl.multiple_of(c * tk, tk)
        v_blk = v_ref[pl.ds(off, tk), :]
        update(0, q1_ref, k1_ref[pl.ds(off, tk), :], v_blk)
        update(1, q2_ref, k2_ref[pl.ds(off, tk), :], v_blk)
        return carry

    lax.fori_loop(0, n_keys // tk, body, 0)
    update(0, q1_ref, k1c_ref[...], vc_ref[...])
    update(1, q2_ref, k2c_ref[...], vc_ref[...])
    _attn_finalize(acc_ref[0], l_ref[0], acc_ref[1], l_ref[1], lam_ref, sw_ref, o_ref, lambda_init)


def _attn_bounded_kernel(q1_ref, q2_ref, k1_ref, k2_ref, v_ref, k1c_ref, k2c_ref, vc_ref,
                         lam_ref, sw_ref, o_ref, l_ref, acc_ref, *, tk, lambda_init):
    n_keys = v_ref.shape[0]
    l_ref[...] = jnp.zeros_like(l_ref)
    acc_ref[...] = jnp.zeros_like(acc_ref)

    def sub_block(q, k_blk, v_blk):
        s = _dot_nt(q, k_blk)
        p = jnp.exp2(s)
        part = p[:, :V7X_LANES] + p[:, V7X_LANES:]
        return part, _dot(p.astype(BF16), v_blk)

    def body(c, carry):
        for mp, (q_ref, k_ref) in enumerate(((q1_ref, k1_ref), (q2_ref, k2_ref))):
            q = q_ref[...]
            acc = None
            lsum = None
            for sb in range(tk // ATTN_KEY_SUB):
                off = pl.multiple_of(c * tk + sb * ATTN_KEY_SUB, ATTN_KEY_SUB)
                part, d = sub_block(q, k_ref[pl.ds(off, ATTN_KEY_SUB), :],
                                    v_ref[pl.ds(off, ATTN_KEY_SUB), :])
                lsum = part if lsum is None else lsum + part
                acc = d if acc is None else acc + d
            l_ref[mp] += lsum
            acc_ref[mp] += acc
        return carry

    lax.fori_loop(0, n_keys // tk, body, 0)

    for mp, (q_ref, kc_ref) in enumerate(((q1_ref, k1c_ref), (q2_ref, k2c_ref))):
        part, d = sub_block(q_ref[...], kc_ref[...], vc_ref[...])
        l_ref[mp] += part
        acc_ref[mp] += d

    l1 = jnp.sum(l_ref[0], axis=-1, keepdims=True)
    l2 = jnp.sum(l_ref[1], axis=-1, keepdims=True)
    _attn_finalize(acc_ref[0], l1, acc_ref[1], l2, lam_ref, sw_ref, o_ref, lambda_init)


def _attn_call(qkvg_x, qkvg_c, lam_v, subln_w, lambda_init, *, bounded):
    b, seq, _ = qkvg_x.shape
    n_ctx = qkvg_c.shape[1]
    tq, tk = (ATTN_TQ, ATTN_TK) if bounded else (ATTN_TQ_ONLINE, ATTN_TK_ONLINE)
    assert seq % tq == 0 and seq % tk == 0 and tk % ATTN_KEY_SUB == 0 and n_ctx == ATTN_KEY_SUB
    d = DIFF_HEAD_DIM
    k_off = DIFF_WIDTH // d
    v_off = 2 * DIFF_WIDTH // DIFF_VAL_DIM
    in_specs = [
        pl.BlockSpec((None, tq, d), lambda bi, h, i: (bi, i, 2 * h)),
        pl.BlockSpec((None, tq, d), lambda bi, h, i: (bi, i, 2 * h + 1)),
        pl.BlockSpec((None, seq, d), lambda bi, h, i: (bi, 0, k_off + 2 * h)),
        pl.BlockSpec((None, seq, d), lambda bi, h, i: (bi, 0, k_off + 2 * h + 1)),
        pl.BlockSpec((None, seq, DIFF_VAL_DIM), lambda bi, h, i: (bi, 0, v_off + h)),
        pl.BlockSpec((None, n_ctx, d), lambda bi, h, i: (bi, 0, k_off + 2 * h)),
        pl.BlockSpec((None, n_ctx, d), lambda bi, h, i: (bi, 0, k_off + 2 * h + 1)),
        pl.BlockSpec((None, n_ctx, DIFF_VAL_DIM), lambda bi, h, i: (bi, 0, v_off + h)),
        pl.BlockSpec(lam_v.shape, lambda bi, h, i: (0, 0)),
        pl.BlockSpec((1, DIFF_VAL_DIM), lambda bi, h, i: (0, 0)),
    ]
    if bounded:
        body = functools.partial(_attn_bounded_kernel, tk=tk, lambda_init=lambda_init)
        scratch = [pltpu.VMEM((2, tq, V7X_LANES), F32), pltpu.VMEM((2, tq, DIFF_VAL_DIM), F32)]
    else:
        body = functools.partial(_attn_online_kernel, tk=tk, lambda_init=lambda_init)
        scratch = [pltpu.VMEM((2, tq, 1), F32), pltpu.VMEM((2, tq, 1), F32),
                   pltpu.VMEM((2, tq, DIFF_VAL_DIM), F32)]
    return pl.pallas_call(
        body,
        grid=(b, DIFF_HEADS, seq // tq),
        in_specs=in_specs,
        out_specs=pl.BlockSpec((None, tq, DIFF_VAL_DIM), lambda bi, h, i: (bi, i, h)),
        out_shape=jax.ShapeDtypeStruct((b, seq, DIFF_WIDTH), BF16),
        scratch_shapes=scratch,
        compiler_params=_params(("arbitrary", "arbitrary", "arbitrary")),
        name="diff_attn_bounded" if bounded else "diff_attn_online",
    )(qkvg_x, qkvg_x, qkvg_x, qkvg_x, qkvg_x, qkvg_c, qkvg_c, qkvg_c,
      lam_v, subln_w.reshape(1, DIFF_VAL_DIM))


def _axial_rope_tables(n_tokens):
    rows_n = n_tokens // GRID_W
    half = DIFF_HEAD_DIM // 2
    inv_freq = (ROPE_BASE ** (-np.arange(0, half, 2, dtype=np.float32) / half)).astype(np.float32)
    ang_r = np.arange(rows_n, dtype=np.float32)[:, None] * inv_freq
    ang_c = np.arange(GRID_W, dtype=np.float32)[:, None] * inv_freq

    def expand(fn):
        tr, tc = fn(ang_r), fn(ang_c)
        by_row = np.concatenate([tr, tr, np.zeros_like(tr), np.zeros_like(tr)], axis=-1)
        by_col = np.concatenate([np.zeros_like(tc), np.zeros_like(tc), tc, tc], axis=-1)
        full = jnp.asarray(by_row)[:, None, :] + jnp.asarray(by_col)[None, :, :]
        return full.reshape(n_tokens, DIFF_HEAD_DIM)

    return expand(np.cos), expand(np.sin)


def kernel(x, c, ctx, c_ctx, norm_w, ada_w, ada_b, gla_w_in, gla_gate_a1, gla_gate_a2, gla_gate_b,
           gla_gn_w, gla_w_out, diff_w_in, diff_qn_w, diff_kn_w, diff_lam, diff_subln_w, diff_w_out):
    b, seq, d = x.shape
    ctx_len = ctx.shape[1]
    assert ctx_len == GLA_BLOCK and seq % GLA_BLOCK == 0 and d == D_MODEL

    cond = jnp.zeros((ADA_ROWS, d), F32).at[:b].set(c).at[b].set(c_ctx)
    mod = _ada_call(cond, ada_w, ada_b)

    def mod_rows(layer, part):
        m = mod[layer, :, part * d:(part + 1) * d]
        lat = m[:b, None, :]
        return lat, m[b][None, None, :]

    shift, shift_c = mod_rows(0, 0)
    scale, scale_c = mod_rows(0, 1)
    gate, gate_c = mod_rows(0, 2)
    nw0 = norm_w[0].reshape(1, d)
    w_in0 = gla_w_in[0].astype(BF16)
    a1 = jnp.zeros((d, V7X_LANES), F32)
    a1 = a1.at[:, :GLA_GATE_RANK].set(gla_gate_a1[0, 0]).at[:, GLA_GATE_RANK:2 * GLA_GATE_RANK].set(
        gla_gate_a1[0, 1]).astype(BF16)
    a2 = jnp.zeros((V7X_LANES, 2 * GLA_KEY_DIM), F32)
    a2 = a2.at[:GLA_GATE_RANK, :GLA_KEY_DIM].set(gla_gate_a2[0, 0])
    a2 = a2.at[GLA_GATE_RANK:2 * GLA_GATE_RANK, GLA_KEY_DIM:].set(gla_gate_a2[0, 1]).astype(BF16)
    gb = gla_gate_b[0].reshape(1, 2 * GLA_KEY_DIM)
    gla_extras = [a1, a2, gb]
    gla_extra_specs = [
        pl.BlockSpec(a1.shape, lambda bi, i, j: (0, 0)),
        pl.BlockSpec(a2.shape, lambda bi, i, j: (0, 0)),
        pl.BlockSpec(gb.shape, lambda bi, i, j: (0, 0)),
    ]
    la_out = (2 * GLA_KEY_DIM, BF16)
    tm0, tn0 = GLA_INPROJ_TILE
    qkvg0_x, la0_x = _inproj_call(
        _gla_inproj_kernel, x, shift, scale, nw0, w_in0, gla_extras, gla_extra_specs,
        tm=tm0, tn=tn0, extra_out=la_out, name="gla_inproj_x")
    ctx_rows = b * ctx_len
    ctx_m = ctx.reshape(1, ctx_rows, d)

    def per_batch(a):
        return a.reshape(b, ctx_len, a.shape[-1])

    def merged(a):
        return a.reshape(1, ctx_rows, a.shape[-1])

    qkvg0_c, la0_c = _inproj_call(
        _gla_inproj_kernel, ctx_m, shift_c, scale_c, nw0, w_in0, gla_extras, gla_extra_specs,
        tm=ctx_rows, tn=tn0, extra_out=la_out, name="gla_inproj_ctx")

    o_f, o_b, oc_f, oc_b = _gla_scan_call(qkvg0_x, la0_x, per_batch(qkvg0_c), per_batch(la0_c))

    w_out0 = gla_w_out[0].astype(BF16)
    gn_w = gla_gn_w[0].reshape(1, GLA_DV)
    g_col_blk0 = (2 * GLA_KEY_DIM + GLA_VAL_DIM) // GLA_VAL_DIM
    x1 = _finish_call(x, gate, [o_f, o_b], qkvg0_x, g_col_blk0, w_out0, gn_w=gn_w,
                      name="gla_finish_x")
    ctx1 = _finish_call(ctx_m, gate_c, [merged(oc_f), merged(oc_b)], qkvg0_c, g_col_blk0, w_out0,
                        gn_w=gn_w, name="gla_finish_ctx")

    lambda_init = 0.8 - 0.6 * math.exp(-0.3 * 1)
    shift, shift_c = mod_rows(1, 0)
    scale, scale_c = mod_rows(1, 1)
    gate, _ = mod_rows(1, 2)
    nw1 = norm_w[1].reshape(1, d)
    w_in1 = diff_w_in[0].astype(BF16)
    cos, sin = _axial_rope_tables(seq)
    q_scale = math.log2(math.e) * DIFF_HEAD_DIM ** -0.5
    tm1, tn1 = DIFF_INPROJ_TILE
    head_mats = _head_matrices()
    nrm_args = [diff_qn_w[0].reshape(1, DIFF_VAL_DIM), diff_kn_w[0].reshape(1, DIFF_VAL_DIM)]
    nrm_specs = [
        pl.BlockSpec((1, DIFF_VAL_DIM), lambda bi, i, j: (0, 0)),
        pl.BlockSpec((1, DIFF_VAL_DIM), lambda bi, i, j: (0, 0)),
    ]
    hm_spec = pl.BlockSpec(head_mats.shape, lambda bi, i, j: (0, 0, 0))
    (qkvg1_x,) = _inproj_call(
        functools.partial(_diff_inproj_kernel, rope=True, tn=tn1, q_scale=q_scale),
        x1, shift, scale, nw1, w_in1, nrm_args + [cos, sin, head_mats],
        nrm_specs + [pl.BlockSpec((tm1, DIFF_HEAD_DIM), lambda bi, i, j: (i, 0)),
                     pl.BlockSpec((tm1, DIFF_HEAD_DIM), lambda bi, i, j: (i, 0)), hm_spec],
        tm=tm1, tn=tn1, name="diff_inproj_x")
    (qkvg1_c,) = _inproj_call(
        functools.partial(_diff_inproj_kernel, rope=False, tn=tn1, q_scale=q_scale),
        ctx1, shift_c, scale_c, nw1, w_in1, nrm_args + [cos, sin, head_mats],
        nrm_specs + [pl.BlockSpec((ctx_rows, DIFF_HEAD_DIM), lambda bi, i, j: (0, 0)),
                     pl.BlockSpec((ctx_rows, DIFF_HEAD_DIM), lambda bi, i, j: (0, 0)), hm_spec],
        tm=ctx_rows, tn=tn1, name="diff_inproj_ctx")
    qkvg1_c = per_batch(qkvg1_c)

    score_bound = q_scale * DIFF_HEAD_DIM * jnp.max(
        jnp.max(jnp.abs(diff_qn_w[0]), axis=-1) * jnp.max(jnp.abs(diff_kn_w[0]), axis=-1))
    attn = functools.partial(_attn_call, qkvg1_x, qkvg1_c, diff_lam[0], diff_subln_w[0], lambda_init)
    o_attn = lax.cond(score_bound <= ATTN_SCORE_BOUND,
                      functools.partial(attn, bounded=True), functools.partial(attn, bounded=False))

    w_out1 = diff_w_out[0].astype(BF16)
    g_col_blk1 = 3 * DIFF_WIDTH // DIFF_WIDTH
    return _finish_call(x1, gate, [o_attn], qkvg1_x, g_col_blk1, w_out1, name="diff_finish_x")
```

```python
import functools
import math

import jax
import jax.numpy as jnp
import numpy as np
from jax import lax
from jax.experimental import pallas as pl
from jax.experimental.pallas import tpu as pltpu

F32 = jnp.float32
BF16 = jnp.bfloat16

D_MODEL = 2048
DEPTH = 2
GRID_W = 64
EPS = 1e-6
GLA_HEADS = 4
GLA_KEY_DIM = D_MODEL // 2
GLA_VAL_DIM = D_MODEL
GLA_DK = GLA_KEY_DIM // GLA_HEADS
GLA_DV = GLA_VAL_DIM // GLA_HEADS
GLA_GATE_RANK = 16
GLA_GATE_NORM = 16.0
GLA_CHUNK = 64
GLA_IN_DIM = 2 * GLA_KEY_DIM + 2 * GLA_VAL_DIM
DIFF_HEAD_DIM = 128
DIFF_HEADS = D_MODEL // (2 * DIFF_HEAD_DIM)
DIFF_VAL_DIM = 2 * DIFF_HEAD_DIM
DIFF_WIDTH = DIFF_HEADS * DIFF_VAL_DIM
DIFF_IN_DIM = 4 * DIFF_WIDTH
ROPE_BASE = 10000.0

V7X_LANES = 128
V7X_SUBLANES = 8
V7X_MXU_DIM = 256
VMEM_LIMIT = 56 * 1024 * 1024

ADA_ROWS = V7X_SUBLANES
ADA_TN = 1024
GLA_BLOCK = 256
GLA_STEP_BLOCKS = 4
GLA_INPROJ_TILE = (1024, 1024)
DIFF_INPROJ_TILE = (1024, 1024)
FINISH_TM = 512
ATTN_TQ = 1024
ATTN_TK = 8192
ATTN_TQ_ONLINE = 512
ATTN_TK_ONLINE = 1024
ATTN_KEY_SUB = V7X_MXU_DIM
ATTN_SCORE_BOUND = 60.0


def _dot(a, b):
    return jnp.dot(a, b, preferred_element_type=F32)


def _dot_nt(a, b):
    return lax.dot_general(a, b, (((1,), (1,)), ((), ())), preferred_element_type=F32)


def _params(sem):
    return pltpu.CompilerParams(dimension_semantics=sem, vmem_limit_bytes=VMEM_LIMIT)


def _ada_kernel(c_ref, w_ref, b_ref, o_ref):
    c = c_ref[...]
    s = c * jax.nn.sigmoid(c)
    o_ref[...] = jnp.dot(s, w_ref[...], preferred_element_type=F32,
                         precision=lax.Precision.HIGHEST) + b_ref[...]


def _ada_call(cond, ada_w, ada_b):
    tn = ADA_TN
    n = ada_w.shape[-1]
    return pl.pallas_call(
        _ada_kernel,
        grid=(DEPTH, n // tn),
        in_specs=[
            pl.BlockSpec((ADA_ROWS, D_MODEL), lambda l, j: (0, 0)),
            pl.BlockSpec((None, D_MODEL, tn), lambda l, j: (l, 0, j)),
            pl.BlockSpec((None, 1, tn), lambda l, j: (l, 0, j)),
        ],
        out_specs=pl.BlockSpec((None, ADA_ROWS, tn), lambda l, j: (l, 0, j)),
        out_shape=jax.ShapeDtypeStruct((DEPTH, ADA_ROWS, n), F32),
        compiler_params=_params(("arbitrary", "arbitrary")),
        name="ada_mod",
    )(cond, ada_w, ada_b.reshape(DEPTH, 1, n))


def _normalize_into(x_ref, shift_ref, scale_ref, nw_ref, hn_ref):
    x = x_ref[...]
    ms = jnp.mean(x * x, axis=-1, keepdims=True)
    a = nw_ref[...] * (1.0 + scale_ref[...])
    hn_ref[...] = (x * lax.rsqrt(ms + EPS) * a + shift_ref[...]).astype(BF16)


def _gla_inproj_kernel(x_ref, shift_ref, scale_ref, nw_ref, w_ref, a1_ref, a2_ref, gb_ref,
                       out_ref, la_ref, hn_ref):
    j = pl.program_id(2)

    @pl.when(j == 0)
    def _():
        _normalize_into(x_ref, shift_ref, scale_ref, nw_ref, hn_ref)
        z = _dot(hn_ref[...], a1_ref[...])
        lg = _dot(z.astype(BF16), a2_ref[...]) + gb_ref[...]
        log_sig = jnp.minimum(lg, 0.0) - jnp.log(1.0 + jnp.exp(-jnp.abs(lg)))
        la_ref[...] = (log_sig * (1.0 / GLA_GATE_NORM)).astype(la_ref.dtype)

    out_ref[...] = _dot(hn_ref[...], w_ref[...]).astype(BF16)


def _head_matrices():
    qd = DIFF_HEAD_DIM // 4
    e = np.arange(qd)
    rot = np.zeros((DIFF_HEAD_DIM, DIFF_HEAD_DIM), np.float32)
    rot[qd + e, e] = -1.0
    rot[e, qd + e] = 1.0
    rot[3 * qd + e, 2 * qd + e] = -1.0
    rot[2 * qd + e, 3 * qd + e] = 1.0
    both_maps = np.eye(2, dtype=np.float32)
    mean = np.kron(both_maps, np.full((DIFF_HEAD_DIM, DIFF_HEAD_DIM), 1.0 / DIFF_HEAD_DIM, np.float32))
    return jnp.asarray(np.stack([mean, np.kron(both_maps, rot)]), BF16)


def _diff_inproj_kernel(x_ref, shift_ref, scale_ref, nw_ref, w_ref, qn_ref, kn_ref,
                        cos_ref, sin_ref, hm_ref, out_ref, hn_ref, *, rope, tn, q_scale):
    j = pl.program_id(2)
    tiles_per_part = DIFF_WIDTH // tn

    @pl.when(j == 0)
    def _():
        _normalize_into(x_ref, shift_ref, scale_ref, nw_ref, hn_ref)

    acc = _dot(hn_ref[...], w_ref[...])

    def head_norm(nrm_ref, scale):
        w2 = nrm_ref[...] * scale
        if rope:
            cos2 = jnp.concatenate([cos_ref[...], cos_ref[...]], axis=1)
            sin2 = jnp.concatenate([sin_ref[...], sin_ref[...]], axis=1)
        for head in range(tn // DIFF_VAL_DIM):
            cols = slice(head * DIFF_VAL_DIM, (head + 1) * DIFF_VAL_DIM)
            t2 = acc[:, cols]
            ms = _dot((t2 * t2).astype(BF16), hm_ref[0])
            u = t2 * w2
            if rope:
                u = u * cos2 + _dot(u.astype(BF16), hm_ref[1]) * sin2
            out_ref[:, cols] = (u * lax.rsqrt(ms + EPS)).astype(BF16)

    @pl.when(j < tiles_per_part)
    def _():
        head_norm(qn_ref, q_scale)

    @pl.when((j >= tiles_per_part) & (j < 2 * tiles_per_part))
    def _():
        head_norm(kn_ref, 1.0)

    @pl.when(j >= 2 * tiles_per_part)
    def _():
        out_ref[...] = acc.astype(BF16)


def _inproj_call(kernel, x, shift, scale, nw, w, extras, extra_specs, *, tm, tn,
                 extra_out=None, name):
    b, t, d = x.shape
    n = w.shape[-1]
    assert t % tm == 0 and n % tn == 0
    in_specs = [
        pl.BlockSpec((None, tm, d), lambda bi, i, j: (bi, i, 0)),
        pl.BlockSpec((None, 1, d), lambda bi, i, j: (bi, 0, 0)),
        pl.BlockSpec((None, 1, d), lambda bi, i, j: (bi, 0, 0)),
        pl.BlockSpec((1, d), lambda bi, i, j: (0, 0)),
        pl.BlockSpec((d, tn), lambda bi, i, j: (0, j)),
    ] + extra_specs
    out_specs = [pl.BlockSpec((None, tm, tn), lambda bi, i, j: (bi, i, j))]
    out_shape = [jax.ShapeDtypeStruct((b, t, n), BF16)]
    if extra_out is not None:
        width, dtype = extra_out
        out_specs.append(pl.BlockSpec((None, tm, width), lambda bi, i, j: (bi, i, 0)))
        out_shape.append(jax.ShapeDtypeStruct((b, t, width), dtype))
    return pl.pallas_call(
        kernel,
        grid=(b, t // tm, n // tn),
        in_specs=in_specs,
        out_specs=out_specs,
        out_shape=out_shape,
        scratch_shapes=[pltpu.VMEM((tm, d), BF16)],
        compiler_params=_params(("arbitrary", "arbitrary", "arbitrary")),
        name=name,
    )(x, shift, scale, nw, w, *extras)


def _gla_tri_matrices():
    idx = jnp.arange(GLA_BLOCK)
    i, j = idx[:, None], idx[None, :]
    same = (i // GLA_CHUNK) == (j // GLA_CHUNK)
    lower, upper = j <= i, j >= i
    return jnp.stack([jnp.stack([lower, lower & same]),
                      jnp.stack([upper, upper & same])]).astype(BF16)


def _dot_tn(a, b):
    return lax.dot_general(a, b, (((0,), (0,)), ((), ())), preferred_element_type=F32)


def _gla_direction(q_ref, k_ref, v_ref, la_ref, tri_ref, st_ref, o_ref, direction, rows):
    fwd = direction == 0
    cs = GLA_CHUNK
    n_sub = GLA_BLOCK // cs
    b = _dot(tri_ref[direction, 0], la_ref[rows, :])
    tot = b[GLA_BLOCK - 1:GLA_BLOCK] if fwd else b[0:1]

    def per_chunk_rows(row_of_chunk):
        return jnp.concatenate(
            [jnp.broadcast_to(b[row_of_chunk(c):row_of_chunk(c) + 1], (cs, GLA_DK))
             for c in range(n_sub)], axis=0)

    if fwd:
        mid = per_chunk_rows(lambda c: c * cs + cs // 2 - 1)
        edge = per_chunk_rows(lambda c: max(c * cs - 1, 0))
    else:
        mid = per_chunk_rows(lambda c: c * cs + cs // 2)
        edge = per_chunk_rows(lambda c: min((c + 1) * cs, GLA_BLOCK - 1))

    q = q_ref[rows, :].astype(F32) * (GLA_DK ** -0.5)
    k = k_ref[rows, :].astype(F32)
    v = v_ref[rows, :]
    st = st_ref[...]
    q_in = (q * jnp.exp(b)).astype(BF16)
    k_in = (k * jnp.exp(tot - b)).astype(BF16)
    q_d = (q * jnp.exp(b - mid)).astype(BF16)
    k_d = (k * jnp.exp(mid - b)).astype(BF16)
    q_o = (q * jnp.exp(b - edge)).astype(BF16)

    scores = jnp.where(tri_ref[direction, 1] > 0, _dot_nt(q_d, k_d), 0.0)
    tok = lax.broadcasted_iota(jnp.int32, (GLA_BLOCK, 1), 0)
    off = []
    for c in range(n_sub):
        lo, hi = c * cs, (c + 1) * cs
        first, last, ref_row = (0, lo, lo - 1) if fwd else (hi, GLA_BLOCK, hi)
        if last > first:
            expo = jnp.where((tok >= first) & (tok < last), b[ref_row:ref_row + 1] - b, -1e30)
            k_o = (k * jnp.exp(expo)).astype(BF16)
            off.append(_dot_nt(q_o[lo:hi], k_o))
        else:
            off.append(jnp.zeros((cs, GLA_BLOCK), F32))
    scores = scores + jnp.concatenate(off, axis=0)

    o = _dot_nt(q_in, st.astype(BF16)) + _dot(scores.astype(BF16), v)
    o_ref[rows, :] = o.astype(BF16)
    st_ref[...] = st * jnp.exp(tot) + _dot_tn(v, k_in)


def _gla_scan_kernel(qf_ref, kf_ref, vf_ref, laf_ref, qb_ref, kb_ref, vb_ref, lab_ref,
                     qc_ref, kc_ref, vc_ref, lacf_ref, lacb_ref, tri_ref,
                     of_ref, ob_ref, ocf_ref, ocb_ref, sf_ref, sb_ref):
    t = pl.program_id(2)

    @pl.when(t == 0)
    def _():
        sf_ref[...] = jnp.zeros_like(sf_ref)
        sb_ref[...] = jnp.zeros_like(sb_ref)
        whole = slice(0, GLA_BLOCK)
        _gla_direction(qc_ref, kc_ref, vc_ref, lacf_ref, tri_ref, sf_ref, ocf_ref, 0, whole)
        _gla_direction(qc_ref, kc_ref, vc_ref, lacb_ref, tri_ref, sb_ref, ocb_ref, 1, whole)

    @pl.when(t > 0)
    def _():
        blocks = [slice(s * GLA_BLOCK, (s + 1) * GLA_BLOCK) for s in range(GLA_STEP_BLOCKS)]
        for rows_f, rows_b in zip(blocks, reversed(blocks)):
            _gla_direction(qf_ref, kf_ref, vf_ref, laf_ref, tri_ref, sf_ref, of_ref, 0, rows_f)
            _gla_direction(qb_ref, kb_ref, vb_ref, lab_ref, tri_ref, sb_ref, ob_ref, 1, rows_b)


def _gla_scan_call(qkvg_x, la_x, qkvg_c, la_c):
    b, seq, _ = qkvg_x.shape
    step_rows = GLA_BLOCK * GLA_STEP_BLOCKS
    nb = seq // step_rows
    assert qkvg_c.shape[1] == GLA_BLOCK and seq % step_rows == 0
    k_off = GLA_KEY_DIM // GLA_DK
    v_off = 2 * GLA_KEY_DIM // GLA_DV

    def pos_f(t):
        return jnp.maximum(t - 1, 0)

    def pos_b(t):
        return nb - jnp.maximum(t, 1)

    def pos_c(t):
        return 0

    def specs(pos, rows, la_col_off, with_la=True):
        out = [
            pl.BlockSpec((None, rows, GLA_DK), lambda bi, h, t: (bi, pos(t), h)),
            pl.BlockSpec((None, rows, GLA_DK), lambda bi, h, t: (bi, pos(t), k_off + h)),
            pl.BlockSpec((None, rows, GLA_DV), lambda bi, h, t: (bi, pos(t), v_off + h)),
        ]
        if with_la:
            out.append(la_spec(pos, rows, la_col_off))
        return out

    def la_spec(pos, rows, la_col_off):
        return pl.BlockSpec((None, rows, GLA_DK), lambda bi, h, t: (bi, pos(t), la_col_off + h))

    def o_spec(pos, rows):
        return pl.BlockSpec((None, rows, GLA_DV), lambda bi, h, t: (bi, pos(t), h))

    tri = _gla_tri_matrices()
    in_specs = (specs(pos_f, step_rows, 0) + specs(pos_b, step_rows, GLA_HEADS)
                + specs(pos_c, GLA_BLOCK, 0, with_la=False)
                + [la_spec(pos_c, GLA_BLOCK, 0), la_spec(pos_c, GLA_BLOCK, GLA_HEADS),
                   pl.BlockSpec(tri.shape, lambda bi, h, t: (0, 0, 0, 0))])
    ox_shape = jax.ShapeDtypeStruct((b, seq, GLA_VAL_DIM), BF16)
    oc_shape = jax.ShapeDtypeStruct((b, GLA_BLOCK, GLA_VAL_DIM), BF16)
    return pl.pallas_call(
        _gla_scan_kernel,
        grid=(b, GLA_HEADS, nb + 1),
        in_specs=in_specs,
        out_specs=[o_spec(pos_f, step_rows), o_spec(pos_b, step_rows),
                   o_spec(pos_c, GLA_BLOCK), o_spec(pos_c, GLA_BLOCK)],
        out_shape=[ox_shape, ox_shape, oc_shape, oc_shape],
        scratch_shapes=[pltpu.VMEM((GLA_DV, GLA_DK), F32), pltpu.VMEM((GLA_DV, GLA_DK), F32)],
        compiler_params=_params(("arbitrary", "arbitrary", "arbitrary")),
        name="gla_scan",
    )(qkvg_x, qkvg_x, qkvg_x, la_x, qkvg_x, qkvg_x, qkvg_x, la_x,
      qkvg_c, qkvg_c, qkvg_c, la_c, la_c, tri)


def _finish_kernel(*refs, head_norm):
    if head_norm:
        x_ref, gate_ref, o1_ref, o2_ref, g_ref, gn_ref, w_ref, out_ref = refs
        o = o1_ref[...].astype(F32) + o2_ref[...].astype(F32)
        parts = []
        for h in range(GLA_HEADS):
            oh = o[:, h * GLA_DV:(h + 1) * GLA_DV]
            ms = jnp.mean(oh * oh, axis=-1, keepdims=True)
            parts.append(oh * lax.rsqrt(ms + EPS) * gn_ref[...])
        o = jnp.concatenate(parts, axis=-1)
    else:
        x_ref, gate_ref, o1_ref, g_ref, w_ref, out_ref = refs
        o = o1_ref[...].astype(F32)
    half_g = 0.5 * g_ref[...].astype(F32)
    silu_g = half_g * (1.0 + jnp.tanh(half_g))
    y_in = (o * silu_g).astype(BF16)
    y = _dot(y_in, w_ref[...])
    out_ref[...] = x_ref[...] + gate_ref[...] * y


def _finish_call(x, gate, o_list, g_arr, g_col_blk, w, *, gn_w=None, name):
    b, t, d = x.shape
    tm = min(FINISH_TM, t)
    assert t % tm == 0
    width = w.shape[0]
    in_specs = [
        pl.BlockSpec((None, tm, d), lambda bi, i: (bi, i, 0)),
        pl.BlockSpec((None, 1, d), lambda bi, i: (bi, 0, 0)),
    ]
    args = [x, gate]
    for o in o_list:
        in_specs.append(pl.BlockSpec((None, tm, width), lambda bi, i: (bi, i, 0)))
        args.append(o)
    in_specs.append(pl.BlockSpec((None, tm, width), lambda bi, i: (bi, i, g_col_blk)))
    args.append(g_arr)
    if gn_w is not None:
        in_specs.append(pl.BlockSpec((1, gn_w.shape[-1]), lambda bi, i: (0, 0)))
        args.append(gn_w)
    in_specs.append(pl.BlockSpec((width, d), lambda bi, i: (0, 0)))
    args.append(w)
    return pl.pallas_call(
        functools.partial(_finish_kernel, head_norm=gn_w is not None),
        grid=(b, t // tm),
        in_specs=in_specs,
        out_specs=pl.BlockSpec((None, tm, d), lambda bi, i: (bi, i, 0)),
        out_shape=jax.ShapeDtypeStruct((b, t, d), F32),
        compiler_params=_params(("arbitrary", "arbitrary")),
        name=name,
    )(*args)


def _attn_finalize(acc1, l1, acc2, l2, lam_ref, sw_ref, o_ref, lambda_init):
    lv = lam_ref[...]
    lam = (jnp.exp(jnp.sum(lv[0:1] * lv[1:2], axis=-1, keepdims=True))
           - jnp.exp(jnp.sum(lv[2:3] * lv[3:4], axis=-1, keepdims=True)) + lambda_init)
    o = acc1 / l1 - lam * (acc2 / l2)
    ms = jnp.mean(o * o, axis=-1, keepdims=True)
    o = o * lax.rsqrt(ms + EPS) * sw_ref[...] * (1.0 - lambda_init)
    o_ref[...] = o.astype(BF16)


def _attn_online_kernel(q1_ref, q2_ref, k1_ref, k2_ref, v_ref, k1c_ref, k2c_ref, vc_ref,
                        lam_ref, sw_ref, o_ref, m_ref, l_ref, acc_ref, *, tk, lambda_init):
    n_keys = v_ref.shape[0]
    m_ref[...] = jnp.full_like(m_ref, -1e30)
    l_ref[...] = jnp.zeros_like(l_ref)
    acc_ref[...] = jnp.zeros_like(acc_ref)

    def update(mp, q_ref, k_blk, v_blk):
        s = _dot_nt(q_ref[...], k_blk)
        m_old = m_ref[mp]
        m_new = jnp.maximum(m_old, jnp.max(s, axis=1, keepdims=True))
        alpha = jnp.exp2(m_old - m_new)
        p = jnp.exp2(s - m_new)
        l_ref[mp] = alpha * l_ref[mp] + jnp.sum(p, axis=1, keepdims=True)
        acc_ref[mp] = alpha * acc_ref[mp] + _dot(p.astype(BF16), v_blk)
        m_ref[mp] = m_new

    def body(c, carry):
        off = pl.multiple_of(c * tk, tk)
        v_blk = v_ref[pl.ds(off, tk), :]
        update(0, q1_ref, k1_ref[pl.ds(off, tk), :], v_blk)
        update(1, q2_ref, k2_ref[pl.ds(off, tk), :], v_blk)
        return carry

    lax.fori_loop(0, n_keys // tk, body, 0)
    update(0, q1_ref, k1c_ref[...], vc_ref[...])
    update(1, q2_ref, k2c_ref[...], vc_ref[...])
    _attn_finalize(acc_ref[0], l_ref[0], acc_ref[1], l_ref[1], lam_ref, sw_ref, o_ref, lambda_init)


def _attn_bounded_kernel(q1_ref, q2_ref, k1_ref, k2_ref, v_ref, k1c_ref, k2c_ref, vc_ref,
                         lam_ref, sw_ref, o_ref, l_ref, acc_ref, *, tk, lambda_init):
    n_keys = v_ref.shape[0]
    l_ref[...] = jnp.zeros_like(l_ref)
    acc_ref[...] = jnp.zeros_like(acc_ref)

    def sub_block(q, k_blk, v_blk):
        s = _dot_nt(q, k_blk)
        p = jnp.exp2(s)
        part = p[:, :V7X_LANES] + p[:, V7X_LANES:]
        return part, _dot(p.astype(BF16), v_blk)

    def body(c, carry):
        for mp, (q_ref, k_ref) in enumerate(((q1_ref, k1_ref), (q2_ref, k2_ref))):
            q = q_ref[...]
            acc = None
            lsum = None
            for sb in range(tk // ATTN_KEY_SUB):
                off = pl.multiple_of(c * tk + sb * ATTN_KEY_SUB, ATTN_KEY_SUB)
                part, d = sub_block(q, k_ref[pl.ds(off, ATTN_KEY_SUB), :],
                                    v_ref[pl.ds(off, ATTN_KEY_SUB), :])
                lsum = part if lsum is None else lsum + part
                acc = d if acc is None else acc + d
            l_ref[mp] += lsum
            acc_ref[mp] += acc
        return carry

    lax.fori_loop(0, n_keys // tk, body, 0)

    for mp, (q_ref, kc_ref) in enumerate(((q1_ref, k1c_ref), (q2_ref, k2c_ref))):
        part, d = sub_block(q_ref[...], kc_ref[...], vc_ref[...])
        l_ref[mp] += part
        acc_ref[mp] += d

    l1 = jnp.sum(l_ref[0], axis=-1, keepdims=True)
    l2 = jnp.sum(l_ref[1], axis=-1, keepdims=True)
    _attn_finalize(acc_ref[0], l1, acc_ref[1], l2, lam_ref, sw_ref, o_ref, lambda_init)


def _attn_call(qkvg_x, qkvg_c, lam_v, subln_w, lambda_init, *, bounded):
    b, seq, _ = qkvg_x.shape
    n_ctx = qkvg_c.shape[1]
    tq, tk = (ATTN_TQ, ATTN_TK) if bounded else (ATTN_TQ_ONLINE, ATTN_TK_ONLINE)
    assert seq % tq == 0 and seq % tk == 0 and tk % ATTN_KEY_SUB == 0 and n_ctx == ATTN_KEY_SUB
    d = DIFF_HEAD_DIM
    k_off = DIFF_WIDTH // d
    v_off = 2 * DIFF_WIDTH // DIFF_VAL_DIM
    in_specs = [
        pl.BlockSpec((None, tq, d), lambda bi, h, i: (bi, i, 2 * h)),
        pl.BlockSpec((None, tq, d), lambda bi, h, i: (bi, i, 2 * h + 1)),
        pl.BlockSpec((None, seq, d), lambda bi, h, i: (bi, 0, k_off + 2 * h)),
        pl.BlockSpec((None, seq, d), lambda bi, h, i: (bi, 0, k_off + 2 * h + 1)),
        pl.BlockSpec((None, seq, DIFF_VAL_DIM), lambda bi, h, i: (bi, 0, v_off + h)),
        pl.BlockSpec((None, n_ctx, d), lambda bi, h, i: (bi, 0, k_off + 2 * h)),
        pl.BlockSpec((None, n_ctx, d), lambda bi, h, i: (bi, 0, k_off + 2 * h + 1)),
        pl.BlockSpec((None, n_ctx, DIFF_VAL_DIM), lambda bi, h, i: (bi, 0, v_off + h)),
        pl.BlockSpec(lam_v.shape, lambda bi, h, i: (0, 0)),
        pl.BlockSpec((1, DIFF_VAL_DIM), lambda bi, h, i: (0, 0)),
    ]
    if bounded:
        body = functools.partial(_attn_bounded_kernel, tk=tk, lambda_init=lambda_init)
        scratch = [pltpu.VMEM((2, tq, V7X_LANES), F32), pltpu.VMEM((2, tq, DIFF_VAL_DIM), F32)]
    else:
        body = functools.partial(_attn_online_kernel, tk=tk, lambda_init=lambda_init)
        scratch = [pltpu.VMEM((2, tq, 1), F32), pltpu.VMEM((2, tq, 1), F32),
                   pltpu.VMEM((2, tq, DIFF_VAL_DIM), F32)]
    return pl.pallas_call(
        body,
        grid=(b, DIFF_HEADS, seq // tq),
        in_specs=in_specs,
        out_specs=pl.BlockSpec((None, tq, DIFF_VAL_DIM), lambda bi, h, i: (bi, i, h)),
        out_shape=jax.ShapeDtypeStruct((b, seq, DIFF_WIDTH), BF16),
        scratch_shapes=scratch,
        compiler_params=_params(("arbitrary", "arbitrary", "arbitrary")),
        name="diff_attn_bounded" if bounded else "diff_attn_online",
    )(qkvg_x, qkvg_x, qkvg_x, qkvg_x, qkvg_x, qkvg_c, qkvg_c, qkvg_c,
      lam_v, subln_w.reshape(1, DIFF_VAL_DIM))


def _axial_rope_tables(n_tokens):
    rows_n = n_tokens // GRID_W
    half = DIFF_HEAD_DIM // 2
    inv_freq = (ROPE_BASE ** (-np.arange(0, half, 2, dtype=np.float32) / half)).astype(np.float32)
    ang_r = np.arange(rows_n, dtype=np.float32)[:, None] * inv_freq
    ang_c = np.arange(GRID_W, dtype=np.float32)[:, None] * inv_freq

    def expand(fn):
        tr, tc = fn(ang_r), fn(ang_c)
        by_row = np.concatenate([tr, tr, np.zeros_like(tr), np.zeros_like(tr)], axis=-1)
        by_col = np.concatenate([np.zeros_like(tc), np.zeros_like(tc), tc, tc], axis=-1)
        full = jnp.asarray(by_row)[:, None, :] + jnp.asarray(by_col)[None, :, :]
        return full.reshape(n_tokens, DIFF_HEAD_DIM)

    return expand(np.cos), expand(np.sin)


def kernel(x, c, ctx, c_ctx, norm_w, ada_w, ada_b, gla_w_in, gla_gate_a1, gla_gate_a2, gla_gate_b,
           gla_gn_w, gla_w_out, diff_w_in, diff_qn_w, diff_kn_w, diff_lam, diff_subln_w, diff_w_out):
    b, seq, d = x.shape
    ctx_len = ctx.shape[1]
    assert ctx_len == GLA_BLOCK and seq % GLA_BLOCK == 0 and d == D_MODEL

    cond = jnp.zeros((ADA_ROWS, d), F32).at[:b].set(c).at[b].set(c_ctx)
    mod = _ada_call(cond, ada_w, ada_b)

    def mod_rows(layer, part):
        m = mod[layer, :, part * d:(part + 1) * d]
        lat = m[:b, None, :]
        return lat, m[b][None, None, :]

    shift, shift_c = mod_rows(0, 0)
    scale, scale_c = mod_rows(0, 1)
    gate, gate_c = mod_rows(0, 2)
    nw0 = norm_w[0].reshape(1, d)
    w_in0 = gla_w_in[0].astype(BF16)
    a1 = jnp.zeros((d, V7X_LANES), F32)
    a1 = a1.at[:, :GLA_GATE_RANK].set(gla_gate_a1[0, 0]).at[:, GLA_GATE_RANK:2 * GLA_GATE_RANK].set(
        gla_gate_a1[0, 1]).astype(BF16)
    a2 = jnp.zeros((V7X_LANES, 2 * GLA_KEY_DIM), F32)
    a2 = a2.at[:GLA_GATE_RANK, :GLA_KEY_DIM].set(gla_gate_a2[0, 0])
    a2 = a2.at[GLA_GATE_RANK:2 * GLA_GATE_RANK, GLA_KEY_DIM:].set(gla_gate_a2[0, 1]).astype(BF16)
    gb = gla_gate_b[0].reshape(1, 2 * GLA_KEY_DIM)
    gla_extras = [a1, a2, gb]
    gla_extra_specs = [
        pl.BlockSpec(a1.shape, lambda bi, i, j: (0, 0)),
        pl.BlockSpec(a2.shape, lambda bi, i, j: (0, 0)),
        pl.BlockSpec(gb.shape, lambda bi, i, j: (0, 0)),
    ]
    la_out = (2 * GLA_KEY_DIM, BF16)
    tm0, tn0 = GLA_INPROJ_TILE
    qkvg0_x, la0_x = _inproj_call(
        _gla_inproj_kernel, x, shift, scale, nw0, w_in0, gla_extras, gla_extra_specs,
        tm=tm0, tn=tn0, extra_out=la_out, name="gla_inproj_x")
    ctx_rows = b * ctx_len
    ctx_m = ctx.reshape(1, ctx_rows, d)

    def per_batch(a):
        return a.reshape(b, ctx_len, a.shape[-1])

    def merged(a):
        return a.reshape(1, ctx_rows, a.shape[-1])

    qkvg0_c, la0_c = _inproj_call(
        _gla_inproj_kernel, ctx_m, shift_c, scale_c, nw0, w_in0, gla_extras, gla_extra_specs,
        tm=ctx_rows, tn=tn0, extra_out=la_out, name="gla_inproj_ctx")

    o_f, o_b, oc_f, oc_b = _gla_scan_call(qkvg0_x, la0_x, per_batch(qkvg0_c), per_batch(la0_c))

    w_out0 = gla_w_out[0].astype(BF16)
    gn_w = gla_gn_w[0].reshape(1, GLA_DV)
    g_col_blk0 = (2 * GLA_KEY_DIM + GLA_VAL_DIM) // GLA_VAL_DIM
    x1 = _finish_call(x, gate, [o_f, o_b], qkvg0_x, g_col_blk0, w_out0, gn_w=gn_w,
                      name="gla_finish_x")
    ctx1 = _finish_call(ctx_m, gate_c, [merged(oc_f), merged(oc_b)], qkvg0_c, g_col_blk0, w_out0,
                        gn_w=gn_w, name="gla_finish_ctx")

    lambda_init = 0.8 - 0.6 * math.exp(-0.3 * 1)
    shift, shift_c = mod_rows(1, 0)
    scale, scale_c = mod_rows(1, 1)
    gate, _ = mod_rows(1, 2)
    nw1 = norm_w[1].reshape(1, d)
    w_in1 = diff_w_in[0].astype(BF16)
    cos, sin = _axial_rope_tables(seq)
    q_scale = math.log2(math.e) * DIFF_HEAD_DIM ** -0.5
    tm1, tn1 = DIFF_INPROJ_TILE
    head_mats = _head_matrices()
    nrm_args = [diff_qn_w[0].reshape(1, DIFF_VAL_DIM), diff_kn_w[0].reshape(1, DIFF_VAL_DIM)]
    nrm_specs = [
        pl.BlockSpec((1, DIFF_VAL_DIM), lambda bi, i, j: (0, 0)),
        pl.BlockSpec((1, DIFF_VAL_DIM), lambda bi, i, j: (0, 0)),
    ]
    hm_spec = pl.BlockSpec(head_mats.shape, lambda bi, i, j: (0, 0, 0))
    (qkvg1_x,) = _inproj_call(
        functools.partial(_diff_inproj_kernel, rope=True, tn=tn1, q_scale=q_scale),
        x1, shift, scale, nw1, w_in1, nrm_args + [cos, sin, head_mats],
        nrm_specs + [pl.BlockSpec((tm1, DIFF_HEAD_DIM), lambda bi, i, j: (i, 0)),
                     pl.BlockSpec((tm1, DIFF_HEAD_DIM), lambda bi, i, j: (i, 0)), hm_spec],
        tm=tm1, tn=tn1, name="diff_inproj_x")
    (qkvg1_c,) = _inproj_call(
        functools.partial(_diff_inproj_kernel, rope=False, tn=tn1, q_scale=q_scale),
        ctx1, shift_c, scale_c, nw1, w_in1, nrm_args + [cos, sin, head_mats],
        nrm_specs + [pl.BlockSpec((ctx_rows, DIFF_HEAD_DIM), lambda bi, i, j: (0, 0)),
                     pl.BlockSpec((ctx_rows, DIFF_HEAD_DIM), lambda bi, i, j: (0, 0)), hm_spec],
        tm=ctx_rows, tn=tn1, name="diff_inproj_ctx")
    qkvg1_c = per_batch(qkvg1_c)

    score_bound = q_scale * DIFF_HEAD_DIM * jnp.max(
        jnp.max(jnp.abs(diff_qn_w[0]), axis=-1) * jnp.max(jnp.abs(diff_kn_w[0]), axis=-1))
    attn = functools.partial(_attn_call, qkvg1_x, qkvg1_c, diff_lam[0], diff_subln_w[0], lambda_init)
    o_attn = lax.cond(score_bound <= ATTN_SCORE_BOUND,
                      functools.partial(attn, bounded=True), functools.partial(attn, bounded=False))

    w_out1 = diff_w_out[0].astype(BF16)
    g_col_blk1 = 3 * DIFF_WIDTH // DIFF_WIDTH
    return _finish_call(x1, gate, [o_attn], qkvg1_x, g_col_blk1, w_out1, name="diff_finish_x")
```

```python
import functools
import math

import jax
import jax.numpy as jnp
import numpy as np
from jax import lax
from jax.experimental import pallas as pl
from jax.experimental.pallas import tpu as pltpu

F32 = jnp.float32
BF16 = jnp.bfloat16

D_MODEL = 2048
DEPTH = 2
GRID_W = 64
EPS = 1e-6
GLA_HEADS = 4
GLA_KEY_DIM = D_MODEL // 2
GLA_VAL_DIM = D_MODEL
GLA_DK = GLA_KEY_DIM // GLA_HEADS
GLA_DV = GLA_VAL_DIM // GLA_HEADS
GLA_GATE_RANK = 16
GLA_GATE_NORM = 16.0
GLA_CHUNK = 64
GLA_IN_DIM = 2 * GLA_KEY_DIM + 2 * GLA_VAL_DIM
DIFF_HEAD_DIM = 128
DIFF_HEADS = D_MODEL // (2 * DIFF_HEAD_DIM)
DIFF_VAL_DIM = 2 * DIFF_HEAD_DIM
DIFF_WIDTH = DIFF_HEADS * DIFF_VAL_DIM
DIFF_IN_DIM = 4 * DIFF_WIDTH
ROPE_BASE = 10000.0

V7X_LANES = 128
V7X_SUBLANES = 8
V7X_MXU_DIM = 256
VMEM_LIMIT = 56 * 1024 * 1024

ADA_ROWS = V7X_SUBLANES
ADA_TN = 1024
GLA_BLOCK = 256
GLA_STEP_BLOCKS = 4
GLA_INPROJ_TILE = (1024, 768)
DIFF_INPROJ_TILE = (1024, 1024)
FINISH_TM = 512
ATTN_TQ = 1024
ATTN_TK = 8192
ATTN_TQ_ONLINE = 512
ATTN_TK_ONLINE = 1024
ATTN_KEY_SUB = V7X_MXU_DIM
ATTN_SCORE_BOUND = 60.0


def _dot(a, b):
    return jnp.dot(a, b, preferred_element_type=F32)


def _dot_nt(a, b):
    return lax.dot_general(a, b, (((1,), (1,)), ((), ())), preferred_element_type=F32)


def _params(sem):
    return pltpu.CompilerParams(dimension_semantics=sem, vmem_limit_bytes=VMEM_LIMIT)


def _ada_kernel(c_ref, w_ref, b_ref, o_ref):
    c = c_ref[...]
    s = c * jax.nn.sigmoid(c)
    o_ref[...] = jnp.dot(s, w_ref[...], preferred_element_type=F32,
                         precision=lax.Precision.HIGHEST) + b_ref[...]


def _ada_call(cond, ada_w, ada_b):
    tn = ADA_TN
    n = ada_w.shape[-1]
    return pl.pallas_call(
        _ada_kernel,
        grid=(DEPTH, n // tn),
        in_specs=[
            pl.BlockSpec((ADA_ROWS, D_MODEL), lambda l, j: (0, 0)),
            pl.BlockSpec((None, D_MODEL, tn), lambda l, j: (l, 0, j)),
            pl.BlockSpec((None, 1, tn), lambda l, j: (l, 0, j)),
        ],
        out_specs=pl.BlockSpec((None, ADA_ROWS, tn), lambda l, j: (l, 0, j)),
        out_shape=jax.ShapeDtypeStruct((DEPTH, ADA_ROWS, n), F32),
        compiler_params=_params(("arbitrary", "arbitrary")),
        name="ada_mod",
    )(cond, ada_w, ada_b.reshape(DEPTH, 1, n))


def _normalize_into(x_ref, shift_ref, scale_ref, nw_ref, hn_ref):
    x = x_ref[...]
    ms = jnp.mean(x * x, axis=-1, keepdims=True)
    a = nw_ref[...] * (1.0 + scale_ref[...])
    hn_ref[...] = (x * lax.rsqrt(ms + EPS) * a + shift_ref[...]).astype(BF16)


def _gla_inproj_kernel(x_ref, shift_ref, scale_ref, nw_ref, w_ref, a1_ref, a2_ref, gb_ref,
                       out_ref, la_ref, hn_ref, z_ref):
    j = pl.program_id(2)

    @pl.when(j == 0)
    def _():
        _normalize_into(x_ref, shift_ref, scale_ref, nw_ref, hn_ref)
        z_ref[...] = _dot(hn_ref[...], a1_ref[...]).astype(BF16)

    out_ref[...] = _dot(hn_ref[...], w_ref[...]).astype(BF16)
    lg = _dot(z_ref[...], a2_ref[...]) + gb_ref[...]
    log_sig = jnp.minimum(lg, 0.0) - jnp.log(1.0 + jnp.exp(-jnp.abs(lg)))
    la_ref[...] = (log_sig * (1.0 / GLA_GATE_NORM)).astype(la_ref.dtype)


def _head_matrices():
    qd = DIFF_HEAD_DIM // 4
    e = np.arange(qd)
    rot = np.zeros((DIFF_HEAD_DIM, DIFF_HEAD_DIM), np.float32)
    rot[qd + e, e] = -1.0
    rot[e, qd + e] = 1.0
    rot[3 * qd + e, 2 * qd + e] = -1.0
    rot[2 * qd + e, 3 * qd + e] = 1.0
    both_maps = np.eye(2, dtype=np.float32)
    mean = np.kron(both_maps, np.full((DIFF_HEAD_DIM, DIFF_HEAD_DIM), 1.0 / DIFF_HEAD_DIM, np.float32))
    return jnp.asarray(np.stack([mean, np.kron(both_maps, rot)]), BF16)


def _diff_inproj_kernel(x_ref, shift_ref, scale_ref, nw_ref, w_ref, qn_ref, kn_ref,
                        cos_ref, sin_ref, hm_ref, out_ref, hn_ref, *, rope, tn, q_scale):
    j = pl.program_id(2)
    tiles_per_part = DIFF_WIDTH // tn

    @pl.when(j == 0)
    def _():
        _normalize_into(x_ref, shift_ref, scale_ref, nw_ref, hn_ref)

    acc = _dot(hn_ref[...], w_ref[...])

    def head_norm(nrm_ref, scale):
        w2 = nrm_ref[...] * scale
        if rope:
            cos2 = jnp.concatenate([cos_ref[...], cos_ref[...]], axis=1)
            sin2 = jnp.concatenate([sin_ref[...], sin_ref[...]], axis=1)
        for head in range(tn // DIFF_VAL_DIM):
            cols = slice(head * DIFF_VAL_DIM, (head + 1) * DIFF_VAL_DIM)
            t2 = acc[:, cols]
            ms = _dot((t2 * t2).astype(BF16), hm_ref[0])
            u = t2 * w2
            if rope:
                u = u * cos2 + _dot(u.astype(BF16), hm_ref[1]) * sin2
            out_ref[:, cols] = (u * lax.rsqrt(ms + EPS)).astype(BF16)

    @pl.when(j < tiles_per_part)
    def _():
        head_norm(qn_ref, q_scale)

    @pl.when((j >= tiles_per_part) & (j < 2 * tiles_per_part))
    def _():
        head_norm(kn_ref, 1.0)

    @pl.when(j >= 2 * tiles_per_part)
    def _():
        out_ref[...] = acc.astype(BF16)


def _inproj_call(kernel, x, shift, scale, nw, w, extras, extra_specs, *, tm, tn,
                 extra_out=None, extra_scratch=(), name):
    b, t, d = x.shape
    n = w.shape[-1]
    assert t % tm == 0 and n % tn == 0
    in_specs = [
        pl.BlockSpec((None, tm, d), lambda bi, i, j: (bi, i, 0)),
        pl.BlockSpec((None, 1, d), lambda bi, i, j: (bi, 0, 0)),
        pl.BlockSpec((None, 1, d), lambda bi, i, j: (bi, 0, 0)),
        pl.BlockSpec((1, d), lambda bi, i, j: (0, 0)),
        pl.BlockSpec((d, tn), lambda bi, i, j: (0, j)),
    ] + extra_specs
    out_specs = [pl.BlockSpec((None, tm, tn), lambda bi, i, j: (bi, i, j))]
    out_shape = [jax.ShapeDtypeStruct((b, t, n), BF16)]
    if extra_out is not None:
        width, dtype = extra_out
        out_specs.append(pl.BlockSpec((None, tm, width // (n // tn)), lambda bi, i, j: (bi, i, j)))
        out_shape.append(jax.ShapeDtypeStruct((b, t, width), dtype))
    return pl.pallas_call(
        kernel,
        grid=(b, t // tm, n // tn),
        in_specs=in_specs,
        out_specs=out_specs,
        out_shape=out_shape,
        scratch_shapes=[pltpu.VMEM((tm, d), BF16)] + [pltpu.VMEM((tm,) + tuple(s), dt)
                                                      for s, dt in extra_scratch],
        compiler_params=_params(("arbitrary", "arbitrary", "arbitrary")),
        name=name,
    )(x, shift, scale, nw, w, *extras)


def _gla_tri_matrices():
    idx = jnp.arange(GLA_BLOCK)
    i, j = idx[:, None], idx[None, :]
    same = (i // GLA_CHUNK) == (j // GLA_CHUNK)
    lower, upper = j <= i, j >= i
    return jnp.stack([jnp.stack([lower, lower & same]),
                      jnp.stack([upper, upper & same])]).astype(BF16)


def _dot_tn(a, b):
    return lax.dot_general(a, b, (((0,), (0,)), ((), ())), preferred_element_type=F32)


def _gla_direction(q_ref, k_ref, v_ref, la_ref, tri_ref, st_ref, o_ref, direction, rows):
    fwd = direction == 0
    cs = GLA_CHUNK
    n_sub = GLA_BLOCK // cs
    b = _dot(tri_ref[direction, 0], la_ref[rows, :])
    tot = b[GLA_BLOCK - 1:GLA_BLOCK] if fwd else b[0:1]

    def per_chunk_rows(row_of_chunk):
        return jnp.concatenate(
            [jnp.broadcast_to(b[row_of_chunk(c):row_of_chunk(c) + 1], (cs, GLA_DK))
             for c in range(n_sub)], axis=0)

    if fwd:
        mid = per_chunk_rows(lambda c: c * cs + cs // 2 - 1)
        edge = per_chunk_rows(lambda c: max(c * cs - 1, 0))
    else:
        mid = per_chunk_rows(lambda c: c * cs + cs // 2)
        edge = per_chunk_rows(lambda c: min((c + 1) * cs, GLA_BLOCK - 1))

    q = q_ref[rows, :].astype(F32) * (GLA_DK ** -0.5)
    k = k_ref[rows, :].astype(F32)
    v = v_ref[rows, :]
    st = st_ref[...]
    q_in = (q * jnp.exp(b)).astype(BF16)
    k_in = (k * jnp.exp(tot - b)).astype(BF16)
    q_d = (q * jnp.exp(b - mid)).astype(BF16)
    k_d = (k * jnp.exp(mid - b)).astype(BF16)
    q_o = (q * jnp.exp(b - edge)).astype(BF16)

    scores = jnp.where(tri_ref[direction, 1] > 0, _dot_nt(q_d, k_d), 0.0)
    tok = lax.broadcasted_iota(jnp.int32, (GLA_BLOCK, 1), 0)
    off = []
    for c in range(n_sub):
        lo, hi = c * cs, (c + 1) * cs
        first, last, ref_row = (0, lo, lo - 1) if fwd else (hi, GLA_BLOCK, hi)
        if last > first:
            expo = jnp.where((tok >= first) & (tok < last), b[ref_row:ref_row + 1] - b, -1e30)
            k_o = (k * jnp.exp(expo)).astype(BF16)
            off.append(_dot_nt(q_o[lo:hi], k_o))
        else:
            off.append(jnp.zeros((cs, GLA_BLOCK), F32))
    scores = scores + jnp.concatenate(off, axis=0)

    o = _dot_nt(q_in, st.astype(BF16)) + _dot(scores.astype(BF16), v)
    o_ref[rows, :] = o.astype(BF16)
    st_ref[...] = st * jnp.exp(tot) + _dot_tn(v, k_in)


def _gla_scan_kernel(qf_ref, kf_ref, vf_ref, laf_ref, qb_ref, kb_ref, vb_ref, lab_ref,
                     qc_ref, kc_ref, vc_ref, lacf_ref, lacb_ref, tri_ref,
                     of_ref, ob_ref, ocf_ref, ocb_ref, sf_ref, sb_ref):
    t = pl.program_id(2)

    @pl.when(t == 0)
    def _():
        sf_ref[...] = jnp.zeros_like(sf_ref)
        sb_ref[...] = jnp.zeros_like(sb_ref)
        whole = slice(0, GLA_BLOCK)
        _gla_direction(qc_ref, kc_ref, vc_ref, lacf_ref, tri_ref, sf_ref, ocf_ref, 0, whole)
        _gla_direction(qc_ref, kc_ref, vc_ref, lacb_ref, tri_ref, sb_ref, ocb_ref, 1, whole)

    @pl.when(t > 0)
    def _():
        blocks = [slice(s * GLA_BLOCK, (s + 1) * GLA_BLOCK) for s in range(GLA_STEP_BLOCKS)]
        for rows_f, rows_b in zip(blocks, reversed(blocks)):
            _gla_direction(qf_ref, kf_ref, vf_ref, laf_ref, tri_ref, sf_ref, of_ref, 0, rows_f)
            _gla_direction(qb_ref, kb_ref, vb_ref, lab_ref, tri_ref, sb_ref, ob_ref, 1, rows_b)


def _gla_scan_call(qkvg_x, la_x, qkvg_c, la_c):
    b, seq, _ = qkvg_x.shape
    step_rows = GLA_BLOCK * GLA_STEP_BLOCKS
    nb = seq // step_rows
    assert qkvg_c.shape[1] == GLA_BLOCK and seq % step_rows == 0
    k_off = GLA_KEY_DIM // GLA_DK
    v_off = 2 * GLA_KEY_DIM // GLA_DV

    def pos_f(t):
        return jnp.maximum(t - 1, 0)

    def pos_b(t):
        return nb - jnp.maximum(t, 1)

    def pos_c(t):
        return 0

    def specs(pos, rows, la_col_off, with_la=True):
        out = [
            pl.BlockSpec((None, rows, GLA_DK), lambda bi, h, t: (bi, pos(t), h)),
            pl.BlockSpec((None, rows, GLA_DK), lambda bi, h, t: (bi, pos(t), k_off + h)),
            pl.BlockSpec((None, rows, GLA_DV), lambda bi, h, t: (bi, pos(t), v_off + h)),
        ]
        if with_la:
            out.append(la_spec(pos, rows, la_col_off))
        return out

    def la_spec(pos, rows, la_col_off):
        return pl.BlockSpec((None, rows, GLA_DK), lambda bi, h, t: (bi, pos(t), la_col_off + h))

    def o_spec(pos, rows):
        return pl.BlockSpec((None, rows, GLA_DV), lambda bi, h, t: (bi, pos(t), h))

    tri = _gla_tri_matrices()
    in_specs = (specs(pos_f, step_rows, 0) + specs(pos_b, step_rows, GLA_HEADS)
                + specs(pos_c, GLA_BLOCK, 0, with_la=False)
                + [la_spec(pos_c, GLA_BLOCK, 0), la_spec(pos_c, GLA_BLOCK, GLA_HEADS),
                   pl.BlockSpec(tri.shape, lambda bi, h, t: (0, 0, 0, 0))])
    ox_shape = jax.ShapeDtypeStruct((b, seq, GLA_VAL_DIM), BF16)
    oc_shape = jax.ShapeDtypeStruct((b, GLA_BLOCK, GLA_VAL_DIM), BF16)
    return pl.pallas_call(
        _gla_scan_kernel,
        grid=(b, GLA_HEADS, nb + 1),
        in_specs=in_specs,
        out_specs=[o_spec(pos_f, step_rows), o_spec(pos_b, step_rows),
                   o_spec(pos_c, GLA_BLOCK), o_spec(pos_c, GLA_BLOCK)],
        out_shape=[ox_shape, ox_shape, oc_shape, oc_shape],
        scratch_shapes=[pltpu.VMEM((GLA_DV, GLA_DK), F32), pltpu.VMEM((GLA_DV, GLA_DK), F32)],
        compiler_params=_params(("arbitrary", "arbitrary", "arbitrary")),
        name="gla_scan",
    )(qkvg_x, qkvg_x, qkvg_x, la_x, qkvg_x, qkvg_x, qkvg_x, la_x,
      qkvg_c, qkvg_c, qkvg_c, la_c, la_c, tri)


def _finish_kernel(*refs, head_norm):
    if head_norm:
        x_ref, gate_ref, o1_ref, o2_ref, g_ref, gn_ref, w_ref, out_ref = refs
        o = o1_ref[...].astype(F32) + o2_ref[...].astype(F32)
        parts = []
        for h in range(GLA_HEADS):
            oh = o[:, h * GLA_DV:(h + 1) * GLA_DV]
            ms = jnp.mean(oh * oh, axis=-1, keepdims=True)
            parts.append(oh * lax.rsqrt(ms + EPS) * gn_ref[...])
        o = jnp.concatenate(parts, axis=-1)
    else:
        x_ref, gate_ref, o1_ref, g_ref, w_ref, out_ref = refs
        o = o1_ref[...].astype(F32)
    half_g = 0.5 * g_ref[...].astype(F32)
    silu_g = half_g * (1.0 + jnp.tanh(half_g))
    y_in = (o * silu_g).astype(BF16)
    y = _dot(y_in, w_ref[...])
    out_ref[...] = x_ref[...] + gate_ref[...] * y


def _finish_call(x, gate, o_list, g_arr, g_col_blk, w, *, gn_w=None, name):
    b, t, d = x.shape
    tm = min(FINISH_TM, t)
    assert t % tm == 0
    width = w.shape[0]
    in_specs = [
        pl.BlockSpec((None, tm, d), lambda bi, i: (bi, i, 0)),
        pl.BlockSpec((None, 1, d), lambda bi, i: (bi, 0, 0)),
    ]
    args = [x, gate]
    for o in o_list:
        in_specs.append(pl.BlockSpec((None, tm, width), lambda bi, i: (bi, i, 0)))
        args.append(o)
    in_specs.append(pl.BlockSpec((None, tm, width), lambda bi, i: (bi, i, g_col_blk)))
    args.append(g_arr)
    if gn_w is not None:
        in_specs.append(pl.BlockSpec((1, gn_w.shape[-1]), lambda bi, i: (0, 0)))
        args.append(gn_w)
    in_specs.append(pl.BlockSpec((width, d), lambda bi, i: (0, 0)))
    args.append(w)
    return pl.pallas_call(
        functools.partial(_finish_kernel, head_norm=gn_w is not None),
        grid=(b, t // tm),
        in_specs=in_specs,
        out_specs=pl.BlockSpec((None, tm, d), lambda bi, i: (bi, i, 0)),
        out_shape=jax.ShapeDtypeStruct((b, t, d), F32),
        compiler_params=_params(("arbitrary", "arbitrary")),
        name=name,
    )(*args)


def _attn_finalize(acc1, l1, acc2, l2, lam_ref, sw_ref, o_ref, lambda_init):
    lv = lam_ref[...]
    lam = (jnp.exp(jnp.sum(lv[0:1] * lv[1:2], axis=-1, keepdims=True))
           - jnp.exp(jnp.sum(lv[2:3] * lv[3:4], axis=-1, keepdims=True)) + lambda_init)
    o = acc1 / l1 - lam * (acc2 / l2)
    ms = jnp.mean(o * o, axis=-1, keepdims=True)
    o = o * lax.rsqrt(ms + EPS) * sw_ref[...] * (1.0 - lambda_init)
    o_ref[...] = o.astype(BF16)


def _attn_online_kernel(q1_ref, q2_ref, k1_ref, k2_ref, v_ref, k1c_ref, k2c_ref, vc_ref,
                        lam_ref, sw_ref, o_ref, m_ref, l_ref, acc_ref, *, tk, lambda_init):
    n_keys = v_ref.shape[0]
    m_ref[...] = jnp.full_like(m_ref, -1e30)
    l_ref[...] = jnp.zeros_like(l_ref)
    acc_ref[...] = jnp.zeros_like(acc_ref)

    def update(mp, q_ref, k_blk, v_blk):
        s = _dot_nt(q_ref[...], k_blk)
        m_old = m_ref[mp]
        m_new = jnp.maximum(m_old, jnp.max(s, axis=1, keepdims=True))
        alpha = jnp.exp2(m_old - m_new)
        p = jnp.exp2(s - m_new)
        l_ref[mp] = alpha * l_ref[mp] + jnp.sum(p, axis=1, keepdims=True)
        acc_ref[mp] = alpha * acc_ref[mp] + _dot(p.astype(BF16), v_blk)
        m_ref[mp] = m_new

    def body(c, carry):
        off = pl.multiple_of(c * tk, tk)
        v_blk = v_ref[pl.ds(off, tk), :]
        update(0, q1_ref, k1_ref[pl.ds(off, tk), :], v_blk)
        update(1, q2_ref, k2_ref[pl.ds(off, tk), :], v_blk)
        return carry

    lax.fori_loop(0, n_keys // tk, body, 0)
    update(0, q1_ref, k1c_ref[...], vc_ref[...])
    update(1, q2_ref, k2c_ref[...], vc_ref[...])
    _attn_finalize(acc_ref[0], l_ref[0], acc_ref[1], l_ref[1], lam_ref, sw_ref, o_ref, lambda_init)


def _attn_bounded_kernel(q1_ref, q2_ref, k1_ref, k2_ref, v_ref, k1c_ref, k2c_ref, vc_ref,
                         lam_ref, sw_ref, o_ref, l_ref, acc_ref, *, tk, lambda_init):
    n_keys = v_ref.shape[0]
    l_ref[...] = jnp.zeros_like(l_ref)
    acc_ref[...] = jnp.zeros_like(acc_ref)

    def sub_block(q, k_blk, v_blk):
        s = _dot_nt(q, k_blk)
        p = jnp.exp2(s)
        part = p[:, :V7X_LANES] + p[:, V7X_LANES:]
        return part, _dot(p.astype(BF16), v_blk)

    def body(c, carry):
        for mp, (q_ref, k_ref) in enumerate(((q1_ref, k1_ref), (q2_ref, k2_ref))):
            q = q_ref[...]
            acc = None
            lsum = None
            for sb in range(tk // ATTN_KEY_SUB):
                off = pl.multiple_of(c * tk + sb * ATTN_KEY_SUB, ATTN_KEY_SUB)
                part, d = sub_block(q, k_ref[pl.ds(off, ATTN_KEY_SUB), :],
                                    v_ref[pl.ds(off, ATTN_KEY_SUB), :])
                lsum = part if lsum is None else lsum + part
                acc = d if acc is None else acc + d
            l_ref[mp] += lsum
            acc_ref[mp] += acc
        return carry

    lax.fori_loop(0, n_keys // tk, body, 0)

    for mp, (q_ref, kc_ref) in enumerate(((q1_ref, k1c_ref), (q2_ref, k2c_ref))):
        part, d = sub_block(q_ref[...], kc_ref[...], vc_ref[...])
        l_ref[mp] += part
        acc_ref[mp] += d

    l1 = jnp.sum(l_ref[0], axis=-1, keepdims=True)
    l2 = jnp.sum(l_ref[1], axis=-1, keepdims=True)
    _attn_finalize(acc_ref[0], l1, acc_ref[1], l2, lam_ref, sw_ref, o_ref, lambda_init)


def _attn_call(qkvg_x, qkvg_c, lam_v, subln_w, lambda_init, *, bounded):
    b, seq, _ = qkvg_x.shape
    n_ctx = qkvg_c.shape[1]
    tq, tk = (ATTN_TQ, ATTN_TK) if bounded else (ATTN_TQ_ONLINE, ATTN_TK_ONLINE)
    assert seq % tq == 0 and seq % tk == 0 and tk % ATTN_KEY_SUB == 0 and n_ctx == ATTN_KEY_SUB
    d = DIFF_HEAD_DIM
    k_off = DIFF_WIDTH // d
    v_off = 2 * DIFF_WIDTH // DIFF_VAL_DIM
    in_specs = [
        pl.BlockSpec((None, tq, d), lambda bi, h, i: (bi, i, 2 * h)),
        pl.BlockSpec((None, tq, d), lambda bi, h, i: (bi, i, 2 * h + 1)),
        pl.BlockSpec((None, seq, d), lambda bi, h, i: (bi, 0, k_off + 2 * h)),
        pl.BlockSpec((None, seq, d), lambda bi, h, i: (bi, 0, k_off + 2 * h + 1)),
        pl.BlockSpec((None, seq, DIFF_VAL_DIM), lambda bi, h, i: (bi, 0, v_off + h)),
        pl.BlockSpec((None, n_ctx, d), lambda bi, h, i: (bi, 0, k_off + 2 * h)),
        pl.BlockSpec((None, n_ctx, d), lambda bi, h, i: (bi, 0, k_off + 2 * h + 1)),
        pl.BlockSpec((None, n_ctx, DIFF_VAL_DIM), lambda bi, h, i: (bi, 0, v_off + h)),
        pl.BlockSpec(lam_v.shape, lambda bi, h, i: (0, 0)),
        pl.BlockSpec((1, DIFF_VAL_DIM), lambda bi, h, i: (0, 0)),
    ]
    if bounded:
        body = functools.partial(_attn_bounded_kernel, tk=tk, lambda_init=lambda_init)
        scratch = [pltpu.VMEM((2, tq, V7X_LANES), F32), pltpu.VMEM((2, tq, DIFF_VAL_DIM), F32)]
    else:
        body = functools.partial(_attn_online_kernel, tk=tk, lambda_init=lambda_init)
        scratch = [pltpu.VMEM((2, tq, 1), F32), pltpu.VMEM((2, tq, 1), F32),
                   pltpu.VMEM((2, tq, DIFF_VAL_DIM), F32)]
    return pl.pallas_call(
        body,
        grid=(b, DIFF_HEADS, seq // tq),
        in_specs=in_specs,
        out_specs=pl.BlockSpec((None, tq, DIFF_VAL_DIM), lambda bi, h, i: (bi, i, h)),
        out_shape=jax.ShapeDtypeStruct((b, seq, DIFF_WIDTH), BF16),
        scratch_shapes=scratch,
        compiler_params=_params(("arbitrary", "arbitrary", "arbitrary")),
        name="diff_attn_bounded" if bounded else "diff_attn_online",
    )(qkvg_x, qkvg_x, qkvg_x, qkvg_x, qkvg_x, qkvg_c, qkvg_c, qkvg_c,
      lam_v, subln_w.reshape(1, DIFF_VAL_DIM))


def _axial_rope_tables(n_tokens):
    rows_n = n_tokens // GRID_W
    half = DIFF_HEAD_DIM // 2
    inv_freq = (ROPE_BASE ** (-np.arange(0, half, 2, dtype=np.float32) / half)).astype(np.float32)
    ang_r = np.arange(rows_n, dtype=np.float32)[:, None] * inv_freq
    ang_c = np.arange(GRID_W, dtype=np.float32)[:, None] * inv_freq

    def expand(fn):
        tr, tc = fn(ang_r), fn(ang_c)
        by_row = np.concatenate([tr, tr, np.zeros_like(tr), np.zeros_like(tr)], axis=-1)
        by_col = np.concatenate([np.zeros_like(tc), np.zeros_like(tc), tc, tc], axis=-1)
        full = jnp.asarray(by_row)[:, None, :] + jnp.asarray(by_col)[None, :, :]
        return full.reshape(n_tokens, DIFF_HEAD_DIM)

    return expand(np.cos), expand(np.sin)


def kernel(x, c, ctx, c_ctx, norm_w, ada_w, ada_b, gla_w_in, gla_gate_a1, gla_gate_a2, gla_gate_b,
           gla_gn_w, gla_w_out, diff_w_in, diff_qn_w, diff_kn_w, diff_lam, diff_subln_w, diff_w_out):
    b, seq, d = x.shape
    ctx_len = ctx.shape[1]
    assert ctx_len == GLA_BLOCK and seq % GLA_BLOCK == 0 and d == D_MODEL

    cond = jnp.zeros((ADA_ROWS, d), F32).at[:b].set(c).at[b].set(c_ctx)
    mod = _ada_call(cond, ada_w, ada_b)

    def mod_rows(layer, part):
        m = mod[layer, :, part * d:(part + 1) * d]
        lat = m[:b, None, :]
        return lat, m[b][None, None, :]

    shift, shift_c = mod_rows(0, 0)
    scale, scale_c = mod_rows(0, 1)
    gate, gate_c = mod_rows(0, 2)
    nw0 = norm_w[0].reshape(1, d)
    w_in0 = gla_w_in[0].astype(BF16)
    a1 = jnp.zeros((d, V7X_LANES), F32)
    a1 = a1.at[:, :GLA_GATE_RANK].set(gla_gate_a1[0, 0]).at[:, GLA_GATE_RANK:2 * GLA_GATE_RANK].set(
        gla_gate_a1[0, 1]).astype(BF16)
    a2 = jnp.zeros((V7X_LANES, 2 * GLA_KEY_DIM), F32)
    a2 = a2.at[:GLA_GATE_RANK, :GLA_KEY_DIM].set(gla_gate_a2[0, 0])
    a2 = a2.at[GLA_GATE_RANK:2 * GLA_GATE_RANK, GLA_KEY_DIM:].set(gla_gate_a2[0, 1]).astype(BF16)
    gb = gla_gate_b[0].reshape(1, 2 * GLA_KEY_DIM)
    gla_extras = [a1, a2, gb]
    tm0, tn0 = GLA_INPROJ_TILE
    la_chunk = 2 * GLA_KEY_DIM // (GLA_IN_DIM // tn0)
    assert la_chunk % V7X_LANES == 0
    gla_extra_specs = [
        pl.BlockSpec(a1.shape, lambda bi, i, j: (0, 0)),
        pl.BlockSpec((V7X_LANES, la_chunk), lambda bi, i, j: (0, j)),
        pl.BlockSpec((1, la_chunk), lambda bi, i, j: (0, j)),
    ]
    la_out = (2 * GLA_KEY_DIM, BF16)
    z_scratch = [((V7X_LANES,), BF16)]
    qkvg0_x, la0_x = _inproj_call(
        _gla_inproj_kernel, x, shift, scale, nw0, w_in0, gla_extras, gla_extra_specs,
        tm=tm0, tn=tn0, extra_out=la_out, extra_scratch=z_scratch, name="gla_inproj_x")
    ctx_rows = b * ctx_len
    ctx_m = ctx.reshape(1, ctx_rows, d)

    def per_batch(a):
        return a.reshape(b, ctx_len, a.shape[-1])

    def merged(a):
        return a.reshape(1, ctx_rows, a.shape[-1])

    qkvg0_c, la0_c = _inproj_call(
        _gla_inproj_kernel, ctx_m, shift_c, scale_c, nw0, w_in0, gla_extras, gla_extra_specs,
        tm=ctx_rows, tn=tn0, extra_out=la_out, extra_scratch=z_scratch, name="gla_inproj_ctx")

    o_f, o_b, oc_f, oc_b = _gla_scan_call(qkvg0_x, la0_x, per_batch(qkvg0_c), per_batch(la0_c))

    w_out0 = gla_w_out[0].astype(BF16)
    gn_w = gla_gn_w[0].reshape(1, GLA_DV)
    g_col_blk0 = (2 * GLA_KEY_DIM + GLA_VAL_DIM) // GLA_VAL_DIM
    x1 = _finish_call(x, gate, [o_f, o_b], qkvg0_x, g_col_blk0, w_out0, gn_w=gn_w,
                      name="gla_finish_x")
    ctx1 = _finish_call(ctx_m, gate_c, [merged(oc_f), merged(oc_b)], qkvg0_c, g_col_blk0, w_out0,
                        gn_w=gn_w, name="gla_finish_ctx")

    lambda_init = 0.8 - 0.6 * math.exp(-0.3 * 1)
    shift, shift_c = mod_rows(1, 0)
    scale, scale_c = mod_rows(1, 1)
    gate, _ = mod_rows(1, 2)
    nw1 = norm_w[1].reshape(1, d)
    w_in1 = diff_w_in[0].astype(BF16)
    cos, sin = _axial_rope_tables(seq)
    q_scale = math.log2(math.e) * DIFF_HEAD_DIM ** -0.5
    tm1, tn1 = DIFF_INPROJ_TILE
    head_mats = _head_matrices()
    nrm_args = [diff_qn_w[0].reshape(1, DIFF_VAL_DIM), diff_kn_w[0].reshape(1, DIFF_VAL_DIM)]
    nrm_specs = [
        pl.BlockSpec((1, DIFF_VAL_DIM), lambda bi, i, j: (0, 0)),
        pl.BlockSpec((1, DIFF_VAL_DIM), lambda bi, i, j: (0, 0)),
    ]
    hm_spec = pl.BlockSpec(head_mats.shape, lambda bi, i, j: (0, 0, 0))
    (qkvg1_x,) = _inproj_call(
        functools.partial(_diff_inproj_kernel, rope=True, tn=tn1, q_scale=q_scale),
        x1, shift, scale, nw1, w_in1, nrm_args + [cos, sin, head_mats],
        nrm_specs + [pl.BlockSpec((tm1, DIFF_HEAD_DIM), lambda bi, i, j: (i, 0)),
                     pl.BlockSpec((tm1, DIFF_HEAD_DIM), lambda bi, i, j: (i, 0)), hm_spec],
        tm=tm1, tn=tn1, name="diff_inproj_x")
    (qkvg1_c,) = _inproj_call(
        functools.partial(_diff_inproj_kernel, rope=False, tn=tn1, q_scale=q_scale),
        ctx1, shift_c, scale_c, nw1, w_in1, nrm_args + [cos, sin, head_mats],
        nrm_specs + [pl.BlockSpec((ctx_rows, DIFF_HEAD_DIM), lambda bi, i, j: (0, 0)),
                     pl.BlockSpec((ctx_rows, DIFF_HEAD_DIM), lambda bi, i, j: (0, 0)), hm_spec],
        tm=ctx_rows, tn=tn1, name="diff_inproj_ctx")
    qkvg1_c = per_batch(qkvg1_c)

    score_bound = q_scale * DIFF_HEAD_DIM * jnp.max(
        jnp.max(jnp.abs(diff_qn_w[0]), axis=-1) * jnp.max(jnp.abs(diff_kn_w[0]), axis=-1))
    attn = functools.partial(_attn_call, qkvg1_x, qkvg1_c, diff_lam[0], diff_subln_w[0], lambda_init)
    o_attn = lax.cond(score_bound <= ATTN_SCORE_BOUND,
                      functools.partial(attn, bounded=True), functools.partial(attn, bounded=False))

    w_out1 = diff_w_out[0].astype(BF16)
    g_col_blk1 = 3 * DIFF_WIDTH // DIFF_WIDTH
    return _finish_call(x1, gate, [o_attn], qkvg1_x, g_col_blk1, w_out1, name="diff_finish_x")
```

```python
import functools
import math

import jax
import jax.numpy as jnp
import numpy as np
from jax import lax
from jax.experimental import pallas as pl
from jax.experimental.pallas import tpu as pltpu

F32 = jnp.float32
BF16 = jnp.bfloat16

D_MODEL = 2048
DEPTH = 2
GRID_W = 64
EPS = 1e-6
GLA_HEADS = 4
GLA_KEY_DIM = D_MODEL // 2
GLA_VAL_DIM = D_MODEL
GLA_DK = GLA_KEY_DIM // GLA_HEADS
GLA_DV = GLA_VAL_DIM // GLA_HEADS
GLA_GATE_RANK = 16
GLA_GATE_NORM = 16.0
GLA_CHUNK = 64
GLA_IN_DIM = 2 * GLA_KEY_DIM + 2 * GLA_VAL_DIM
DIFF_HEAD_DIM = 128
DIFF_HEADS = D_MODEL // (2 * DIFF_HEAD_DIM)
DIFF_VAL_DIM = 2 * DIFF_HEAD_DIM
DIFF_WIDTH = DIFF_HEADS * DIFF_VAL_DIM
DIFF_IN_DIM = 4 * DIFF_WIDTH
ROPE_BASE = 10000.0

V7X_LANES = 128
V7X_SUBLANES = 8
V7X_MXU_DIM = 256
VMEM_LIMIT = 56 * 1024 * 1024

ADA_ROWS = V7X_SUBLANES
ADA_TN = 1024
GLA_BLOCK = 256
GLA_STEP_BLOCKS = 4
GLA_INPROJ_TILE = (1024, 768)
DIFF_INPROJ_TILE = (1024, 1024)
FINISH_TM = 512
ATTN_TQ = 1024
ATTN_TK = 8192
ATTN_TQ_ONLINE = 512
ATTN_TK_ONLINE = 1024
ATTN_KEY_SUB = V7X_MXU_DIM
ATTN_SCORE_BOUND = 60.0


def _dot(a, b):
    return jnp.dot(a, b, preferred_element_type=F32)


def _dot_nt(a, b):
    return lax.dot_general(a, b, (((1,), (1,)), ((), ())), preferred_element_type=F32)


def _params(sem):
    return pltpu.CompilerParams(dimension_semantics=sem, vmem_limit_bytes=VMEM_LIMIT)


def _ada_kernel(c_ref, w_ref, b_ref, o_ref):
    c = c_ref[...]
    s = c * jax.nn.sigmoid(c)
    o_ref[...] = jnp.dot(s, w_ref[...], preferred_element_type=F32,
                         precision=lax.Precision.HIGHEST) + b_ref[...]


def _ada_call(cond, ada_w, ada_b):
    tn = ADA_TN
    n = ada_w.shape[-1]
    return pl.pallas_call(
        _ada_kernel,
        grid=(DEPTH, n // tn),
        in_specs=[
            pl.BlockSpec((ADA_ROWS, D_MODEL), lambda l, j: (0, 0)),
            pl.BlockSpec((None, D_MODEL, tn), lambda l, j: (l, 0, j)),
            pl.BlockSpec((None, 1, tn), lambda l, j: (l, 0, j)),
        ],
        out_specs=pl.BlockSpec((None, ADA_ROWS, tn), lambda l, j: (l, 0, j)),
        out_shape=jax.ShapeDtypeStruct((DEPTH, ADA_ROWS, n), F32),
        compiler_params=_params(("arbitrary", "arbitrary")),
        name="ada_mod",
    )(cond, ada_w, ada_b.reshape(DEPTH, 1, n))


def _normalize_into(x_ref, shift_ref, scale_ref, nw_ref, hn_ref):
    x = x_ref[...]
    ms = jnp.mean(x * x, axis=-1, keepdims=True)
    a = nw_ref[...] * (1.0 + scale_ref[...])
    hn_ref[...] = (x * lax.rsqrt(ms + EPS) * a + shift_ref[...]).astype(BF16)


def _gla_inproj_kernel(x_ref, shift_ref, scale_ref, nw_ref, w_ref, a1_ref, a2_ref, gb_ref,
                       out_ref, la_ref, hn_ref, z_ref):
    j = pl.program_id(2)

    @pl.when(j == 0)
    def _():
        _normalize_into(x_ref, shift_ref, scale_ref, nw_ref, hn_ref)
        z_ref[...] = _dot(hn_ref[...], a1_ref[...]).astype(BF16)

    lg = _dot(z_ref[...], a2_ref[...]) + gb_ref[...]
    acc = _dot(hn_ref[...], w_ref[...])
    log_sig = jnp.minimum(lg, 0.0) - jnp.log(1.0 + jnp.exp(-jnp.abs(lg)))
    la_ref[...] = (log_sig * (1.0 / GLA_GATE_NORM)).astype(la_ref.dtype)
    out_ref[...] = acc.astype(BF16)


def _head_matrices():
    qd = DIFF_HEAD_DIM // 4
    e = np.arange(qd)
    rot = np.zeros((DIFF_HEAD_DIM, DIFF_HEAD_DIM), np.float32)
    rot[qd + e, e] = -1.0
    rot[e, qd + e] = 1.0
    rot[3 * qd + e, 2 * qd + e] = -1.0
    rot[2 * qd + e, 3 * qd + e] = 1.0
    both_maps = np.eye(2, dtype=np.float32)
    mean = np.kron(both_maps, np.full((DIFF_HEAD_DIM, DIFF_HEAD_DIM), 1.0 / DIFF_HEAD_DIM, np.float32))
    return jnp.asarray(np.stack([mean, np.kron(both_maps, rot)]), BF16)


def _diff_inproj_kernel(x_ref, shift_ref, scale_ref, nw_ref, w_ref, qn_ref, kn_ref,
                        cos_ref, sin_ref, hm_ref, out_ref, hn_ref, *, rope, tn, q_scale):
    j = pl.program_id(2)
    tiles_per_part = DIFF_WIDTH // tn

    @pl.when(j == 0)
    def _():
        _normalize_into(x_ref, shift_ref, scale_ref, nw_ref, hn_ref)

    acc = _dot(hn_ref[...], w_ref[...])

    def head_norm(nrm_ref, scale):
        w2 = nrm_ref[...] * scale
        if rope:
            cos2 = jnp.concatenate([cos_ref[...], cos_ref[...]], axis=1)
            sin2 = jnp.concatenate([sin_ref[...], sin_ref[...]], axis=1)
        for head in range(tn // DIFF_VAL_DIM):
            cols = slice(head * DIFF_VAL_DIM, (head + 1) * DIFF_VAL_DIM)
            t2 = acc[:, cols]
            ms = _dot((t2 * t2).astype(BF16), hm_ref[0])
            u = t2 * w2
            if rope:
                u = u * cos2 + _dot(u.astype(BF16), hm_ref[1]) * sin2
            out_ref[:, cols] = (u * lax.rsqrt(ms + EPS)).astype(BF16)

    @pl.when(j < tiles_per_part)
    def _():
        head_norm(qn_ref, q_scale)

    @pl.when((j >= tiles_per_part) & (j < 2 * tiles_per_part))
    def _():
        head_norm(kn_ref, 1.0)

    @pl.when(j >= 2 * tiles_per_part)
    def _():
        out_ref[...] = acc.astype(BF16)


def _inproj_call(kernel, x, shift, scale, nw, w, extras, extra_specs, *, tm, tn,
                 extra_out=None, extra_scratch=(), name):
    b, t, d = x.shape
    n = w.shape[-1]
    assert t % tm == 0 and n % tn == 0
    in_specs = [
        pl.BlockSpec((None, tm, d), lambda bi, i, j: (bi, i, 0)),
        pl.BlockSpec((None, 1, d), lambda bi, i, j: (bi, 0, 0)),
        pl.BlockSpec((None, 1, d), lambda bi, i, j: (bi, 0, 0)),
        pl.BlockSpec((1, d), lambda bi, i, j: (0, 0)),
        pl.BlockSpec((d, tn), lambda bi, i, j: (0, j)),
    ] + extra_specs
    out_specs = [pl.BlockSpec((None, tm, tn), lambda bi, i, j: (bi, i, j))]
    out_shape = [jax.ShapeDtypeStruct((b, t, n), BF16)]
    if extra_out is not None:
        width, dtype = extra_out
        out_specs.append(pl.BlockSpec((None, tm, width // (n // tn)), lambda bi, i, j: (bi, i, j)))
        out_shape.append(jax.ShapeDtypeStruct((b, t, width), dtype))
    return pl.pallas_call(
        kernel,
        grid=(b, t // tm, n // tn),
        in_specs=in_specs,
        out_specs=out_specs,
        out_shape=out_shape,
        scratch_shapes=[pltpu.VMEM((tm, d), BF16)] + [pltpu.VMEM((tm,) + tuple(s), dt)
                                                      for s, dt in extra_scratch],
        compiler_params=_params(("arbitrary", "arbitrary", "arbitrary")),
        name=name,
    )(x, shift, scale, nw, w, *extras)


def _gla_tri_matrices():
    idx = jnp.arange(GLA_BLOCK)
    i, j = idx[:, None], idx[None, :]
    same = (i // GLA_CHUNK) == (j // GLA_CHUNK)
    lower, upper = j <= i, j >= i
    return jnp.stack([jnp.stack([lower, lower & same]),
                      jnp.stack([upper, upper & same])]).astype(BF16)


def _dot_tn(a, b):
    return lax.dot_general(a, b, (((0,), (0,)), ((), ())), preferred_element_type=F32)


def _gla_operands(q_ref, k_ref, v_ref, rows, b, direction):
    fwd = direction == 0
    cs = GLA_CHUNK
    n_sub = GLA_BLOCK // cs
    tot = b[GLA_BLOCK - 1:GLA_BLOCK] if fwd else b[0:1]

    def per_chunk_rows(row_of_chunk):
        return jnp.concatenate(
            [jnp.broadcast_to(b[row_of_chunk(c):row_of_chunk(c) + 1], (cs, GLA_DK))
             for c in range(n_sub)], axis=0)

    if fwd:
        mid = per_chunk_rows(lambda c: c * cs + cs // 2 - 1)
        edge = per_chunk_rows(lambda c: max(c * cs - 1, 0))
    else:
        mid = per_chunk_rows(lambda c: c * cs + cs // 2)
        edge = per_chunk_rows(lambda c: min((c + 1) * cs, GLA_BLOCK - 1))

    q = q_ref[rows, :].astype(F32) * (GLA_DK ** -0.5)
    k = k_ref[rows, :].astype(F32)
    ops = dict(
        v=v_ref[rows, :],
        decay=jnp.exp(tot),
        q_in=(q * jnp.exp(b)).astype(BF16),
        k_in=(k * jnp.exp(tot - b)).astype(BF16),
        q_d=(q * jnp.exp(b - mid)).astype(BF16),
        k_d=(k * jnp.exp(mid - b)).astype(BF16),
        q_o=(q * jnp.exp(b - edge)).astype(BF16),
        k_o=[],
    )
    tok = lax.broadcasted_iota(jnp.int32, (GLA_BLOCK, 1), 0)
    for c in range(n_sub):
        lo, hi = c * cs, (c + 1) * cs
        first, last, ref_row = (0, lo, lo - 1) if fwd else (hi, GLA_BLOCK, hi)
        if last > first:
            expo = jnp.where((tok >= first) & (tok < last), b[ref_row:ref_row + 1] - b, -1e30)
            ops["k_o"].append((k * jnp.exp(expo)).astype(BF16))
        else:
            ops["k_o"].append(None)
    return ops


def _gla_scores(ops, same_chunk):
    cs = GLA_CHUNK
    scores = jnp.where(same_chunk > 0, _dot_nt(ops["q_d"], ops["k_d"]), 0.0)
    off = [jnp.zeros((cs, GLA_BLOCK), F32) if k_o is None
           else _dot_nt(ops["q_o"][c * cs:(c + 1) * cs], k_o)
           for c, k_o in enumerate(ops["k_o"])]
    return (scores + jnp.concatenate(off, axis=0)).astype(BF16)


def _gla_blocks(items, tri_ref, st_refs, o_refs):
    decays = [_dot(tri_ref[d, 0], refs[3][rows, :]) for d, refs, rows in items]
    ops = [_gla_operands(refs[0], refs[1], refs[2], rows, b, d)
           for (d, refs, rows), b in zip(items, decays)]
    scores = [_gla_scores(op, tri_ref[d, 1]) for (d, _, _), op in zip(items, ops)]
    intra = [_dot(s, op["v"]) for s, op in zip(scores, ops)]
    update = [_dot_tn(op["v"], op["k_in"]) for op in ops]

    state = [st_refs[0][...], st_refs[1][...]]
    outs = ([], [])
    for (d, _, rows), op, pv, upd in zip(items, ops, intra, update):
        o = _dot_nt(op["q_in"], state[d].astype(BF16)) + pv
        outs[d].append((rows.start, o.astype(BF16)))
        state[d] = state[d] * op["decay"] + upd
    for d in range(2):
        st_refs[d][...] = state[d]
        o_refs[d][...] = jnp.concatenate([o for _, o in sorted(outs[d], key=lambda p: p[0])], axis=0)


def _gla_scan_kernel(qf_ref, kf_ref, vf_ref, laf_ref, qb_ref, kb_ref, vb_ref, lab_ref,
                     qc_ref, kc_ref, vc_ref, lacf_ref, lacb_ref, tri_ref,
                     of_ref, ob_ref, ocf_ref, ocb_ref, sf_ref, sb_ref):
    t = pl.program_id(2)
    st_refs = (sf_ref, sb_ref)

    @pl.when(t == 0)
    def _():
        sf_ref[...] = jnp.zeros_like(sf_ref)
        sb_ref[...] = jnp.zeros_like(sb_ref)
        whole = slice(0, GLA_BLOCK)
        _gla_blocks([(0, (qc_ref, kc_ref, vc_ref, lacf_ref), whole),
                     (1, (qc_ref, kc_ref, vc_ref, lacb_ref), whole)],
                    tri_ref, st_refs, (ocf_ref, ocb_ref))

    @pl.when(t > 0)
    def _():
        blocks = [slice(s * GLA_BLOCK, (s + 1) * GLA_BLOCK) for s in range(GLA_STEP_BLOCKS)]
        items = []
        for rows_f, rows_b in zip(blocks, reversed(blocks)):
            items.append((0, (qf_ref, kf_ref, vf_ref, laf_ref), rows_f))
            items.append((1, (qb_ref, kb_ref, vb_ref, lab_ref), rows_b))
        _gla_blocks(items, tri_ref, st_refs, (of_ref, ob_ref))


def _gla_scan_call(qkvg_x, la_x, qkvg_c, la_c):
    b, seq, _ = qkvg_x.shape
    step_rows = GLA_BLOCK * GLA_STEP_BLOCKS
    nb = seq // step_rows
    assert qkvg_c.shape[1] == GLA_BLOCK and seq % step_rows == 0
    k_off = GLA_KEY_DIM // GLA_DK
    v_off = 2 * GLA_KEY_DIM // GLA_DV

    def pos_f(t):
        return jnp.maximum(t - 1, 0)

    def pos_b(t):
        return nb - jnp.maximum(t, 1)

    def pos_c(t):
        return 0

    def specs(pos, rows, la_col_off, with_la=True):
        out = [
            pl.BlockSpec((None, rows, GLA_DK), lambda bi, h, t: (bi, pos(t), h)),
            pl.BlockSpec((None, rows, GLA_DK), lambda bi, h, t: (bi, pos(t), k_off + h)),
            pl.BlockSpec((None, rows, GLA_DV), lambda bi, h, t: (bi, pos(t), v_off + h)),
        ]
        if with_la:
            out.append(la_spec(pos, rows, la_col_off))
        return out

    def la_spec(pos, rows, la_col_off):
        return pl.BlockSpec((None, rows, GLA_DK), lambda bi, h, t: (bi, pos(t), la_col_off + h))

    def o_spec(pos, rows):
        return pl.BlockSpec((None, rows, GLA_DV), lambda bi, h, t: (bi, pos(t), h))

    tri = _gla_tri_matrices()
    in_specs = (specs(pos_f, step_rows, 0) + specs(pos_b, step_rows, GLA_HEADS)
                + specs(pos_c, GLA_BLOCK, 0, with_la=False)
                + [la_spec(pos_c, GLA_BLOCK, 0), la_spec(pos_c, GLA_BLOCK, GLA_HEADS),
                   pl.BlockSpec(tri.shape, lambda bi, h, t: (0, 0, 0, 0))])
    ox_shape = jax.ShapeDtypeStruct((b, seq, GLA_VAL_DIM), BF16)
    oc_shape = jax.ShapeDtypeStruct((b, GLA_BLOCK, GLA_VAL_DIM), BF16)
    return pl.pallas_call(
        _gla_scan_kernel,
        grid=(b, GLA_HEADS, nb + 1),
        in_specs=in_specs,
        out_specs=[o_spec(pos_f, step_rows), o_spec(pos_b, step_rows),
                   o_spec(pos_c, GLA_BLOCK), o_spec(pos_c, GLA_BLOCK)],
        out_shape=[ox_shape, ox_shape, oc_shape, oc_shape],
        scratch_shapes=[pltpu.VMEM((GLA_DV, GLA_DK), F32), pltpu.VMEM((GLA_DV, GLA_DK), F32)],
        compiler_params=_params(("arbitrary", "arbitrary", "arbitrary")),
        name="gla_scan",
    )(qkvg_x, qkvg_x, qkvg_x, la_x, qkvg_x, qkvg_x, qkvg_x, la_x,
      qkvg_c, qkvg_c, qkvg_c, la_c, la_c, tri)


def _finish_kernel(*refs, head_norm):
    if head_norm:
        x_ref, gate_ref, o1_ref, o2_ref, g_ref, gn_ref, w_ref, out_ref = refs
        o = o1_ref[...].astype(F32) + o2_ref[...].astype(F32)
        parts = []
        for h in range(GLA_HEADS):
            oh = o[:, h * GLA_DV:(h + 1) * GLA_DV]
            ms = jnp.mean(oh * oh, axis=-1, keepdims=True)
            parts.append(oh * lax.rsqrt(ms + EPS) * gn_ref[...])
        o = jnp.concatenate(parts, axis=-1)
    else:
        x_ref, gate_ref, o1_ref, g_ref, w_ref, out_ref = refs
        o = o1_ref[...].astype(F32)
    half_g = 0.5 * g_ref[...].astype(F32)
    silu_g = half_g * (1.0 + jnp.tanh(half_g))
    y_in = (o * silu_g).astype(BF16)
    y = _dot(y_in, w_ref[...])
    out_ref[...] = x_ref[...] + gate_ref[...] * y


def _finish_call(x, gate, o_list, g_arr, g_col_blk, w, *, gn_w=None, name):
    b, t, d = x.shape
    tm = min(FINISH_TM, t)
    assert t % tm == 0
    width = w.shape[0]
    in_specs = [
        pl.BlockSpec((None, tm, d), lambda bi, i: (bi, i, 0)),
        pl.BlockSpec((None, 1, d), lambda bi, i: (bi, 0, 0)),
    ]
    args = [x, gate]
    for o in o_list:
        in_specs.append(pl.BlockSpec((None, tm, width), lambda bi, i: (bi, i, 0)))
        args.append(o)
    in_specs.append(pl.BlockSpec((None, tm, width), lambda bi, i: (bi, i, g_col_blk)))
    args.append(g_arr)
    if gn_w is not None:
        in_specs.append(pl.BlockSpec((1, gn_w.shape[-1]), lambda bi, i: (0, 0)))
        args.append(gn_w)
    in_specs.append(pl.BlockSpec((width, d), lambda bi, i: (0, 0)))
    args.append(w)
    return pl.pallas_call(
        functools.partial(_finish_kernel, head_norm=gn_w is not None),
        grid=(b, t // tm),
        in_specs=in_specs,
        out_specs=pl.BlockSpec((None, tm, d), lambda bi, i: (bi, i, 0)),
        out_shape=jax.ShapeDtypeStruct((b, t, d), F32),
        compiler_params=_params(("arbitrary", "arbitrary")),
        name=name,
    )(*args)


def _attn_finalize(acc1, l1, acc2, l2, lam_ref, sw_ref, o_ref, lambda_init):
    lv = lam_ref[...]
    lam = (jnp.exp(jnp.sum(lv[0:1] * lv[1:2], axis=-1, keepdims=True))
           - jnp.exp(jnp.sum(lv[2:3] * lv[3:4], axis=-1, keepdims=True)) + lambda_init)
    o = acc1 / l1 - lam * (acc2 / l2)
    ms = jnp.mean(o * o, axis=-1, keepdims=True)
    o = o * lax.rsqrt(ms + EPS) * sw_ref[...] * (1.0 - lambda_init)
    o_ref[...] = o.astype(BF16)


def _attn_online_kernel(q1_ref, q2_ref, k1_ref, k2_ref, v_ref, k1c_ref, k2c_ref, vc_ref,
                        lam_ref, sw_ref, o_ref, m_ref, l_ref, acc_ref, *, tk, lambda_init):
    n_keys = v_ref.shape[0]
    m_ref[...] = jnp.full_like(m_ref, -1e30)
    l_ref[...] = jnp.zeros_like(l_ref)
    acc_ref[...] = jnp.zeros_like(acc_ref)

    def update(mp, q_ref, k_blk, v_blk):
        s = _dot_nt(q_ref[...], k_blk)
        m_old = m_ref[mp]
        m_new = jnp.maximum(m_old, jnp.max(s, axis=1, keepdims=True))
        alpha = jnp.exp2(m_old - m_new)
        p = jnp.exp2(s - m_new)
        l_ref[mp] = alpha * l_ref[mp] + jnp.sum(p, axis=1, keepdims=True)
        acc_ref[mp] = alpha * acc_ref[mp] + _dot(p.astype(BF16), v_blk)
        m_ref[mp] = m_new

    def body(c, carry):
        off = pl.multiple_of(c * tk, tk)
        v_blk = v_ref[pl.ds(off, tk), :]
        update(0, q1_ref, k1_ref[pl.ds(off, tk), :], v_blk)
        update(1, q2_ref, k2_ref[pl.ds(off, tk), :], v_blk)
        return carry

    lax.fori_loop(0, n_keys // tk, body, 0)
    update(0, q1_ref, k1c_ref[...], vc_ref[...])
    update(1, q2_ref, k2c_ref[...], vc_ref[...])
    _attn_finalize(acc_ref[0], l_ref[0], acc_ref[1], l_ref[1], lam_ref, sw_ref, o_ref, lambda_init)


def _attn_bounded_kernel(q1_ref, q2_ref, k1_ref, k2_ref, v_ref, k1c_ref, k2c_ref, vc_ref,
                         lam_ref, sw_ref, o_ref, l_ref, acc_ref, *, tk, lambda_init):
    n_keys = v_ref.shape[0]
    l_ref[...] = jnp.zeros_like(l_ref)
    acc_ref[...] = jnp.zeros_like(acc_ref)

    def sub_block(q, k_blk, v_blk):
        s = _dot_nt(q, k_blk)
        p = jnp.exp2(s)
        part = p[:, :V7X_LANES] + p[:, V7X_LANES:]
        return part, _dot(p.astype(BF16), v_blk)

    def body(c, carry):
        for mp, (q_ref, k_ref) in enumerate(((q1_ref, k1_ref), (q2_ref, k2_ref))):
            q = q_ref[...]
            acc = None
            lsum = None
            for sb in range(tk // ATTN_KEY_SUB):
                off = pl.multiple_of(c * tk + sb * ATTN_KEY_SUB, ATTN_KEY_SUB)
                part, d = sub_block(q, k_ref[pl.ds(off, ATTN_KEY_SUB), :],
                                    v_ref[pl.ds(off, ATTN_KEY_SUB), :])
                lsum = part if lsum is None else lsum + part
                acc = d if acc is None else acc + d
            l_ref[mp] += lsum
            acc_ref[mp] += acc
        return carry

    lax.fori_loop(0, n_keys // tk, body, 0)

    for mp, (q_ref, kc_ref) in enumerate(((q1_ref, k1c_ref), (q2_ref, k2c_ref))):
        part, d = sub_block(q_ref[...], kc_ref[...], vc_ref[...])
        l_ref[mp] += part
        acc_ref[mp] += d

    l1 = jnp.sum(l_ref[0], axis=-1, keepdims=True)
    l2 = jnp.sum(l_ref[1], axis=-1, keepdims=True)
    _attn_finalize(acc_ref[0], l1, acc_ref[1], l2, lam_ref, sw_ref, o_ref, lambda_init)


def _attn_call(qkvg_x, qkvg_c, lam_v, subln_w, lambda_init, *, bounded):
    b, seq, _ = qkvg_x.shape
    n_ctx = qkvg_c.shape[1]
    tq, tk = (ATTN_TQ, ATTN_TK) if bounded else (ATTN_TQ_ONLINE, ATTN_TK_ONLINE)
    assert seq % tq == 0 and seq % tk == 0 and tk % ATTN_KEY_SUB == 0 and n_ctx == ATTN_KEY_SUB
    d = DIFF_HEAD_DIM
    k_off = DIFF_WIDTH // d
    v_off = 2 * DIFF_WIDTH // DIFF_VAL_DIM
    in_specs = [
        pl.BlockSpec((None, tq, d), lambda bi, h, i: (bi, i, 2 * h)),
        pl.BlockSpec((None, tq, d), lambda bi, h, i: (bi, i, 2 * h + 1)),
        pl.BlockSpec((None, seq, d), lambda bi, h, i: (bi, 0, k_off + 2 * h)),
        pl.BlockSpec((None, seq, d), lambda bi, h, i: (bi, 0, k_off + 2 * h + 1)),
        pl.BlockSpec((None, seq, DIFF_VAL_DIM), lambda bi, h, i: (bi, 0, v_off + h)),
        pl.BlockSpec((None, n_ctx, d), lambda bi, h, i: (bi, 0, k_off + 2 * h)),
        pl.BlockSpec((None, n_ctx, d), lambda bi, h, i: (bi, 0, k_off + 2 * h + 1)),
        pl.BlockSpec((None, n_ctx, DIFF_VAL_DIM), lambda bi, h, i: (bi, 0, v_off + h)),
        pl.BlockSpec(lam_v.shape, lambda bi, h, i: (0, 0)),
        pl.BlockSpec((1, DIFF_VAL_DIM), lambda bi, h, i: (0, 0)),
    ]
    if bounded:
        body = functools.partial(_attn_bounded_kernel, tk=tk, lambda_init=lambda_init)
        scratch = [pltpu.VMEM((2, tq, V7X_LANES), F32), pltpu.VMEM((2, tq, DIFF_VAL_DIM), F32)]
    else:
        body = functools.partial(_attn_online_kernel, tk=tk, lambda_init=lambda_init)
        scratch = [pltpu.VMEM((2, tq, 1), F32), pltpu.VMEM((2, tq, 1), F32),
                   pltpu.VMEM((2, tq, DIFF_VAL_DIM), F32)]
    return pl.pallas_call(
        body,
        grid=(b, DIFF_HEADS, seq // tq),
        in_specs=in_specs,
        out_specs=pl.BlockSpec((None, tq, DIFF_VAL_DIM), lambda bi, h, i: (bi, i, h)),
        out_shape=jax.ShapeDtypeStruct((b, seq, DIFF_WIDTH), BF16),
        scratch_shapes=scratch,
        compiler_params=_params(("arbitrary", "arbitrary", "arbitrary")),
        name="diff_attn_bounded" if bounded else "diff_attn_online",
    )(qkvg_x, qkvg_x, qkvg_x, qkvg_x, qkvg_x, qkvg_c, qkvg_c, qkvg_c,
      lam_v, subln_w.reshape(1, DIFF_VAL_DIM))


def _axial_rope_tables(n_tokens):
    rows_n = n_tokens // GRID_W
    half = DIFF_HEAD_DIM // 2
    inv_freq = (ROPE_BASE ** (-np.arange(0, half, 2, dtype=np.float32) / half)).astype(np.float32)
    ang_r = np.arange(rows_n, dtype=np.float32)[:, None] * inv_freq
    ang_c = np.arange(GRID_W, dtype=np.float32)[:, None] * inv_freq

    def expand(fn):
        tr, tc = fn(ang_r), fn(ang_c)
        by_row = np.concatenate([tr, tr, np.zeros_like(tr), np.zeros_like(tr)], axis=-1)
        by_col = np.concatenate([np.zeros_like(tc), np.zeros_like(tc), tc, tc], axis=-1)
        full = jnp.asarray(by_row)[:, None, :] + jnp.asarray(by_col)[None, :, :]
        return full.reshape(n_tokens, DIFF_HEAD_DIM)

    return expand(np.cos), expand(np.sin)


def kernel(x, c, ctx, c_ctx, norm_w, ada_w, ada_b, gla_w_in, gla_gate_a1, gla_gate_a2, gla_gate_b,
           gla_gn_w, gla_w_out, diff_w_in, diff_qn_w, diff_kn_w, diff_lam, diff_subln_w, diff_w_out):
    b, seq, d = x.shape
    ctx_len = ctx.shape[1]
    assert ctx_len == GLA_BLOCK and seq % GLA_BLOCK == 0 and d == D_MODEL

    cond = jnp.zeros((ADA_ROWS, d), F32).at[:b].set(c).at[b].set(c_ctx)
    mod = _ada_call(cond, ada_w, ada_b)

    def mod_rows(layer, part):
        m = mod[layer, :, part * d:(part + 1) * d]
        lat = m[:b, None, :]
        return lat, m[b][None, None, :]

    shift, shift_c = mod_rows(0, 0)
    scale, scale_c = mod_rows(0, 1)
    gate, gate_c = mod_rows(0, 2)
    nw0 = norm_w[0].reshape(1, d)
    w_in0 = gla_w_in[0].astype(BF16)
    a1 = jnp.zeros((d, V7X_LANES), F32)
    a1 = a1.at[:, :GLA_GATE_RANK].set(gla_gate_a1[0, 0]).at[:, GLA_GATE_RANK:2 * GLA_GATE_RANK].set(
        gla_gate_a1[0, 1]).astype(BF16)
    a2 = jnp.zeros((V7X_LANES, 2 * GLA_KEY_DIM), F32)
    a2 = a2.at[:GLA_GATE_RANK, :GLA_KEY_DIM].set(gla_gate_a2[0, 0])
    a2 = a2.at[GLA_GATE_RANK:2 * GLA_GATE_RANK, GLA_KEY_DIM:].set(gla_gate_a2[0, 1]).astype(BF16)
    gb = gla_gate_b[0].reshape(1, 2 * GLA_KEY_DIM)
    gla_extras = [a1, a2, gb]
    tm0, tn0 = GLA_INPROJ_TILE
    la_chunk = 2 * GLA_KEY_DIM // (GLA_IN_DIM // tn0)
    assert la_chunk % V7X_LANES == 0
    gla_extra_specs = [
        pl.BlockSpec(a1.shape, lambda bi, i, j: (0, 0)),
        pl.BlockSpec((V7X_LANES, la_chunk), lambda bi, i, j: (0, j)),
        pl.BlockSpec((1, la_chunk), lambda bi, i, j: (0, j)),
    ]
    la_out = (2 * GLA_KEY_DIM, BF16)
    z_scratch = [((V7X_LANES,), BF16)]
    qkvg0_x, la0_x = _inproj_call(
        _gla_inproj_kernel, x, shift, scale, nw0, w_in0, gla_extras, gla_extra_specs,
        tm=tm0, tn=tn0, extra_out=la_out, extra_scratch=z_scratch, name="gla_inproj_x")
    ctx_rows = b * ctx_len
    ctx_m = ctx.reshape(1, ctx_rows, d)

    def per_batch(a):
        return a.reshape(b, ctx_len, a.shape[-1])

    def merged(a):
        return a.reshape(1, ctx_rows, a.shape[-1])

    qkvg0_c, la0_c = _inproj_call(
        _gla_inproj_kernel, ctx_m, shift_c, scale_c, nw0, w_in0, gla_extras, gla_extra_specs,
        tm=ctx_rows, tn=tn0, extra_out=la_out, extra_scratch=z_scratch, name="gla_inproj_ctx")

    o_f, o_b, oc_f, oc_b = _gla_scan_call(qkvg0_x, la0_x, per_batch(qkvg0_c), per_batch(la0_c))

    w_out0 = gla_w_out[0].astype(BF16)
    gn_w = gla_gn_w[0].reshape(1, GLA_DV)
    g_col_blk0 = (2 * GLA_KEY_DIM + GLA_VAL_DIM) // GLA_VAL_DIM
    x1 = _finish_call(x, gate, [o_f, o_b], qkvg0_x, g_col_blk0, w_out0, gn_w=gn_w,
                      name="gla_finish_x")
    ctx1 = _finish_call(ctx_m, gate_c, [merged(oc_f), merged(oc_b)], qkvg0_c, g_col_blk0, w_out0,
                        gn_w=gn_w, name="gla_finish_ctx")

    lambda_init = 0.8 - 0.6 * math.exp(-0.3 * 1)
    shift, shift_c = mod_rows(1, 0)
    scale, scale_c = mod_rows(1, 1)
    gate, _ = mod_rows(1, 2)
    nw1 = norm_w[1].reshape(1, d)
    w_in1 = diff_w_in[0].astype(BF16)
    cos, sin = _axial_rope_tables(seq)
    q_scale = math.log2(math.e) * DIFF_HEAD_DIM ** -0.5
    tm1, tn1 = DIFF_INPROJ_TILE
    head_mats = _head_matrices()
    nrm_args = [diff_qn_w[0].reshape(1, DIFF_VAL_DIM), diff_kn_w[0].reshape(1, DIFF_VAL_DIM)]
    nrm_specs = [
        pl.BlockSpec((1, DIFF_VAL_DIM), lambda bi, i, j: (0, 0)),
        pl.BlockSpec((1, DIFF_VAL_DIM), lambda bi, i, j: (0, 0)),
    ]
    hm_spec = pl.BlockSpec(head_mats.shape, lambda bi, i, j: (0, 0, 0))
    (qkvg1_x,) = _inproj_call(
        functools.partial(_diff_inproj_kernel, rope=True, tn=tn1, q_scale=q_scale),
        x1, shift, scale, nw1, w_in1, nrm_args + [cos, sin, head_mats],
        nrm_specs + [pl.BlockSpec((tm1, DIFF_HEAD_DIM), lambda bi, i, j: (i, 0)),
                     pl.BlockSpec((tm1, DIFF_HEAD_DIM), lambda bi, i, j: (i, 0)), hm_spec],
        tm=tm1, tn=tn1, name="diff_inproj_x")
    (qkvg1_c,) = _inproj_call(
        functools.partial(_diff_inproj_kernel, rope=False, tn=tn1, q_scale=q_scale),
        ctx1, shift_c, scale_c, nw1, w_in1, nrm_args + [cos, sin, head_mats],
        nrm_specs + [pl.BlockSpec((ctx_rows, DIFF_HEAD_DIM), lambda bi, i, j: (0, 0)),
                     pl.BlockSpec((ctx_rows, DIFF_HEAD_DIM), lambda bi, i, j: (0, 0)), hm_spec],
        tm=ctx_rows, tn=tn1, name="diff_inproj_ctx")
    qkvg1_c = per_batch(qkvg1_c)

    score_bound = q_scale * DIFF_HEAD_DIM * jnp.max(
        jnp.max(jnp.abs(diff_qn_w[0]), axis=-1) * jnp.max(jnp.abs(diff_kn_w[0]), axis=-1))
    attn = functools.partial(_attn_call, qkvg1_x, qkvg1_c, diff_lam[0], diff_subln_w[0], lambda_init)
    o_attn = lax.cond(score_bound <= ATTN_SCORE_BOUND,
                      functools.partial(attn, bounded=True), functools.partial(attn, bounded=False))

    w_out1 = diff_w_out[0].astype(BF16)
    g_col_blk1 = 3 * DIFF_WIDTH // DIFF_WIDTH
    return _finish_call(x1, gate, [o_attn], qkvg1_x, g_col_blk1, w_out1, name="diff_finish_x")
```
